```python
import math
import jax
import jax.numpy as jnp
from jax import lax
import numpy as np

D_MODEL = 1024
BATCH = 2
SEQ = 16384
DEPTH = 4

GRID_W = 64
CTX_LEN = 256
N_MOD = 6
NORM_EPS = 1e-6
N_DIFF_HEADS = 4
DIFF_HEAD_DIM = 64
DIFF_V_DIM = 2 * DIFF_HEAD_DIM
ATTN_QK = N_DIFF_HEADS * 2 * DIFF_HEAD_DIM
ATTN_V = N_DIFF_HEADS * DIFF_V_DIM
Q_BLOCK = 128
ROPE_BASE = 10000.0
S5_WIDTH = 512
S5_GROUP = 16
S5_GROUPS = S5_WIDTH // S5_GROUP
S5_STATE = 64
MLSTM_HEADS = 4
MLSTM_HEAD_DIM = 128
MLSTM_WIDTH = MLSTM_HEADS * MLSTM_HEAD_DIM
MLSTM_GATES = 2 * 2 * MLSTM_HEADS
MLSTM_CHUNK = 128
CONV_K = 5
N_BRANCH = 3
BRANCH_WIDTH = 512
IN_SIZES = (ATTN_QK, ATTN_QK, ATTN_V, S5_WIDTH, MLSTM_WIDTH, MLSTM_WIDTH, MLSTM_WIDTH, MLSTM_WIDTH, MLSTM_GATES, N_BRANCH * D_MODEL)
D_IN = 2 * ATTN_QK + ATTN_V + S5_WIDTH + 4 * MLSTM_WIDTH + MLSTM_GATES + N_BRANCH * D_MODEL
N_EXPERTS = 16
D_EXPERT = 2048
CAPACITY_FACTOR = 2

kernel_name = 'hybrid_diffattn_s5_mlstm_ecmoe_prefix_block'


def rms_norm(t):
    tf = t.astype(jnp.float32)
    return (tf * lax.rsqrt(jnp.mean(tf * tf, axis=-1, keepdims=True) + NORM_EPS)).astype(t.dtype)


def modulate(t, shift, scale):
    return rms_norm(t) * (1.0 + scale) + shift


def split_projection(p):
    cuts = [int(v) for v in np.cumsum(IN_SIZES)[:-1]]
    return jnp.split(p, cuts, axis=-1)


def axial_rope_tables(rows):
    n_freq = DIFF_HEAD_DIM // 4
    inv_freq = ROPE_BASE ** (-jnp.arange(n_freq, dtype=jnp.float32) / n_freq)
    t = jnp.arange(rows * GRID_W)
    row = (t // GRID_W).astype(jnp.float32)
    col = (t % GRID_W).astype(jnp.float32)
    ang = jnp.concatenate([row[:, None] * inv_freq, col[:, None] * inv_freq], axis=-1)
    return jnp.cos(ang), jnp.sin(ang)


def apply_rope(t, cos, sin):
    half = t.shape[-1] // 2
    t1, t2 = t[..., :half], t[..., half:]
    cs = cos[:, None, None, :].astype(t.dtype)
    sn = sin[:, None, None, :].astype(t.dtype)
    return jnp.concatenate([t1 * cs - t2 * sn, t2 * cs + t1 * sn], axis=-1)


def diff_softmax_mix(q, k, v, lam):
    s = jnp.einsum('bqhmd,bkhmd->bhmqk', q, k).astype(jnp.float32) * (DIFF_HEAD_DIM ** -0.5)
    p = jax.nn.softmax(s, axis=-1)
    a = p[:, :, 0] - lam * p[:, :, 1]
    return jnp.einsum('bhqk,bkhe->bqhe', a.astype(v.dtype), v)


def diff_attend_blocks(q, k, v, lam):
    b, t = q.shape[0], q.shape[1]
    nb = t // Q_BLOCK
    qb = jnp.swapaxes(q.reshape((b, nb, Q_BLOCK) + q.shape[2:]), 0, 1)
    ob = lax.map(lambda qq: diff_softmax_mix(qq, k, v, lam), qb)
    return jnp.swapaxes(ob, 0, 1).reshape((b, t) + ob.shape[3:])


def diff_attention_mixer(q_l, k_l, v_l, q_c, k_c, v_c, q_gain, k_gain, lam_vecs, out_gain, lam_init, cos, sin, with_ctx):
    def split_heads(t, g):
        t = t.reshape(t.shape[0], t.shape[1], N_DIFF_HEADS, 2, DIFF_HEAD_DIM)
        return rms_norm(t) * g

    def values(t):
        return t.reshape(t.shape[0], t.shape[1], N_DIFF_HEADS, DIFF_V_DIM)

    def finish(o):
        o = rms_norm(o) * out_gain * (1.0 - lam_init)
        return o.reshape(o.shape[0], o.shape[1], ATTN_V)

    lv = lam_vecs.astype(jnp.float32)
    lam = jnp.exp(jnp.sum(lv[0] * lv[1])) - jnp.exp(jnp.sum(lv[2] * lv[3])) + lam_init
    ql = apply_rope(split_heads(q_l, q_gain), cos, sin)
    kl = apply_rope(split_heads(k_l, k_gain), cos, sin)
    kc = split_heads(k_c, k_gain)
    vl, vc = values(v_l), values(v_c)
    k_all = jnp.concatenate([kc, kl], axis=1)
    v_all = jnp.concatenate([vc, vl], axis=1)
    y_l = finish(diff_attend_blocks(ql, k_all, v_all, lam))
    y_c = None
    if with_ctx:
        y_c = finish(diff_softmax_mix(split_heads(q_c, q_gain), kc, vc, lam))
    return y_l, y_c


def s5_discretize(lam_re, lam_im, log_step, b_re, b_im):
    dt = jnp.exp(log_step)[:, None]
    mag = jnp.exp(lam_re * dt)
    ab_re = mag * jnp.cos(lam_im * dt)
    ab_im = mag * jnp.sin(lam_im * dt)
    nr, ni = ab_re - 1.0, ab_im
    den = lam_re * lam_re + lam_im * lam_im
    f_re = (nr * lam_re + ni * lam_im) / den
    f_im = (ni * lam_re - nr * lam_im) / den
    bb_re = f_re[..., None] * b_re - f_im[..., None] * b_im
    bb_im = f_re[..., None] * b_im + f_im[..., None] * b_re
    return ab_re, ab_im, bb_re, bb_im


def complex_affine_combine(e1, e2):
    a1r, a1i, b1r, b1i = e1
    a2r, a2i, b2r, b2i = e2
    return (a2r * a1r - a2i * a1i,
            a2r * a1i + a2i * a1r,
            a2r * b1r - a2i * b1i + b2r,
            a2r * b1i + a2i * b1r + b2i)


def s5_states(u, ab_re, ab_im, bb_re, bb_im, h0):
    bu_re = jnp.einsum('gph,btgh->btgp', bb_re, u)
    bu_im = jnp.einsum('gph,btgh->btgp', bb_im, u)
    a_re = jnp.broadcast_to(ab_re, bu_re.shape)
    a_im = jnp.broadcast_to(ab_im, bu_im.shape)
    acc_re, acc_im, s_re, s_im = lax.associative_scan(complex_affine_combine, (a_re, a_im, bu_re, bu_im), axis=1)
    if h0 is not None:
        h_re, h_im = h0[0][:, None], h0[1][:, None]
        s_re = s_re + acc_re * h_re - acc_im * h_im
        s_im = s_im + acc_re * h_im + acc_im * h_re
    return s_re, s_im


def s5_readout(s_re, s_im, c_re, c_im):
    return jnp.einsum('ghp,btgp->btgh', c_re, s_re) - jnp.einsum('ghp,btgp->btgh', c_im, s_im)


def s5_mixer(u_l, u_c, lam_re, lam_im, log_step, b_re, b_im, c_re, c_im, d, w_glu, with_ctx):
    b, t, _ = u_l.shape
    ul = u_l.reshape(b, t, S5_GROUPS, S5_GROUP)
    uc = u_c.reshape(b, u_c.shape[1], S5_GROUPS, S5_GROUP)
    dg = d.reshape(S5_GROUPS, S5_GROUP)
    y_l = dg * ul
    y_c = dg * uc if with_ctx else None
    for di in range(2):
        rev = (lambda a: jnp.flip(a, axis=1)) if di == 1 else (lambda a: a)
        disc = s5_discretize(lam_re[di], lam_im[di], log_step[di], b_re[di], b_im[di])
        sc_re, sc_im = s5_states(rev(uc), *disc, None)
        sl_re, sl_im = s5_states(rev(ul), *disc, (sc_re[:, -1], sc_im[:, -1]))
        y_l = y_l + rev(s5_readout(sl_re, sl_im, c_re[di], c_im[di]))
        if with_ctx:
            y_c = y_c + rev(s5_readout(sc_re, sc_im, c_re[di], c_im[di]))

    def glu(y):
        g = jax.nn.gelu(y.reshape(y.shape[0], y.shape[1], S5_WIDTH))
        return g * jax.nn.sigmoid(g @ w_glu)

    return glu(y_l), (glu(y_c) if with_ctx else None)


def short_conv(t, w, bias):
    y = lax.conv_general_dilated(t, w[:, None, :].astype(t.dtype), (1,), 'SAME',
                                 dimension_numbers=('NWC', 'WIO', 'NWC'),
                                 feature_group_count=t.shape[-1])
    return jax.nn.silu(y + bias)


def mlstm_prepare(q, k, v, gates, conv_w, conv_b, i_bias, f_bias, need_q):
    b, t, _ = k.shape

    def heads(a):
        return a.reshape(b, t, MLSTM_HEADS, MLSTM_HEAD_DIM).transpose(0, 2, 1, 3)

    kh = heads(short_conv(k, conv_w[:, MLSTM_WIDTH:], conv_b[MLSTM_WIDTH:])) * (MLSTM_HEAD_DIM ** -0.5)
    qh = heads(short_conv(q, conv_w[:, :MLSTM_WIDTH], conv_b[:MLSTM_WIDTH])) if need_q else None
    g = gates.astype(jnp.float32).reshape(b, t, 2, 2, MLSTM_HEADS)
    ig = (g[:, :, :, 0] + i_bias).transpose(2, 0, 3, 1)
    logf = jax.nn.log_sigmoid(g[:, :, :, 1] + f_bias).transpose(2, 0, 3, 1)
    return qh, kh, heads(v), ig, logf


def mlstm_chunked(q, k, v, ig, logf, state, return_h):
    b, h, t, dh = k.shape
    nc = t // MLSTM_CHUNK
    kc = k.reshape(b, h, nc, MLSTM_CHUNK, dh)
    vc = v.reshape(b, h, nc, MLSTM_CHUNK, dh)
    ig = ig.reshape(b, h, nc, MLSTM_CHUNK)
    cum = jnp.cumsum(logf.reshape(b, h, nc, MLSTM_CHUNK), axis=-1)
    g = cum[..., -1]
    w = g[..., None] - cum + ig
    mw = jnp.max(w, axis=-1)
    ew = jnp.exp(w - mw[..., None])
    kv = jnp.einsum('bhcs,bhcsv,bhcsk->bhcvk', ew, vc, kc)
    ks = jnp.einsum('bhcs,bhcsk->bhck', ew, kc)

    def step(carry, inp):
        c_s, n_s, m_s = carry
        g_c, mw_c, kv_c, ks_c = inp
        m_new = jnp.maximum(g_c + m_s, mw_c)
        a = jnp.exp(g_c + m_s - m_new)
        e = jnp.exp(mw_c - m_new)
        c_new = a[..., None, None] * c_s + e[..., None, None] * kv_c
        n_new = a[..., None] * n_s + e[..., None] * ks_c
        return (c_new, n_new, m_new), (c_s, n_s, m_s)

    first = lambda a: jnp.moveaxis(a, 2, 0)
    final, (c0, n0, m0) = lax.scan(step, state, (first(g), first(mw), first(kv), first(ks)))
    if not return_h:
        return final, None
    c0 = jnp.moveaxis(c0, 0, 2)
    n0 = jnp.moveaxis(n0, 0, 2)
    m0 = jnp.moveaxis(m0, 0, 2)
    qc = q.reshape(b, h, nc, MLSTM_CHUNK, dh)
    lower = jnp.tril(jnp.ones((MLSTM_CHUNK, MLSTM_CHUNK), dtype=bool))
    dmat = cum[..., :, None] - cum[..., None, :] + ig[..., None, :]
    dmat = jnp.where(lower, dmat, -jnp.inf)
    inter = cum + m0[..., None]
    m_t = jnp.maximum(inter, jnp.max(dmat, axis=-1))
    pm = jnp.exp(dmat - m_t[..., None])
    ei = jnp.exp(inter - m_t)
    wq = pm * jnp.einsum('bhctd,bhcsd->bhcts', qc, kc)
    num = jnp.einsum('bhcts,bhcsv->bhctv', wq, vc) + ei[..., None] * jnp.einsum('bhcvk,bhctk->bhctv', c0, qc)
    den = jnp.sum(wq, axis=-1) + ei * jnp.einsum('bhck,bhctk->bhct', n0, qc)
    hout = num / jnp.maximum(jnp.abs(den), jnp.exp(-m_t))[..., None]
    return final, hout.reshape(b, h, t, dh)


def mlstm_mixer(lat, ctx_parts, conv_w, conv_b, i_bias, f_bias, with_ctx):
    q_l, k_l, v_l, o_l, g_l = lat
    q_c, k_c, v_c, o_c, g_c = ctx_parts
    ql, kl, vl, igl, lfl = mlstm_prepare(q_l, k_l, v_l, g_l, conv_w, conv_b, i_bias, f_bias, True)
    qc, kc, vc, igc, lfc = mlstm_prepare(q_c, k_c, v_c, g_c, conv_w, conv_b, i_bias, f_bias, with_ctx)
    b = kl.shape[0]
    zero = (jnp.zeros((b, MLSTM_HEADS, MLSTM_HEAD_DIM, MLSTM_HEAD_DIM), jnp.float32),
            jnp.zeros((b, MLSTM_HEADS, MLSTM_HEAD_DIM), jnp.float32),
            jnp.zeros((b, MLSTM_HEADS), jnp.float32))
    hs_l, hs_c = [], []
    for di in range(2):
        rev = (lambda a: jnp.flip(a, axis=2)) if di == 1 else (lambda a: a)
        q_in = rev(qc) if with_ctx else None
        st, hc = mlstm_chunked(q_in, rev(kc), rev(vc), rev(igc[di]), rev(lfc[di]), zero, with_ctx)
        _, hl = mlstm_chunked(rev(ql), rev(kl), rev(vl), rev(igl[di]), rev(lfl[di]), st, True)
        hs_l.append(rev(hl))
        if with_ctx:
            hs_c.append(rev(hc))

    def out(hs, o):
        hsum = hs[0] + hs[1]
        bb, hh, tt, dd = hsum.shape
        return (hsum.transpose(0, 2, 1, 3).reshape(bb, tt, MLSTM_WIDTH) * jax.nn.sigmoid(o)).astype(o.dtype)

    return out(hs_l, o_l), (out(hs_c, o_c) if with_ctx else None)


def merge_branches(ya, yb, yc, gate_pre, w_branch_l, w_out_l):
    ga, gb, gc = jnp.split(jax.nn.sigmoid(gate_pre), N_BRANCH, axis=-1)
    mixed = ga * (ya @ w_branch_l[0]) + gb * (yb @ w_branch_l[1]) + gc * (yc @ w_branch_l[2])
    return mixed @ w_out_l


def expert_choice_ffn(h, w_router, w_gate, w_up, w_down):
    b, n, _ = h.shape
    cap = CAPACITY_FACTOR * n // N_EXPERTS
    aff = jax.nn.softmax((h @ w_router).astype(jnp.float32), axis=-1)
    top_aff, top_idx = lax.top_k(jnp.swapaxes(aff, 1, 2), cap)
    bidx = jnp.arange(b)[:, None, None]
    xs = h[bidx, top_idx]
    hid = jax.nn.silu(jnp.einsum('becd,edf->becf', xs, w_gate)) * jnp.einsum('becd,edf->becf', xs, w_up)
    ys = jnp.einsum('becf,efd->becd', hid, w_down) * top_aff[..., None].astype(h.dtype)
    return jnp.zeros_like(h).at[bidx, top_idx].add(ys.astype(h.dtype))


def setup_inputs(seed: int = 0) -> dict:
    key = jax.random.key(seed)
    keys = iter(jax.random.split(key, 40))

    def normal(shape, scale):
        return jax.random.normal(next(keys), shape, jnp.float32) * scale

    def uniform(shape, lo, hi):
        return jax.random.uniform(next(keys), shape, jnp.float32, lo, hi)

    L = DEPTH
    G, P, Hg = S5_GROUPS, S5_STATE, S5_GROUP
    n_idx = jnp.arange(P, dtype=jnp.float32)
    return {
        'x': normal((BATCH, SEQ, D_MODEL), 1.0),
        'c': normal((BATCH, D_MODEL), 1.0),
        'ctx': normal((BATCH, CTX_LEN, D_MODEL), 1.0),
        'c_ctx': normal((D_MODEL,), 1.0),
        'w_mod': normal((L, D_MODEL, N_MOD * D_MODEL), 0.5 * D_MODEL ** -0.5),
        'b_mod': normal((L, N_MOD * D_MODEL), 0.02),
        'w_in': normal((L, D_MODEL, D_IN), D_MODEL ** -0.5),
        'attn_q_gain': 1.0 + normal((L, DIFF_HEAD_DIM), 0.02),
        'attn_k_gain': 1.0 + normal((L, DIFF_HEAD_DIM), 0.02),
        'attn_lambda': normal((L, 4, DIFF_HEAD_DIM), 0.1),
        'attn_out_gain': 1.0 + normal((L, DIFF_V_DIM), 0.02),
        's5_lam_re': -0.5 + normal((L, 2, G, P), 0.01),
        's5_lam_im': math.pi * n_idx + normal((L, 2, G, P), 0.01),
        's5_log_step': uniform((L, 2, G), math.log(1e-3), math.log(1e-1)),
        's5_b_re': normal((L, 2, G, P, Hg), (2.0 * Hg) ** -0.5),
        's5_b_im': normal((L, 2, G, P, Hg), (2.0 * Hg) ** -0.5),
        's5_c_re': normal((L, 2, G, Hg, P), (2.0 * P) ** -0.5),
        's5_c_im': normal((L, 2, G, Hg, P), (2.0 * P) ** -0.5),
        's5_d': normal((L, S5_WIDTH), 1.0),
        's5_w_glu': normal((L, S5_WIDTH, S5_WIDTH), S5_WIDTH ** -0.5),
        'mlstm_conv_w': normal((L, CONV_K, 2 * MLSTM_WIDTH), CONV_K ** -0.5),
        'mlstm_conv_b': normal((L, 2 * MLSTM_WIDTH), 0.02),
        'mlstm_i_bias': normal((L, 2, MLSTM_HEADS), 0.1),
        'mlstm_f_bias': uniform((L, 2, MLSTM_HEADS), 3.0, 6.0),
        'w_branch': normal((L, N_BRANCH, BRANCH_WIDTH, D_MODEL), BRANCH_WIDTH ** -0.5),
        'w_out': normal((L, D_MODEL, D_MODEL), D_MODEL ** -0.5),
        'w_router': normal((L, D_MODEL, N_EXPERTS), D_MODEL ** -0.5),
        'w_exp_gate': normal((L, N_EXPERTS, D_MODEL, D_EXPERT), D_MODEL ** -0.5),
        'w_exp_up': normal((L, N_EXPERTS, D_MODEL, D_EXPERT), D_MODEL ** -0.5),
        'w_exp_down': normal((L, N_EXPERTS, D_EXPERT, D_MODEL), D_EXPERT ** -0.5),
    }


def reference(x, c, ctx, c_ctx, w_mod, b_mod, w_in, attn_q_gain, attn_k_gain, attn_lambda, attn_out_gain,
              s5_lam_re, s5_lam_im, s5_log_step, s5_b_re, s5_b_im, s5_c_re, s5_c_im, s5_d, s5_w_glu,
              mlstm_conv_w, mlstm_conv_b, mlstm_i_bias, mlstm_f_bias,
              w_branch, w_out, w_router, w_exp_gate, w_exp_up, w_exp_down):
    n_lat = x.shape[1]
    ROWS = n_lat // GRID_W
    cos, sin = axial_rope_tables(ROWS)
    silu_c = jax.nn.silu(c)[:, None, :]
    silu_cc = jax.nn.silu(c_ctx)
    xc = ctx
    for l in range(DEPTH):
        with_ctx = l != DEPTH - 1
        mod_l = jnp.split(silu_c @ w_mod[l] + b_mod[l], N_MOD, axis=-1)
        mod_c = jnp.split(silu_cc @ w_mod[l] + b_mod[l], N_MOD, axis=-1)
        lam_init = 0.8 - 0.6 * math.exp(-0.3 * l)
        pl = split_projection(modulate(x, mod_l[0], mod_l[1]) @ w_in[l])
        pc = split_projection(modulate(xc, mod_c[0], mod_c[1]) @ w_in[l])
        ya_l, ya_c = diff_attention_mixer(pl[0], pl[1], pl[2], pc[0], pc[1], pc[2],
                                          attn_q_gain[l], attn_k_gain[l], attn_lambda[l], attn_out_gain[l],
                                          lam_init, cos, sin, with_ctx)
        yb_l, yb_c = s5_mixer(pl[3], pc[3], s5_lam_re[l], s5_lam_im[l], s5_log_step[l], s5_b_re[l], s5_b_im[l],
                              s5_c_re[l], s5_c_im[l], s5_d[l], s5_w_glu[l], with_ctx)
        yc_l, yc_c = mlstm_mixer(pl[4:9], pc[4:9], mlstm_conv_w[l], mlstm_conv_b[l],
                                 mlstm_i_bias[l], mlstm_f_bias[l], with_ctx)
        x = x + mod_l[2] * merge_branches(ya_l, yb_l, yc_l, pl[9], w_branch[l], w_out[l])
        x = x + mod_l[5] * expert_choice_ffn(modulate(x, mod_l[3], mod_l[4]),
                                             w_router[l], w_exp_gate[l], w_exp_up[l], w_exp_down[l])
        if with_ctx:
            xc = xc + mod_c[2] * merge_branches(ya_c, yb_c, yc_c, pc[9], w_branch[l], w_out[l])
            xc = xc + mod_c[5] * expert_choice_ffn(modulate(xc, mod_c[3], mod_c[4]),
                                                   w_router[l], w_exp_gate[l], w_exp_up[l], w_exp_down[l])
    return x
```

```python
import functools
import math

import jax
import jax.numpy as jnp
from jax import lax
from jax.experimental import pallas as pl
from jax.experimental.pallas import tpu as pltpu

F32 = jnp.float32
BF16 = jnp.bfloat16
HI = lax.Precision.HIGHEST

N_MOD = 6
NORM_EPS = 1e-6
GRID_W = 64
ROPE_BASE = 10000.0
N_HEADS = 4
DIFF_HEAD_DIM = 64
HEAD_W = 128
BRANCH_W = 512
S5_GROUPS = 32
S5_GROUP = 16
S5_STATE = 64
S5_CHUNK = 16
MLSTM_CHUNK = 128
CONV_K = 5
N_GATES = 16
N_EXPERTS = 16
CAPACITY_FACTOR = 2
LANES = 128
VMEM_LIMIT = 52 * 1024 * 1024
NEG_BIG = -1e30

SEG_Q, SEG_K, SEG_V, SEG_S5, SEG_MQ, SEG_MK, SEG_MV, SEG_MO, SEG_GATE = range(9)
PA_WIDTH = 8 * BRANCH_W + 3 * 1024


def _cparams(*sem):
    return pltpu.CompilerParams(dimension_semantics=sem, vmem_limit_bytes=VMEM_LIMIT)


def _mod_body(c_ref, w_ref, b_ref, o_ref):
    cv = c_ref[...]
    s = cv * jax.nn.sigmoid(cv)
    o_ref[0] = jnp.dot(s, w_ref[0], precision=HI, preferred_element_type=F32) + b_ref[0]


def _mod_vectors(cvec, w_mod, b_mod):
    n_layers, d, n = w_mod.shape
    tn = n // 4
    return pl.pallas_call(
        _mod_body,
        grid=(n_layers, n // tn),
        in_specs=[
            pl.BlockSpec((8, d), lambda l, j: (0, 0)),
            pl.BlockSpec((1, d, tn), lambda l, j: (l, 0, j)),
            pl.BlockSpec((1, 1, tn), lambda l, j: (l, 0, j)),
        ],
        out_specs=pl.BlockSpec((1, 8, tn), lambda l, j: (l, 0, j)),
        out_shape=jax.ShapeDtypeStruct((n_layers, 8, n), F32),
        compiler_params=_cparams("parallel", "parallel"),
        name="mod_vectors",
    )(cvec, w_mod, b_mod.reshape(n_layers, 1, n))


def _modulated_norm(x, mod, i_shift, i_scale):
    ms = jnp.mean(x * x, axis=-1, keepdims=True)
    xn = x * lax.rsqrt(ms + NORM_EPS)
    return xn * (1.0 + mod[i_scale:i_scale + 1, :]) + mod[i_shift:i_shift + 1, :]


def _proj_body(x_ref, m_ref, w_ref, wg_ref, pa_ref, pg_ref, hn_ref):
    @pl.when(pl.program_id(1) == 0)
    def _():
        hb = _modulated_norm(x_ref[...], m_ref[0], 0, 1).astype(BF16)
        hn_ref[...] = hb
        pg_ref[...] = jnp.dot(hb, wg_ref[...], preferred_element_type=F32)

    pa_ref[...] = jnp.dot(hn_ref[...], w_ref[...], preferred_element_type=F32).astype(BF16)


def _group_of_block(i, blocks_per_sample, n_samples):
    return jnp.minimum(i // blocks_per_sample, n_samples)


def _project(xa, mod, wa, wg, dims):
    b, t, tc, d = dims
    r = xa.shape[0]
    tm = 512
    tn = 1024
    npa = wa.shape[1]
    return pl.pallas_call(
        _proj_body,
        grid=(r // tm, npa // tn),
        in_specs=[
            pl.BlockSpec((tm, d), lambda i, j: (i, 0)),
            pl.BlockSpec((1, N_MOD, d), lambda i, j: (_group_of_block(i, t // tm, b), 0, 0)),
            pl.BlockSpec((d, tn), lambda i, j: (0, j)),
            pl.BlockSpec((d, LANES), lambda i, j: (0, 0)),
        ],
        out_specs=[
            pl.BlockSpec((tm, tn), lambda i, j: (i, j)),
            pl.BlockSpec((tm, LANES), lambda i, j: (i, 0)),
        ],
        out_shape=[
            jax.ShapeDtypeStruct((r, npa), BF16),
            jax.ShapeDtypeStruct((r, LANES), F32),
        ],
        scratch_shapes=[pltpu.VMEM((tm, d), BF16)],
        compiler_params=_cparams("parallel", "arbitrary"),
        name="in_proj",
    )(xa, mod, wa, wg)


def _qk_norm_rope(x_bf, gain, cosf, sinf, seg_ones, first_half):
    x = x_bf.astype(F32)
    x2 = x * x
    hi = x2.astype(BF16)
    lo = (x2 - hi.astype(F32)).astype(BF16)
    ss = (jnp.dot(hi, seg_ones, preferred_element_type=F32)
          + jnp.dot(lo, seg_ones, preferred_element_type=F32))
    xn = x * lax.rsqrt(ss * (1.0 / DIFF_HEAD_DIM) + NORM_EPS) * gain
    half = DIFF_HEAD_DIM // 2
    width = x.shape[1]
    nxt = pltpu.roll(xn, width - half, 1)
    prv = pltpu.roll(xn, half, 1)
    partner = jnp.where(first_half, nxt, prv)
    return xn * cosf + partner * sinf


def _short_conv_silu(prev_ref, cur_ref, next_ref, w, bias, at_start, at_end, out_scale):
    tp = cur_ref.shape[0]
    prev = prev_ref[...].astype(F32)[8:16]
    nxt = next_ref[...].astype(F32)[0:8]
    prev = jnp.where(at_start, 0.0, prev)
    nxt = jnp.where(at_end, 0.0, nxt)
    ext = jnp.concatenate([prev, cur_ref[...].astype(F32), nxt], axis=0)
    acc = bias
    for kk in range(CONV_K):
        off = 8 + kk - CONV_K // 2
        acc = acc + w[kk:kk + 1, :] * ext[off:off + tp]
    y = acc * jax.nn.sigmoid(acc)
    return y * out_scale


def _prep_body(q_ref, k_ref, mqp_ref, mq_ref, mqn_ref, mkp_ref, mk_ref, mkn_ref,
               cos_ref, sin_ref, gq_ref, gk_ref, so_ref, cw_ref, cb_ref,
               qo_ref, ko_ref, mqo_ref, mko_ref, *, b, t, tc, tp):
    cos4 = jnp.concatenate([cos_ref[...]] * 4, axis=1)
    sin4 = jnp.concatenate([sin_ref[...]] * 4, axis=1)
    lane = lax.broadcasted_iota(jnp.int32, (tp, BRANCH_W), 1)
    first_half = (lane % DIFF_HEAD_DIM) < (DIFF_HEAD_DIM // 2)
    seg_ones = so_ref[...]
    qo_ref[...] = _qk_norm_rope(q_ref[...], gq_ref[...], cos4, sin4, seg_ones, first_half).astype(BF16)
    ko_ref[...] = _qk_norm_rope(k_ref[...], gk_ref[...], cos4, sin4, seg_ones, first_half).astype(BF16)

    row0 = pl.program_id(0) * tp
    in_lat = row0 < b * t
    local = jnp.where(in_lat, row0 % t, (row0 - b * t) % tc)
    seq_len = jnp.where(in_lat, t, tc)
    at_start = local == 0
    at_end = local + tp == seq_len
    cw = cw_ref[...]
    cb = cb_ref[...]
    mqo_ref[...] = _short_conv_silu(mqp_ref, mq_ref, mqn_ref, cw[:, :BRANCH_W], cb[:, :BRANCH_W],
                                    at_start, at_end, 1.0).astype(BF16)
    mko_ref[...] = _short_conv_silu(mkp_ref, mk_ref, mkn_ref, cw[:, BRANCH_W:], cb[:, BRANCH_W:],
                                    at_start, at_end, HEAD_W ** -0.5).astype(BF16)


def _prepare(pa, cos_tab, sin_tab, gq, gk, seg_ones, conv_w, conv_b, dims):
    b, t, tc, d = dims
    r = pa.shape[0]
    tp = 256
    halo = 16
    hb = tp // halo
    last_halo = r // halo - 1

    def cur(seg):
        return pl.BlockSpec((tp, BRANCH_W), lambda i: (i, seg))

    def prev(seg):
        return pl.BlockSpec((halo, BRANCH_W), lambda i: (jnp.maximum(i * hb - 1, 0), seg))

    def nxt(seg):
        return pl.BlockSpec((halo, BRANCH_W), lambda i: (jnp.minimum((i + 1) * hb, last_halo), seg))

    full = lambda shape: pl.BlockSpec(shape, lambda i: (0, 0))
    out = jax.ShapeDtypeStruct((r, BRANCH_W), BF16)
    return pl.pallas_call(
        functools.partial(_prep_body, b=b, t=t, tc=tc, tp=tp),
        grid=(r // tp,),
        in_specs=[
            cur(SEG_Q), cur(SEG_K),
            prev(SEG_MQ), cur(SEG_MQ), nxt(SEG_MQ),
            prev(SEG_MK), cur(SEG_MK), nxt(SEG_MK),
            pl.BlockSpec((tp, LANES), lambda i: (i, 0)),
            pl.BlockSpec((tp, LANES), lambda i: (i, 0)),
            full((1, BRANCH_W)), full((1, BRANCH_W)),
            full((BRANCH_W, BRANCH_W)),
            full((8, 2 * BRANCH_W)), full((1, 2 * BRANCH_W)),
        ],
        out_specs=[pl.BlockSpec((tp, BRANCH_W), lambda i: (i, 0))] * 4,
        out_shape=[out, out, out, out],
        compiler_params=_cparams("parallel"),
        name="row_prep",
    )(pa, pa, pa, pa, pa, pa, pa, pa, cos_tab, sin_tab, gq, gk, seg_ones, conv_w, conv_b)


def _attn_body(lam_ref, og_ref, q_ref, k_ref, v_ref, kc_ref, vc_ref, o_ref,
               q0_ref, q1_ref, m_ref, l_ref, acc_ref, *, lam_init, has_ctx, nk):
    kj = pl.program_id(3)

    def process(kb, vb):
        for mi, qr in enumerate((q0_ref, q1_ref)):
            s = lax.dot_general(qr[...], kb, (((1,), (1,)), ((), ())), preferred_element_type=F32)
            m_old = m_ref[mi]
            m_new = jnp.maximum(m_old, jnp.max(s, axis=-1, keepdims=True))
            alpha = jnp.exp(m_old - m_new)
            p = jnp.exp(s - m_new)
            l_ref[mi] = alpha * l_ref[mi] + jnp.sum(p, axis=-1, keepdims=True)
            acc_ref[mi] = alpha * acc_ref[mi] + jnp.dot(p.astype(BF16), vb, preferred_element_type=F32)
            m_ref[mi] = m_new

    @pl.when(kj == 0)
    def _():
        q = q_ref[...]
        lane = lax.broadcasted_iota(jnp.int32, q.shape, 1)
        zero = jnp.zeros_like(q)
        q0_ref[...] = jnp.where(lane < DIFF_HEAD_DIM, q, zero)
        q1_ref[...] = jnp.where(lane >= DIFF_HEAD_DIM, q, zero)
        m_ref[...] = jnp.full(m_ref.shape, NEG_BIG, F32)
        l_ref[...] = jnp.zeros(l_ref.shape, F32)
        acc_ref[...] = jnp.zeros(acc_ref.shape, F32)
        if has_ctx:
            process(kc_ref[...], vc_ref[...])

    process(k_ref[...], v_ref[...])

    @pl.when(kj == nk - 1)
    def _():
        lv = lam_ref[...]
        lam = (jnp.exp(jnp.sum(lv[0:1] * lv[1:2], keepdims=True))
               - jnp.exp(jnp.sum(lv[2:3] * lv[3:4], keepdims=True)) + lam_init)
        o = acc_ref[0] / l_ref[0] - lam * (acc_ref[1] / l_ref[1])
        ms = jnp.mean(o * o, axis=-1, keepdims=True)
        o = o * lax.rsqrt(ms + NORM_EPS) * (og_ref[...] * (1.0 - lam_init))
        o_ref[...] = o.astype(BF16)


def _attention(qh, kh, pa, lam_vecs, out_gain, lam_init, dims, *, ctx_queries):
    b, t, tc, d = dims
    v_col = SEG_V * (BRANCH_W // HEAD_W)
    ctx_blk0 = (b * t) // tc
    if ctx_queries:
        tq = tk = tc
        nq, nk = 1, 1
        q_row = lambda bb, qi: ctx_blk0 + bb
        k_row = lambda bb, kj: ctx_blk0 + bb
        n_rows = b * tc
        o_row = lambda bb, qi: bb
    else:
        tq = min(1024, t)
        tk = min(512, t)
        nq, nk = t // tq, t // tk
        q_row = lambda bb, qi: bb * nq + qi
        k_row = lambda bb, kj: bb * nk + kj
        n_rows = b * t
        o_row = q_row
    body = functools.partial(_attn_body, lam_init=lam_init, has_ctx=not ctx_queries, nk=nk)
    return pl.pallas_call(
        body,
        grid=(b, N_HEADS, nq, nk),
        in_specs=[
            pl.BlockSpec((4, DIFF_HEAD_DIM), lambda bb, h, qi, kj: (0, 0)),
            pl.BlockSpec((1, HEAD_W), lambda bb, h, qi, kj: (0, 0)),
            pl.BlockSpec((tq, HEAD_W), lambda bb, h, qi, kj: (q_row(bb, qi), h)),
            pl.BlockSpec((tk, HEAD_W), lambda bb, h, qi, kj: (k_row(bb, kj), h)),
            pl.BlockSpec((tk, HEAD_W), lambda bb, h, qi, kj: (k_row(bb, kj), v_col + h)),
            pl.BlockSpec((tc, HEAD_W), lambda bb, h, qi, kj: (ctx_blk0 + bb, h)),
            pl.BlockSpec((tc, HEAD_W), lambda bb, h, qi, kj: (ctx_blk0 + bb, v_col + h)),
        ],
        out_specs=pl.BlockSpec((tq, HEAD_W), lambda bb, h, qi, kj: (o_row(bb, qi), h)),
        out_shape=jax.ShapeDtypeStruct((n_rows, BRANCH_W), BF16),
        scratch_shapes=[
            pltpu.VMEM((tq, HEAD_W), BF16),
            pltpu.VMEM((tq, HEAD_W), BF16),
            pltpu.VMEM((2, tq, 1), F32),
            pltpu.VMEM((2, tq, 1), F32),
            pltpu.VMEM((2, tq, HEAD_W), F32),
        ],
        compiler_params=_cparams("parallel", "parallel", "parallel", "arbitrary"),
        name="diff_attn_ctx" if ctx_queries else "diff_attn",
    )(lam_vecs, out_gain, qh, kh, pa, kh, pa)


def _s5_matrices(lam_re, lam_im, log_step, b_re, b_im, c_re, c_im, n_levels):
    ll, hg, pp, gg = S5_CHUNK, S5_GROUP, S5_STATE, S5_GROUPS
    dt = jnp.exp(log_step)[:, :, None]
    lr, li = lam_re * dt, lam_im * dt

    def a_pow(tau):
        tau = tau.astype(F32)[:, None, None, None]
        mag = jnp.exp(lr * tau)
        return mag * jnp.cos(li * tau), mag * jnp.sin(li * tau)

    ar1, ai1 = a_pow(jnp.ones((1,)))
    nr, ni = ar1[0] - 1.0, ai1[0]
    den = lam_re * lam_re + lam_im * lam_im
    f_re = (nr * lam_re + ni * lam_im) / den
    f_im = (ni * lam_re - nr * lam_im) / den
    bb_re = f_re[..., None] * b_re - f_im[..., None] * b_im
    bb_im = f_re[..., None] * b_im + f_im[..., None] * b_re

    ar, ai = a_pow(jnp.arange(ll + 1))
    ca_re = c_re[None] * ar[:, :, :, None, :] - c_im[None] * ai[:, :, :, None, :]
    ca_im = c_re[None] * ai[:, :, :, None, :] + c_im[None] * ar[:, :, :, None, :]
    kk = (jnp.einsum('tdgop,dgph->tdgoh', ca_re, bb_re, precision=HI)
          - jnp.einsum('tdgop,dgph->tdgoh', ca_im, bb_im, precision=HI))
    jj = jnp.arange(ll)
    tau = jj[None, :] - jj[:, None]
    toep = kk[jnp.clip(tau, 0, ll)]
    toep = jnp.where((tau >= 0)[:, :, None, None, None, None], toep, 0.0)
    toep = toep.transpose(2, 3, 0, 5, 1, 4).reshape(2, gg, ll * hg, ll * hg)

    rev = ll - 1 - jj
    in_re = ar[rev][..., None] * bb_re[None] - ai[rev][..., None] * bb_im[None]
    in_im = ar[rev][..., None] * bb_im[None] + ai[rev][..., None] * bb_re[None]
    in_re = in_re.transpose(1, 2, 0, 4, 3).reshape(2, gg, ll * hg, pp)
    in_im = in_im.transpose(1, 2, 0, 4, 3).reshape(2, gg, ll * hg, pp)
    out_re = ca_re[1:].transpose(1, 2, 4, 0, 3).reshape(2, gg, pp, ll * hg)
    out_im = -ca_im[1:].transpose(1, 2, 4, 0, 3).reshape(2, gg, pp, ll * hg)

    def pair_diag(m):
        _, _, a, bdim = m.shape
        m = m.reshape(2, gg // 2, 2, a, bdim)
        z = jnp.zeros_like(m[:, :, 0])
        top = jnp.concatenate([m[:, :, 0], z], axis=-1)
        bot = jnp.concatenate([z, m[:, :, 1]], axis=-1)
        return jnp.concatenate([top, bot], axis=-2)

    lev = (ll * (2 ** jnp.arange(n_levels))).astype(F32)
    alr, ali = a_pow(lev)
    pad_lev = (-n_levels) % 8
    def lev_pairs(a):
        a = a.transpose(1, 2, 0, 3).reshape(2, gg // 2, 2, n_levels, pp)
        a = jnp.concatenate([a[:, :, 0], a[:, :, 1]], axis=-1)
        return jnp.pad(a, ((0, 0), (0, 0), (0, pad_lev), (0, 0)))
    return dict(
        toep=toep.reshape(2, gg // 2, 2, ll * hg, ll * hg).astype(BF16),
        in_re=pair_diag(in_re).astype(BF16), in_im=pair_diag(in_im).astype(BF16),
        out_re=pair_diag(out_re).astype(BF16), out_im=pair_diag(out_im).astype(BF16),
        al_re=lev_pairs(alr), al_im=lev_pairs(ali))


def _s5_body(u_ref, t_ref, inr_ref, ini_ref, outr_ref, outi_ref, alr_ref, ali_ref, y_ref,
             sr_ref, si_ref, *, nb, nc, pad, n_levels):
    u = u_ref[0, 0]
    vr = jnp.dot(u, inr_ref[0, 0], preferred_element_type=F32)
    vi = jnp.dot(u, ini_ref[0, 0], preferred_element_type=F32)
    zeros = jnp.zeros((pad, LANES), F32)
    for bb in range(nb):
        sr_ref[bb, 0:pad, :] = zeros
        si_ref[bb, 0:pad, :] = zeros
        sr_ref[bb, pad:pad + nc, :] = vr[bb * nc:(bb + 1) * nc]
        si_ref[bb, pad:pad + nc, :] = vi[bb * nc:(bb + 1) * nc]
    for lev in range(n_levels):
        dd = 1 << lev
        a_r = alr_ref[0, 0, lev:lev + 1, :]
        a_i = ali_ref[0, 0, lev:lev + 1, :]
        for bb in range(nb):
            cr = sr_ref[bb, pad:pad + nc, :]
            ci = si_ref[bb, pad:pad + nc, :]
            pr = sr_ref[bb, pad - dd:pad - dd + nc, :]
            pi = si_ref[bb, pad - dd:pad - dd + nc, :]
            sr_ref[bb, pad:pad + nc, :] = cr + a_r * pr - a_i * pi
            si_ref[bb, pad:pad + nc, :] = ci + a_r * pi + a_i * pr
    half = S5_CHUNK * S5_GROUP
    for bb in range(nb):
        er = sr_ref[bb, pad - 1:pad - 1 + nc, :].astype(BF16)
        ei = si_ref[bb, pad - 1:pad - 1 + nc, :].astype(BF16)
        ub = u[bb * nc:(bb + 1) * nc]
        y = (jnp.dot(er, outr_ref[0, 0], preferred_element_type=F32)
             + jnp.dot(ei, outi_ref[0, 0], preferred_element_type=F32))
        ya = jnp.dot(ub[:, :half], t_ref[0, 0, 0], preferred_element_type=F32)
        yb = jnp.dot(ub[:, half:], t_ref[0, 0, 1], preferred_element_type=F32)
        y_ref[0, 0, bb * nc:(bb + 1) * nc, :] = (y + jnp.concatenate([ya, yb], axis=1)).astype(BF16)


def _s5_levels(nc):
    return max(1, (nc - 1).bit_length())


def _s5_scan(ug, mats, nb, nc):
    n_levels = _s5_levels(nc)
    pad = max(8, 1 << (n_levels - 1))
    gp = S5_GROUPS // 2
    w = 2 * S5_CHUNK * S5_GROUP
    lev_rows = mats["al_re"].shape[2]
    idx = lambda dd, g: (dd, g, 0, 0)
    return pl.pallas_call(
        functools.partial(_s5_body, nb=nb, nc=nc, pad=pad, n_levels=n_levels),
        grid=(2, gp),
        in_specs=[
            pl.BlockSpec((1, 1, nb * nc, w), idx),
            pl.BlockSpec((1, 1, 2, w // 2, w // 2), lambda dd, g: (dd, g, 0, 0, 0)),
            pl.BlockSpec((1, 1, w, LANES), idx),
            pl.BlockSpec((1, 1, w, LANES), idx),
            pl.BlockSpec((1, 1, LANES, w), idx),
            pl.BlockSpec((1, 1, LANES, w), idx),
            pl.BlockSpec((1, 1, lev_rows, LANES), idx),
            pl.BlockSpec((1, 1, lev_rows, LANES), idx),
        ],
        out_specs=pl.BlockSpec((1, 1, nb * nc, w), idx),
        out_shape=jax.ShapeDtypeStruct((2, gp, nb * nc, w), BF16),
        scratch_shapes=[pltpu.VMEM((nb, pad + nc, LANES), F32), pltpu.VMEM((nb, pad + nc, LANES), F32)],
        compiler_params=_cparams("parallel", "parallel"),
        name="s5_scan",
    )(ug, mats["toep"], mats["in_re"], mats["in_im"], mats["out_re"], mats["out_im"],
      mats["al_re"], mats["al_im"])


def _s5_mixer(pa, mats, dims):
    b, t, tc, d = dims
    ts = t + tc
    nc = ts // S5_CHUNK
    gp = S5_GROUPS // 2
    u = pa[:, SEG_S5 * BRANCH_W:(SEG_S5 + 1) * BRANCH_W]
    ul = u[:b * t].reshape(b, t, BRANCH_W)
    uc = u[b * t:].reshape(b, tc, BRANCH_W)
    seq0 = jnp.concatenate([uc, ul], axis=1)
    seq1 = jnp.concatenate([uc[:, ::-1], ul[:, ::-1]], axis=1)
    seq = jnp.stack([seq0, seq1])
    ug = seq.reshape(2, b, nc, S5_CHUNK, gp, 2, S5_GROUP)
    ug = ug.transpose(0, 4, 1, 2, 5, 3, 6).reshape(2, gp, b * nc, 2 * S5_CHUNK * S5_GROUP)
    yg = _s5_scan(ug, mats, b, nc)
    y = yg.reshape(2, gp, b, nc, 2, S5_CHUNK, S5_GROUP).transpose(0, 2, 3, 5, 1, 4, 6)
    y = y.reshape(2, b, ts, BRANCH_W)
    y0c, y0l = y[0, :, :tc], y[0, :, tc:]
    y1c, y1l = y[1, :, :tc][:, ::-1], y[1, :, tc:][:, ::-1]
    rows = lambda yl, yc: jnp.concatenate([yl.reshape(b * t, BRANCH_W), yc.reshape(b * tc, BRANCH_W)])
    return rows(y0l, y0c), rows(y1l, y1c)


def _log_sigmoid(x):
    return -(jnp.maximum(-x, 0.0) + jnp.log1p(jnp.exp(-jnp.abs(x))))


def _mlstm_body(q_ref, k_ref, v_ref, g_ref, gb_ref, h_ref, c_ref, n_ref, m_ref):
    dd = pl.program_id(1)
    fwd = dd == 0

    @pl.when(pl.program_id(2) == 0)
    def _():
        c_ref[...] = jnp.zeros(c_ref.shape, F32)
        n_ref[...] = jnp.zeros(n_ref.shape, F32)
        m_ref[...] = jnp.zeros(m_ref.shape, F32)

    ll = MLSTM_CHUNK
    row = lax.broadcasted_iota(jnp.int32, (ll, ll), 0)
    col = lax.broadcasted_iota(jnp.int32, (ll, ll), 1)
    order = (row - col) * jnp.where(fwd, 1, -1)
    allowed = order >= 0
    tri = jnp.where(allowed, 1.0, 0.0)
    tri_t = jnp.where(order <= 0, 1.0, 0.0)

    g = g_ref[...] + gb_ref[...]
    lf = _log_sigmoid(g)
    g_t = g.T
    lf_t = lf.T
    outs = []
    for h in range(N_HEADS):
        def pick_col(a, base):
            return jnp.where(fwd, a[:, base + h:base + h + 1], a[:, 8 + base + h:8 + base + h + 1])

        def pick_row(a, base):
            return jnp.where(fwd, a[base + h:base + h + 1, :], a[8 + base + h:8 + base + h + 1, :])

        ig_col, lf_col = pick_col(g, 0), pick_col(lf, 4)
        ig_row, lf_row = pick_row(g_t, 0), pick_row(lf_t, 4)
        cum_col = jnp.sum(tri * lf_row, axis=1, keepdims=True)
        cum_row = jnp.sum(tri_t * lf_col, axis=0, keepdims=True)
        gtot = jnp.sum(lf_row, axis=1, keepdims=True)

        hs = slice(h * HEAD_W, (h + 1) * HEAD_W)
        q = q_ref[:, hs]
        k = k_ref[:, hs]
        v = v_ref[:, hs]
        c0 = c_ref[h]
        n0 = n_ref[h]
        m0 = m_ref[h][:, 0:1]

        dmat = jnp.where(allowed, cum_col - cum_row + ig_row, NEG_BIG)
        inter = cum_col + m0
        m_t = jnp.maximum(inter, jnp.max(dmat, axis=1, keepdims=True))
        pm = jnp.exp(dmat - m_t)
        ei = jnp.exp(inter - m_t)
        qk = lax.dot_general(q, k, (((1,), (1,)), ((), ())), preferred_element_type=F32)
        wq = pm * qk
        qc = lax.dot_general(q, c0.astype(BF16), (((1,), (1,)), ((), ())), preferred_element_type=F32)
        num = jnp.dot(wq.astype(BF16), v, preferred_element_type=F32) + ei * qc
        qf = q.astype(F32)
        den = jnp.sum(wq, axis=1, keepdims=True) + ei * jnp.sum(qf * n0, axis=1, keepdims=True)
        outs.append(num / jnp.maximum(jnp.abs(den), jnp.exp(-m_t)))

        w_col = gtot - cum_col + ig_col
        mw = jnp.max(w_col, axis=0, keepdims=True)
        ew = jnp.exp(w_col - mw)
        kf = k.astype(F32)
        vw = (ew * v.astype(F32)).astype(BF16)
        kv = lax.dot_general(vw, k, (((0,), (0,)), ((), ())), preferred_element_type=F32)
        ks = jnp.sum(ew * kf, axis=0, keepdims=True)
        m_new = jnp.maximum(gtot + m0, mw)
        a = jnp.exp(gtot + m0 - m_new)
        e = jnp.exp(mw - m_new)
        c_ref[h] = a * c0 + e * kv
        n_ref[h] = a * n0 + e * ks
        m_ref[h] = jnp.broadcast_to(m_new, (1, LANES))
    h_ref[0] = jnp.concatenate(outs, axis=1).astype(BF16)


def _mlstm(mq, mk, pa, pg, gate_bias, dims):
    b, t, tc, d = dims
    r = pa.shape[0]
    ll = MLSTM_CHUNK
    nctx, nlat = tc // ll, t // ll
    ctx0 = (b * t) // ll

    def rb(bb, dd, c):
        is_ctx = c < nctx
        cc = jnp.where(dd == 0, c, nctx - 1 - c)
        cl = jnp.where(dd == 0, c - nctx, nlat - 1 - (c - nctx))
        return jnp.where(is_ctx, ctx0 + bb * nctx + cc, bb * nlat + cl)

    return pl.pallas_call(
        _mlstm_body,
        grid=(b, 2, nctx + nlat),
        in_specs=[
            pl.BlockSpec((ll, BRANCH_W), lambda bb, dd, c: (rb(bb, dd, c), 0)),
            pl.BlockSpec((ll, BRANCH_W), lambda bb, dd, c: (rb(bb, dd, c), 0)),
            pl.BlockSpec((ll, BRANCH_W), lambda bb, dd, c: (rb(bb, dd, c), SEG_MV)),
            pl.BlockSpec((ll, LANES), lambda bb, dd, c: (rb(bb, dd, c), 0)),
            pl.BlockSpec((1, LANES), lambda bb, dd, c: (0, 0)),
        ],
        out_specs=pl.BlockSpec((1, ll, BRANCH_W), lambda bb, dd, c: (dd, rb(bb, dd, c), 0)),
        out_shape=jax.ShapeDtypeStruct((2, r, BRANCH_W), BF16),
        scratch_shapes=[
            pltpu.VMEM((N_HEADS, HEAD_W, HEAD_W), F32),
            pltpu.VMEM((N_HEADS, 1, HEAD_W), F32),
            pltpu.VMEM((N_HEADS, 1, LANES), F32),
        ],
        compiler_params=_cparams("parallel", "parallel", "arbitrary"),
        name="mlstm",
    )(mq, mk, pa, pg, gate_bias)


def _merge_body(ya_ref, y0_ref, y1_ref, u_ref, h0_ref, h1_ref, mo_ref, ga_ref, gb_ref, gc_ref, x_ref, m_ref,
                d_ref, wglu_ref, wb_ref, wo_ref, wr_ref, wrt_ref,
                x1_ref, h2_ref, aff_ref, afft_ref):
    mod = m_ref[0]
    ys = d_ref[...] * u_ref[...].astype(F32) + y0_ref[...].astype(F32) + y1_ref[...].astype(F32)
    gl = jax.nn.gelu(ys)
    yb = gl * jax.nn.sigmoid(jnp.dot(gl.astype(BF16), wglu_ref[...], preferred_element_type=F32))
    yc = (h0_ref[0].astype(F32) + h1_ref[0].astype(F32)) * jax.nn.sigmoid(mo_ref[...].astype(F32))
    gate = lambda ref: jax.nn.sigmoid(ref[...].astype(F32))
    mixed = (gate(ga_ref) * jnp.dot(ya_ref[...], wb_ref[0], preferred_element_type=F32)
             + gate(gb_ref) * jnp.dot(yb.astype(BF16), wb_ref[1], preferred_element_type=F32)
             + gate(gc_ref) * jnp.dot(yc.astype(BF16), wb_ref[2], preferred_element_type=F32))
    out = jnp.dot(mixed.astype(BF16), wo_ref[...], preferred_element_type=F32)
    x1 = x_ref[...] + mod[2:3, :] * out
    x1_ref[...] = x1
    h2 = _modulated_norm(x1, mod, 3, 4)
    h2_ref[...] = h2.astype(BF16)
    logits = jnp.dot(h2, wr_ref[...], precision=HI, preferred_element_type=F32)
    lane = lax.broadcasted_iota(jnp.int32, logits.shape, 1)
    logits = jnp.where(lane < N_EXPERTS, logits, NEG_BIG)
    ex = jnp.exp(logits - jnp.max(logits, axis=1, keepdims=True))
    aff_ref[...] = ex / jnp.sum(ex, axis=1, keepdims=True)
    lt = lax.dot_general(wrt_ref[...], h2, (((1,), (1,)), ((), ())), precision=HI,
                         preferred_element_type=F32)
    et = jnp.exp(lt - jnp.max(lt, axis=0, keepdims=True))
    afft_ref[...] = et / jnp.sum(et, axis=0, keepdims=True)


def _merge(ya, y0, y1, pa, hm, xa, mod, s5_d, w_glu, w_branch, w_out, wr_pad, wr_t, dims):
    b, t, tc, d = dims
    r = xa.shape[0]
    tm = 256
    gseg = SEG_GATE * BRANCH_W // d
    rowblk = lambda width, col=0: pl.BlockSpec((tm, width), lambda i: (i, col))
    full2 = lambda shape: pl.BlockSpec(shape, lambda i: (0, 0))
    return pl.pallas_call(
        _merge_body,
        grid=(r // tm,),
        in_specs=[
            rowblk(BRANCH_W), rowblk(BRANCH_W), rowblk(BRANCH_W),
            rowblk(BRANCH_W, SEG_S5),
            pl.BlockSpec((1, tm, BRANCH_W), lambda i: (0, i, 0)),
            pl.BlockSpec((1, tm, BRANCH_W), lambda i: (1, i, 0)),
            rowblk(BRANCH_W, SEG_MO),
            rowblk(d, gseg), rowblk(d, gseg + 1), rowblk(d, gseg + 2),
            rowblk(d),
            pl.BlockSpec((1, N_MOD, d), lambda i: (_group_of_block(i, t // tm, b), 0, 0)),
            full2((1, BRANCH_W)),
            full2((BRANCH_W, BRANCH_W)),
            pl.BlockSpec((3, BRANCH_W, d), lambda i: (0, 0, 0)),
            full2((d, d)),
            full2((d, LANES)),
            full2((N_EXPERTS, d)),
        ],
        out_specs=[
            rowblk(d), rowblk(d), rowblk(LANES),
            pl.BlockSpec((N_EXPERTS, tm), lambda i: (0, i)),
        ],
        out_shape=[
            jax.ShapeDtypeStruct((r, d), F32),
            jax.ShapeDtypeStruct((r, d), BF16),
            jax.ShapeDtypeStruct((r, LANES), F32),
            jax.ShapeDtypeStruct((N_EXPERTS, r), F32),
        ],
        compiler_params=_cparams("parallel"),
        name="merge",
    )(ya, y0, y1, pa, hm, hm, pa, pa, pa, pa, xa, mod, s5_d, w_glu, w_branch, w_out, wr_pad, wr_t)


def _route_body(a_ref, tri_ref, low_ref, pos_ref, offs_ref, *, cap):
    a = a_ref[0]
    e, nb, _ = a.shape
    bits = pltpu.bitcast(a, jnp.int32)

    def count(mask):
        c = jnp.sum(jnp.where(mask, 1.0, 0.0), axis=2, keepdims=True)
        return jnp.sum(c, axis=1, keepdims=True)

    def step(i, thr):
        cand = thr | jnp.left_shift(jnp.int32(1), 30 - i)
        return jnp.where(count(bits >= cand) >= cap, cand, thr)

    thr = lax.fori_loop(0, 31, step, jnp.zeros((e, 1, 1), jnp.int32))
    gt = bits > thr
    eq = bits == thr
    need = cap - count(gt)

    tri = tri_ref[...]
    low = low_ref[...]

    def exclusive_prefix(x):
        x2 = x.reshape(e * nb, LANES).astype(BF16)
        incl = jnp.dot(x2, tri, preferred_element_type=F32)
        before = jnp.sum(jnp.dot(low, x2, preferred_element_type=F32), axis=1, keepdims=True)
        return (incl - x2.astype(F32) + before).reshape(e, nb, LANES), before.reshape(e, nb, 1)

    eq_rank, _ = exclusive_prefix(jnp.where(eq, 1.0, 0.0))
    sel = gt | (eq & (eq_rank < need))
    pos, before = exclusive_prefix(jnp.where(sel, 1.0, 0.0))
    pos_ref[0] = jnp.where(sel, pos, -1.0)
    offs_ref[0] = before.astype(jnp.int32)


def _route(aff3, cap):
    ns, e, nb, _ = aff3.shape
    i = jnp.arange(LANES)
    tri = (i[:, None] <= i[None, :]).astype(BF16)
    r = jnp.arange(e * nb)
    low = ((r[:, None] // nb == r[None, :] // nb) & (r[None, :] < r[:, None])).astype(BF16)
    return pl.pallas_call(
        functools.partial(_route_body, cap=cap),
        grid=(ns,),
        in_specs=[
            pl.BlockSpec((1, e, nb, LANES), lambda s: (s, 0, 0, 0)),
            pl.BlockSpec((LANES, LANES), lambda s: (0, 0)),
            pl.BlockSpec((e * nb, e * nb), lambda s: (0, 0)),
        ],
        out_specs=[
            pl.BlockSpec((1, e, nb, LANES), lambda s: (s, 0, 0, 0)),
            pl.BlockSpec((1, e, nb, 1), lambda s: (s, 0, 0, 0)),
        ],
        out_shape=[
            jax.ShapeDtypeStruct((ns, e, nb, LANES), F32),
            jax.ShapeDtypeStruct((ns, e, nb, 1), jnp.int32),
        ],
        compiler_params=_cparams("parallel"),
        name="route",
    )(aff3, tri, low)


SLOT_ALIGN = 16


def _dispatch_body(st_ref, h_ref, p_ref, o_ref, *, tb, win, nj):
    ns, e, j = pl.program_id(0), pl.program_id(1), pl.program_id(2)

    @pl.when(j == 0)
    def _():
        o_ref[...] = jnp.zeros(o_ref.shape, o_ref.dtype)

    start = st_ref[(ns * N_EXPERTS + e) * nj + j]
    base = pl.multiple_of((start // SLOT_ALIGN) * SLOT_ALIGN, SLOT_ALIGN)
    slot = (base + lax.broadcasted_iota(jnp.int32, (win, tb), 0)).astype(F32)
    onehot = jnp.where(slot == p_ref[0], 1.0, 0.0).astype(BF16)
    rows = jnp.dot(onehot, h_ref[...], preferred_element_type=F32)
    cur = o_ref[0, 0, pl.ds(base, win), :]
    o_ref[0, 0, pl.ds(base, win), :] = cur + rows.astype(o_ref.dtype)


def _dispatch(starts, h2, pos_rows, row0, ns, n, cap, tb):
    d = h2.shape[1]
    nj = n // tb
    win = tb + SLOT_ALIGN
    capp = cap + win
    blk0 = row0 // tb
    grid_spec = pltpu.PrefetchScalarGridSpec(
        num_scalar_prefetch=1,
        grid=(ns, N_EXPERTS, nj),
        in_specs=[
            pl.BlockSpec((tb, d), lambda s, e, j, st: (blk0 + s * nj + j, 0)),
            pl.BlockSpec((1, 1, tb), lambda s, e, j, st: ((s * N_EXPERTS + e) * nj + j, 0, 0)),
        ],
        out_specs=pl.BlockSpec((1, 1, capp, d), lambda s, e, j, st: (s, e, 0, 0)),
    )
    return pl.pallas_call(
        functools.partial(_dispatch_body, tb=tb, win=win, nj=nj),
        grid_spec=grid_spec,
        out_shape=jax.ShapeDtypeStruct((ns, N_EXPERTS, capp, d), BF16),
        compiler_params=_cparams("parallel", "parallel", "arbitrary"),
        name="dispatch",
    )(starts, h2, pos_rows)


def _expert_body(x_ref, wg_ref, wu_ref, wd_ref, y_ref, *, fc):
    x = x_ref[0, 0]
    f = wg_ref.shape[2]
    acc = jnp.zeros((x.shape[0], wd_ref.shape[2]), F32)
    for f0 in range(0, f, fc):
        g = jnp.dot(x, wg_ref[0, :, f0:f0 + fc], preferred_element_type=F32)
        u = jnp.dot(x, wu_ref[0, :, f0:f0 + fc], preferred_element_type=F32)
        hid = (g * jax.nn.sigmoid(g) * u).astype(BF16)
        acc = acc + jnp.dot(hid, wd_ref[0, f0:f0 + fc, :], preferred_element_type=F32)
    y_ref[0, 0] = acc.astype(BF16)


def _experts(xs, w_gate, w_up, w_down, cap):
    ns, e, _, d = xs.shape
    f = w_gate.shape[2]
    ts = min(512, cap)
    return pl.pallas_call(
        functools.partial(_expert_body, fc=min(512, f)),
        grid=(e, ns, cap // ts),
        in_specs=[
            pl.BlockSpec((1, 1, ts, d), lambda ee, s, i: (s, ee, i, 0)),
            pl.BlockSpec((1, d, f), lambda ee, s, i: (ee, 0, 0)),
            pl.BlockSpec((1, d, f), lambda ee, s, i: (ee, 0, 0)),
            pl.BlockSpec((1, f, d), lambda ee, s, i: (ee, 0, 0)),
        ],
        out_specs=pl.BlockSpec((1, 1, ts, d), lambda ee, s, i: (s, ee, i, 0)),
        out_shape=jax.ShapeDtypeStruct((ns, e, cap, d), BF16),
        compiler_params=_cparams("parallel", "parallel", "parallel"),
        name="experts",
    )(xs, w_gate, w_up, w_down)


def _combine_body(st_ref, ya_ref, yb_ref, p_ref, aff_ref, x_ref, m_ref, o_ref, *, sb, nj):
    ns, j, e = pl.program_id(0), pl.program_id(1), pl.program_id(2)

    @pl.when(e == 0)
    def _():
        o_ref[...] = jnp.zeros(o_ref.shape, F32)

    tb = o_ref.shape[0]
    start = st_ref[(ns * N_EXPERTS + e) * nj + j]
    a = start // sb
    lane = lax.broadcasted_iota(jnp.int32, (tb, LANES), 1)
    mine = lane == e
    pos = jnp.sum(jnp.where(mine, p_ref[...], 0.0), axis=1, keepdims=True)
    aff = jnp.sum(jnp.where(mine, aff_ref[...], 0.0), axis=1, keepdims=True)
    slot = (a * sb + lax.broadcasted_iota(jnp.int32, (tb, sb), 1)).astype(F32)
    oh_a = jnp.where(pos == slot, 1.0, 0.0).astype(BF16)
    oh_b = jnp.where(pos == slot + float(sb), 1.0, 0.0).astype(BF16)
    got = (jnp.dot(oh_a, ya_ref[0, 0], preferred_element_type=F32)
           + jnp.dot(oh_b, yb_ref[0, 0], preferred_element_type=F32))
    o_ref[...] += aff * got

    @pl.when(e == N_EXPERTS - 1)
    def _():
        o_ref[...] = x_ref[...] + m_ref[0, 5:6, :] * o_ref[...]


def _combine(starts, ys, pos_t, aff, x1, mod, row0, ns, n, cap, tb, mod_group):
    d = x1.shape[1]
    nj = n // tb
    sb = min(tb, cap)
    nsb = cap // sb
    blk0 = row0 // tb

    def ya_map(s, j, e, st):
        return (s, e, st[(s * N_EXPERTS + e) * nj + j] // sb, 0)

    def yb_map(s, j, e, st):
        return (s, e, jnp.minimum(st[(s * N_EXPERTS + e) * nj + j] // sb + 1, nsb - 1), 0)

    row = lambda s, j, e, st: (blk0 + s * nj + j, 0)
    grid_spec = pltpu.PrefetchScalarGridSpec(
        num_scalar_prefetch=1,
        grid=(ns, nj, N_EXPERTS),
        in_specs=[
            pl.BlockSpec((1, 1, sb, d), ya_map),
            pl.BlockSpec((1, 1, sb, d), yb_map),
            pl.BlockSpec((tb, LANES), row),
            pl.BlockSpec((tb, LANES), row),
            pl.BlockSpec((tb, d), row),
            pl.BlockSpec((1, N_MOD, d), lambda s, j, e, st: (mod_group(s), 0, 0)),
        ],
        out_specs=pl.BlockSpec((tb, d), lambda s, j, e, st: (s * nj + j, 0)),
    )
    return pl.pallas_call(
        functools.partial(_combine_body, sb=sb, nj=nj),
        grid_spec=grid_spec,
        out_shape=jax.ShapeDtypeStruct((ns * n, d), F32),
        compiler_params=_cparams("parallel", "parallel", "arbitrary"),
        name="combine",
    )(starts, ys, ys, pos_t, aff, x1, mod)


def _expert_choice(h2, aff, aff_t, x1, mod, w_gate, w_up, w_down, row0, ns, n, mod_group):
    r, d = h2.shape
    e = N_EXPERTS
    cap = CAPACITY_FACTOR * n // e
    tb = min(256, n)
    n_pad = max(n, 8 * LANES)
    a = aff_t[:, row0:row0 + ns * n].reshape(e, ns, n).transpose(1, 0, 2)
    if n_pad > n:
        a = jnp.concatenate([a, jnp.full((ns, e, n_pad - n), -1.0, F32)], axis=2)
    pos, offs = _route(a.reshape(ns, e, n_pad // LANES, LANES), cap)
    pos = pos.reshape(ns, e, n_pad)[:, :, :n]
    starts = offs.reshape(ns, e, n_pad // LANES)[:, :, :n // LANES]
    starts = starts[:, :, ::tb // LANES].reshape(-1)
    nj = n // tb
    xs = _dispatch(starts, h2, pos.reshape(ns * e * nj, 1, tb), row0, ns, n, cap, tb)
    ys = _experts(xs, w_gate, w_up, w_down, cap)
    pos_t = jnp.pad(pos.transpose(0, 2, 1).reshape(ns * n, e), ((0, 0), (0, LANES - e)), constant_values=-1.0)
    pos_t = jnp.pad(pos_t, ((row0, r - row0 - ns * n), (0, 0)))
    return _combine(starts, ys, pos_t, aff, x1, mod, row0, ns, n, cap, tb, mod_group)


def _rope_tables(b, t, tc):
    n_freq = DIFF_HEAD_DIM // 4
    inv_freq = ROPE_BASE ** (-jnp.arange(n_freq, dtype=F32) / n_freq)
    pos = jnp.arange(t)
    row = (pos // GRID_W).astype(F32)
    col = (pos % GRID_W).astype(F32)
    ang = jnp.concatenate([row[:, None] * inv_freq, col[:, None] * inv_freq], axis=-1)
    cos, sin = jnp.cos(ang), jnp.sin(ang)
    cos_seg = jnp.concatenate([cos, cos], axis=-1)
    sin_seg = jnp.concatenate([-sin, sin], axis=-1)
    cos_t = jnp.tile(cos_seg, (b, LANES // DIFF_HEAD_DIM))
    sin_t = jnp.tile(sin_seg, (b, LANES // DIFF_HEAD_DIM))
    cos_t = jnp.concatenate([cos_t, jnp.ones((b * tc, LANES), F32)])
    sin_t = jnp.concatenate([sin_t, jnp.zeros((b * tc, LANES), F32)])
    return cos_t, sin_t


def kernel(x, c, ctx, c_ctx, w_mod, b_mod, w_in, attn_q_gain, attn_k_gain, attn_lambda, attn_out_gain,
           s5_lam_re, s5_lam_im, s5_log_step, s5_b_re, s5_b_im, s5_c_re, s5_c_im, s5_d, s5_w_glu,
           mlstm_conv_w, mlstm_conv_b, mlstm_i_bias, mlstm_f_bias,
           w_branch, w_out, w_router, w_exp_gate, w_exp_up, w_exp_down):
    b, t, d = x.shape
    tc = ctx.shape[1]
    n_layers = w_mod.shape[0]
    dims = (b, t, tc, d)
    assert b + 1 <= 8 and t % 512 == 0 and tc % 256 == 0 and (b * tc) % 512 == 0

    xa = jnp.concatenate([x.reshape(b * t, d), ctx.reshape(b * tc, d)])
    cvec = jnp.zeros((8, d), F32).at[:b].set(c).at[b].set(c_ctx)
    mod_all = _mod_vectors(cvec, w_mod, b_mod).reshape(n_layers, 8, N_MOD, d)

    n_main = 8 * BRANCH_W
    wa = jnp.concatenate([w_in[:, :, :n_main], w_in[:, :, n_main + N_GATES:]], axis=2).astype(BF16)
    wg = jnp.pad(w_in[:, :, n_main:n_main + N_GATES], ((0, 0), (0, 0), (0, LANES - N_GATES))).astype(BF16)
    cos_t, sin_t = _rope_tables(b, t, tc)
    seg = jnp.arange(BRANCH_W) // DIFF_HEAD_DIM
    seg_ones = (seg[:, None] == seg[None, :]).astype(BF16)
    n_seg = BRANCH_W // DIFF_HEAD_DIM
    gq = jnp.tile(attn_q_gain, (1, n_seg))[:, None, :] * (DIFF_HEAD_DIM ** -0.5)
    gk = jnp.tile(attn_k_gain, (1, n_seg))[:, None, :]
    conv_w = jnp.pad(mlstm_conv_w, ((0, 0), (0, 8 - CONV_K), (0, 0)))
    gate_bias = jnp.stack([mlstm_i_bias, mlstm_f_bias], axis=2).reshape(n_layers, 1, N_GATES)
    gate_bias = jnp.pad(gate_bias, ((0, 0), (0, 0), (0, LANES - N_GATES)))
    wr_pad = jnp.pad(w_router, ((0, 0), (0, 0), (0, LANES - N_EXPERTS)))
    wr_t = jnp.swapaxes(w_router, 1, 2)
    n_chunks = (t + tc) // S5_CHUNK
    n_levels = _s5_levels(n_chunks)

    for l in range(n_layers):
        with_ctx = l != n_layers - 1
        lam_init = 0.8 - 0.6 * math.exp(-0.3 * l)
        mod = mod_all[l]
        pa, pg = _project(xa, mod, wa[l], wg[l], dims)
        qh, kh, mq, mk = _prepare(pa, cos_t, sin_t, gq[l], gk[l], seg_ones, conv_w[l],
                                  mlstm_conv_b[l][None, :], dims)
        og = attn_out_gain[l][None, :]
        ya_l = _attention(qh, kh, pa, attn_lambda[l], og, lam_init, dims, ctx_queries=False)
        if with_ctx:
            ya_c = _attention(qh, kh, pa, attn_lambda[l], og, lam_init, dims, ctx_queries=True)
        else:
            ya_c = jnp.zeros((b * tc, BRANCH_W), BF16)
        ya = jnp.concatenate([ya_l, ya_c])
        mats = _s5_matrices(s5_lam_re[l], s5_lam_im[l], s5_log_step[l], s5_b_re[l], s5_b_im[l],
                            s5_c_re[l], s5_c_im[l], n_levels)
        y0, y1 = _s5_mixer(pa, mats, dims)
        hm = _mlstm(mq, mk, pa, pg, gate_bias[l], dims)
        x1, h2, aff, aff_t = _merge(ya, y0, y1, pa, hm, xa, mod, s5_d[l][None, :],
                                    s5_w_glu[l].astype(BF16), w_branch[l].astype(BF16),
                                    w_out[l].astype(BF16), wr_pad[l], wr_t[l], dims)
        wge, wue, wde = (w_exp_gate[l].astype(BF16), w_exp_up[l].astype(BF16), w_exp_down[l].astype(BF16))
        x2_l = _expert_choice(h2, aff, aff_t, x1, mod, wge, wue, wde, 0, b, t, lambda s: s)
        if with_ctx:
            x2_c = _expert_choice(h2, aff, aff_t, x1, mod, wge, wue, wde, b * t, b, tc, lambda s: b)
        else:
            x2_c = x1[b * t:]
        xa = jnp.concatenate([x2_l, x2_c])
    return xa[:b * t].reshape(b, t, d)
```

```python
import functools
import math

import jax
import jax.numpy as jnp
from jax import lax
from jax.experimental import pallas as pl
from jax.experimental.pallas import tpu as pltpu

F32 = jnp.float32
BF16 = jnp.bfloat16
HI = lax.Precision.HIGHEST

N_MOD = 6
NORM_EPS = 1e-6
GRID_W = 64
ROPE_BASE = 10000.0
N_HEADS = 4
DIFF_HEAD_DIM = 64
HEAD_W = 128
BRANCH_W = 512
S5_GROUPS = 32
S5_GROUP = 16
S5_STATE = 64
S5_CHUNK = 16
MLSTM_CHUNK = 128
CONV_K = 5
N_GATES = 16
N_EXPERTS = 16
CAPACITY_FACTOR = 2
LANES = 128
VMEM_LIMIT = 52 * 1024 * 1024
NEG_BIG = -1e30

SEG_Q, SEG_K, SEG_V, SEG_S5, SEG_MQ, SEG_MK, SEG_MV, SEG_MO, SEG_GATE = range(9)
PA_WIDTH = 8 * BRANCH_W + 3 * 1024


def _cparams(*sem):
    return pltpu.CompilerParams(dimension_semantics=sem, vmem_limit_bytes=VMEM_LIMIT)


def _mod_body(c_ref, w_ref, b_ref, o_ref):
    cv = c_ref[...]
    s = cv * jax.nn.sigmoid(cv)
    o_ref[0] = jnp.dot(s, w_ref[0], precision=HI, preferred_element_type=F32) + b_ref[0]


def _mod_vectors(cvec, w_mod, b_mod):
    n_layers, d, n = w_mod.shape
    tn = n // 4
    return pl.pallas_call(
        _mod_body,
        grid=(n_layers, n // tn),
        in_specs=[
            pl.BlockSpec((8, d), lambda l, j: (0, 0)),
            pl.BlockSpec((1, d, tn), lambda l, j: (l, 0, j)),
            pl.BlockSpec((1, 1, tn), lambda l, j: (l, 0, j)),
        ],
        out_specs=pl.BlockSpec((1, 8, tn), lambda l, j: (l, 0, j)),
        out_shape=jax.ShapeDtypeStruct((n_layers, 8, n), F32),
        compiler_params=_cparams("parallel", "parallel"),
        name="mod_vectors",
    )(cvec, w_mod, b_mod.reshape(n_layers, 1, n))


def _modulated_norm(x, mod, i_shift, i_scale):
    ms = jnp.mean(x * x, axis=-1, keepdims=True)
    xn = x * lax.rsqrt(ms + NORM_EPS)
    return xn * (1.0 + mod[i_scale:i_scale + 1, :]) + mod[i_shift:i_shift + 1, :]


def _proj_body(x_ref, m_ref, w_ref, wg_ref, pa_ref, pg_ref, hn_ref):
    @pl.when(pl.program_id(1) == 0)
    def _():
        hb = _modulated_norm(x_ref[...], m_ref[0], 0, 1).astype(BF16)
        hn_ref[...] = hb
        pg_ref[...] = jnp.dot(hb, wg_ref[...], preferred_element_type=F32)

    pa_ref[...] = jnp.dot(hn_ref[...], w_ref[...], preferred_element_type=F32).astype(BF16)


def _group_of_block(i, blocks_per_sample, n_samples):
    return jnp.minimum(i // blocks_per_sample, n_samples)


def _project(xa, mod, wa, wg, dims):
    b, t, tc, d = dims
    r = xa.shape[0]
    tm = 512
    tn = 1024
    npa = wa.shape[1]
    return pl.pallas_call(
        _proj_body,
        grid=(r // tm, npa // tn),
        in_specs=[
            pl.BlockSpec((tm, d), lambda i, j: (i, 0)),
            pl.BlockSpec((1, N_MOD, d), lambda i, j: (_group_of_block(i, t // tm, b), 0, 0)),
            pl.BlockSpec((d, tn), lambda i, j: (0, j)),
            pl.BlockSpec((d, LANES), lambda i, j: (0, 0)),
        ],
        out_specs=[
            pl.BlockSpec((tm, tn), lambda i, j: (i, j)),
            pl.BlockSpec((tm, LANES), lambda i, j: (i, 0)),
        ],
        out_shape=[
            jax.ShapeDtypeStruct((r, npa), BF16),
            jax.ShapeDtypeStruct((r, LANES), F32),
        ],
        scratch_shapes=[pltpu.VMEM((tm, d), BF16)],
        compiler_params=_cparams("parallel", "arbitrary"),
        name="in_proj",
    )(xa, mod, wa, wg)


def _qk_norm_rope(x_bf, gain, cosf, sinf, seg_ones, first_half):
    x = x_bf.astype(F32)
    x2 = x * x
    hi = x2.astype(BF16)
    lo = (x2 - hi.astype(F32)).astype(BF16)
    ss = (jnp.dot(hi, seg_ones, preferred_element_type=F32)
          + jnp.dot(lo, seg_ones, preferred_element_type=F32))
    xn = x * lax.rsqrt(ss * (1.0 / DIFF_HEAD_DIM) + NORM_EPS) * gain
    half = DIFF_HEAD_DIM // 2
    width = x.shape[1]
    nxt = pltpu.roll(xn, width - half, 1)
    prv = pltpu.roll(xn, half, 1)
    partner = jnp.where(first_half, nxt, prv)
    return xn * cosf + partner * sinf


def _short_conv_silu(prev_ref, cur_ref, next_ref, w, bias, at_start, at_end, out_scale):
    tp = cur_ref.shape[0]
    prev = prev_ref[...].astype(F32)[8:16]
    nxt = next_ref[...].astype(F32)[0:8]
    prev = jnp.where(at_start, 0.0, prev)
    nxt = jnp.where(at_end, 0.0, nxt)
    ext = jnp.concatenate([prev, cur_ref[...].astype(F32), nxt], axis=0)
    acc = bias
    for kk in range(CONV_K):
        off = 8 + kk - CONV_K // 2
        acc = acc + w[kk:kk + 1, :] * ext[off:off + tp]
    y = acc * jax.nn.sigmoid(acc)
    return y * out_scale


def _prep_body(q_ref, k_ref, mqp_ref, mq_ref, mqn_ref, mkp_ref, mk_ref, mkn_ref,
               cos_ref, sin_ref, gq_ref, gk_ref, so_ref, cw_ref, cb_ref,
               qo_ref, ko_ref, mqo_ref, mko_ref, *, b, t, tc, tp):
    cos4 = jnp.concatenate([cos_ref[...]] * 4, axis=1)
    sin4 = jnp.concatenate([sin_ref[...]] * 4, axis=1)
    lane = lax.broadcasted_iota(jnp.int32, (tp, BRANCH_W), 1)
    first_half = (lane % DIFF_HEAD_DIM) < (DIFF_HEAD_DIM // 2)
    seg_ones = so_ref[...]
    qo_ref[...] = _qk_norm_rope(q_ref[...], gq_ref[...], cos4, sin4, seg_ones, first_half).astype(BF16)
    ko_ref[...] = _qk_norm_rope(k_ref[...], gk_ref[...], cos4, sin4, seg_ones, first_half).astype(BF16)

    row0 = pl.program_id(0) * tp
    in_lat = row0 < b * t
    local = jnp.where(in_lat, row0 % t, (row0 - b * t) % tc)
    seq_len = jnp.where(in_lat, t, tc)
    at_start = local == 0
    at_end = local + tp == seq_len
    cw = cw_ref[...]
    cb = cb_ref[...]
    mqo_ref[...] = _short_conv_silu(mqp_ref, mq_ref, mqn_ref, cw[:, :BRANCH_W], cb[:, :BRANCH_W],
                                    at_start, at_end, 1.0).astype(BF16)
    mko_ref[...] = _short_conv_silu(mkp_ref, mk_ref, mkn_ref, cw[:, BRANCH_W:], cb[:, BRANCH_W:],
                                    at_start, at_end, HEAD_W ** -0.5).astype(BF16)


def _prepare(pa, cos_tab, sin_tab, gq, gk, seg_ones, conv_w, conv_b, dims):
    b, t, tc, d = dims
    r = pa.shape[0]
    tp = 256
    halo = 16
    hb = tp // halo
    last_halo = r // halo - 1

    def cur(seg):
        return pl.BlockSpec((tp, BRANCH_W), lambda i: (i, seg))

    def prev(seg):
        return pl.BlockSpec((halo, BRANCH_W), lambda i: (jnp.maximum(i * hb - 1, 0), seg))

    def nxt(seg):
        return pl.BlockSpec((halo, BRANCH_W), lambda i: (jnp.minimum((i + 1) * hb, last_halo), seg))

    full = lambda shape: pl.BlockSpec(shape, lambda i: (0, 0))
    out = jax.ShapeDtypeStruct((r, BRANCH_W), BF16)
    return pl.pallas_call(
        functools.partial(_prep_body, b=b, t=t, tc=tc, tp=tp),
        grid=(r // tp,),
        in_specs=[
            cur(SEG_Q), cur(SEG_K),
            prev(SEG_MQ), cur(SEG_MQ), nxt(SEG_MQ),
            prev(SEG_MK), cur(SEG_MK), nxt(SEG_MK),
            pl.BlockSpec((tp, LANES), lambda i: (i, 0)),
            pl.BlockSpec((tp, LANES), lambda i: (i, 0)),
            full((1, BRANCH_W)), full((1, BRANCH_W)),
            full((BRANCH_W, BRANCH_W)),
            full((8, 2 * BRANCH_W)), full((1, 2 * BRANCH_W)),
        ],
        out_specs=[pl.BlockSpec((tp, BRANCH_W), lambda i: (i, 0))] * 4,
        out_shape=[out, out, out, out],
        compiler_params=_cparams("parallel"),
        name="row_prep",
    )(pa, pa, pa, pa, pa, pa, pa, pa, cos_tab, sin_tab, gq, gk, seg_ones, conv_w, conv_b)


def _attn_body(lam_ref, og_ref, q_ref, k_ref, v_ref, kc_ref, vc_ref, o_ref,
               q0_ref, q1_ref, m_ref, l_ref, acc_ref, *, lam_init, has_ctx, nk):
    kj = pl.program_id(3)

    def process(kb, vb):
        for mi, qr in enumerate((q0_ref, q1_ref)):
            s = jnp.dot(kb, qr[...], preferred_element_type=F32)
            m_old = m_ref[mi]
            m_new = jnp.maximum(m_old, jnp.max(s, axis=0, keepdims=True))
            alpha = jnp.exp2(m_old - m_new)
            p = jnp.exp2(s - m_new)
            l_ref[mi] = alpha * l_ref[mi] + jnp.sum(p, axis=0, keepdims=True)
            pv = lax.dot_general(vb, p.astype(BF16), (((0,), (0,)), ((), ())),
                                 preferred_element_type=F32)
            acc_ref[mi] = alpha * acc_ref[mi] + pv
            m_ref[mi] = m_new

    @pl.when(kj == 0)
    def _():
        qt = q_ref[...].astype(F32).T.astype(BF16)
        row = lax.broadcasted_iota(jnp.int32, qt.shape, 0)
        zero = jnp.zeros_like(qt)
        q0_ref[...] = jnp.where(row < DIFF_HEAD_DIM, qt, zero)
        q1_ref[...] = jnp.where(row >= DIFF_HEAD_DIM, qt, zero)
        m_ref[...] = jnp.full(m_ref.shape, NEG_BIG, F32)
        l_ref[...] = jnp.zeros(l_ref.shape, F32)
        acc_ref[...] = jnp.zeros(acc_ref.shape, F32)
        if has_ctx:
            process(kc_ref[...], vc_ref[...])

    process(k_ref[...], v_ref[...])

    @pl.when(kj == nk - 1)
    def _():
        lv = lam_ref[...]
        lam = (jnp.exp(jnp.sum(lv[0:1] * lv[1:2], keepdims=True))
               - jnp.exp(jnp.sum(lv[2:3] * lv[3:4], keepdims=True)) + lam_init)
        o = acc_ref[0] / l_ref[0] - lam * (acc_ref[1] / l_ref[1])
        ms = jnp.mean(o * o, axis=0, keepdims=True)
        o = o * lax.rsqrt(ms + NORM_EPS)
        o_ref[...] = (o.T * (og_ref[...] * (1.0 - lam_init))).astype(BF16)


def _attention(qh, kh, pa, lam_vecs, out_gain, lam_init, dims, *, ctx_queries):
    b, t, tc, d = dims
    v_col = SEG_V * (BRANCH_W // HEAD_W)
    ctx_blk0 = (b * t) // tc
    if ctx_queries:
        tq = tk = tc
        nq, nk = 1, 1
        q_row = lambda bb, qi: ctx_blk0 + bb
        k_row = lambda bb, kj: ctx_blk0 + bb
        n_rows = b * tc
        o_row = lambda bb, qi: bb
    else:
        tq = min(1024, t)
        tk = min(512, t)
        nq, nk = t // tq, t // tk
        q_row = lambda bb, qi: bb * nq + qi
        k_row = lambda bb, kj: bb * nk + kj
        n_rows = b * t
        o_row = q_row
    body = functools.partial(_attn_body, lam_init=lam_init, has_ctx=not ctx_queries, nk=nk)
    return pl.pallas_call(
        body,
        grid=(b, N_HEADS, nq, nk),
        in_specs=[
            pl.BlockSpec((4, DIFF_HEAD_DIM), lambda bb, h, qi, kj: (0, 0)),
            pl.BlockSpec((1, HEAD_W), lambda bb, h, qi, kj: (0, 0)),
            pl.BlockSpec((tq, HEAD_W), lambda bb, h, qi, kj: (q_row(bb, qi), h)),
            pl.BlockSpec((tk, HEAD_W), lambda bb, h, qi, kj: (k_row(bb, kj), h)),
            pl.BlockSpec((tk, HEAD_W), lambda bb, h, qi, kj: (k_row(bb, kj), v_col + h)),
            pl.BlockSpec((tc, HEAD_W), lambda bb, h, qi, kj: (ctx_blk0 + bb, h)),
            pl.BlockSpec((tc, HEAD_W), lambda bb, h, qi, kj: (ctx_blk0 + bb, v_col + h)),
        ],
        out_specs=pl.BlockSpec((tq, HEAD_W), lambda bb, h, qi, kj: (o_row(bb, qi), h)),
        out_shape=jax.ShapeDtypeStruct((n_rows, BRANCH_W), BF16),
        scratch_shapes=[
            pltpu.VMEM((HEAD_W, tq), BF16),
            pltpu.VMEM((HEAD_W, tq), BF16),
            pltpu.VMEM((2, 1, tq), F32),
            pltpu.VMEM((2, 1, tq), F32),
            pltpu.VMEM((2, HEAD_W, tq), F32),
        ],
        compiler_params=_cparams("parallel", "parallel", "parallel", "arbitrary"),
        name="diff_attn_ctx" if ctx_queries else "diff_attn",
    )(lam_vecs, out_gain, qh, kh, pa, kh, pa)


def _s5_matrices(lam_re, lam_im, log_step, b_re, b_im, c_re, c_im, n_levels):
    ll, hg, pp, gg = S5_CHUNK, S5_GROUP, S5_STATE, S5_GROUPS
    dt = jnp.exp(log_step)[:, :, None]
    lr, li = lam_re * dt, lam_im * dt

    def a_pow(tau):
        tau = tau.astype(F32)[:, None, None, None]
        mag = jnp.exp(lr * tau)
        return mag * jnp.cos(li * tau), mag * jnp.sin(li * tau)

    ar1, ai1 = a_pow(jnp.ones((1,)))
    nr, ni = ar1[0] - 1.0, ai1[0]
    den = lam_re * lam_re + lam_im * lam_im
    f_re = (nr * lam_re + ni * lam_im) / den
    f_im = (ni * lam_re - nr * lam_im) / den
    bb_re = f_re[..., None] * b_re - f_im[..., None] * b_im
    bb_im = f_re[..., None] * b_im + f_im[..., None] * b_re

    ar, ai = a_pow(jnp.arange(ll + 1))
    ca_re = c_re[None] * ar[:, :, :, None, :] - c_im[None] * ai[:, :, :, None, :]
    ca_im = c_re[None] * ai[:, :, :, None, :] + c_im[None] * ar[:, :, :, None, :]
    kk = (jnp.einsum('tdgop,dgph->tdgoh', ca_re, bb_re, precision=HI)
          - jnp.einsum('tdgop,dgph->tdgoh', ca_im, bb_im, precision=HI))
    jj = jnp.arange(ll)
    tau = jj[None, :] - jj[:, None]
    toep = kk[jnp.clip(tau, 0, ll)]
    toep = jnp.where((tau >= 0)[:, :, None, None, None, None], toep, 0.0)
    toep = toep.transpose(2, 3, 0, 5, 1, 4).reshape(2, gg, ll * hg, ll * hg)

    rev = ll - 1 - jj
    in_re = ar[rev][..., None] * bb_re[None] - ai[rev][..., None] * bb_im[None]
    in_im = ar[rev][..., None] * bb_im[None] + ai[rev][..., None] * bb_re[None]
    in_re = in_re.transpose(1, 2, 0, 4, 3).reshape(2, gg, ll * hg, pp)
    in_im = in_im.transpose(1, 2, 0, 4, 3).reshape(2, gg, ll * hg, pp)
    out_re = ca_re[1:].transpose(1, 2, 4, 0, 3).reshape(2, gg, pp, ll * hg)
    out_im = -ca_im[1:].transpose(1, 2, 4, 0, 3).reshape(2, gg, pp, ll * hg)

    def pair_diag(m):
        _, _, a, bdim = m.shape
        m = m.reshape(2, gg // 2, 2, a, bdim)
        z = jnp.zeros_like(m[:, :, 0])
        top = jnp.concatenate([m[:, :, 0], z], axis=-1)
        bot = jnp.concatenate([z, m[:, :, 1]], axis=-1)
        return jnp.concatenate([top, bot], axis=-2)

    lev = (ll * (2 ** jnp.arange(n_levels))).astype(F32)
    alr, ali = a_pow(lev)
    pad_lev = (-n_levels) % 8
    def lev_pairs(a):
        a = a.transpose(1, 2, 0, 3).reshape(2, gg // 2, 2, n_levels, pp)
        a = jnp.concatenate([a[:, :, 0], a[:, :, 1]], axis=-1)
        return jnp.pad(a, ((0, 0), (0, 0), (0, pad_lev), (0, 0)))
    return dict(
        toep=toep.reshape(2, gg // 2, 2, ll * hg, ll * hg).astype(BF16),
        in_re=pair_diag(in_re).astype(BF16), in_im=pair_diag(in_im).astype(BF16),
        out_re=pair_diag(out_re).astype(BF16), out_im=pair_diag(out_im).astype(BF16),
        al_re=lev_pairs(alr), al_im=lev_pairs(ali))


def _s5_body(u_ref, t_ref, inr_ref, ini_ref, outr_ref, outi_ref, alr_ref, ali_ref, y_ref,
             sr_ref, si_ref, *, nb, nc, pad, n_levels):
    u = u_ref[0, 0]
    vr = jnp.dot(u, inr_ref[0, 0], preferred_element_type=F32)
    vi = jnp.dot(u, ini_ref[0, 0], preferred_element_type=F32)
    zeros = jnp.zeros((pad, LANES), F32)
    for bb in range(nb):
        sr_ref[bb, 0:pad, :] = zeros
        si_ref[bb, 0:pad, :] = zeros
        sr_ref[bb, pad:pad + nc, :] = vr[bb * nc:(bb + 1) * nc]
        si_ref[bb, pad:pad + nc, :] = vi[bb * nc:(bb + 1) * nc]
    for lev in range(n_levels):
        dd = 1 << lev
        a_r = alr_ref[0, 0, lev:lev + 1, :]
        a_i = ali_ref[0, 0, lev:lev + 1, :]
        for bb in range(nb):
            cr = sr_ref[bb, pad:pad + nc, :]
            ci = si_ref[bb, pad:pad + nc, :]
            pr = sr_ref[bb, pad - dd:pad - dd + nc, :]
            pi = si_ref[bb, pad - dd:pad - dd + nc, :]
            sr_ref[bb, pad:pad + nc, :] = cr + a_r * pr - a_i * pi
            si_ref[bb, pad:pad + nc, :] = ci + a_r * pi + a_i * pr
    half = S5_CHUNK * S5_GROUP
    for bb in range(nb):
        er = sr_ref[bb, pad - 1:pad - 1 + nc, :].astype(BF16)
        ei = si_ref[bb, pad - 1:pad - 1 + nc, :].astype(BF16)
        ub = u[bb * nc:(bb + 1) * nc]
        y = (jnp.dot(er, outr_ref[0, 0], preferred_element_type=F32)
             + jnp.dot(ei, outi_ref[0, 0], preferred_element_type=F32))
        ya = jnp.dot(ub[:, :half], t_ref[0, 0, 0], preferred_element_type=F32)
        yb = jnp.dot(ub[:, half:], t_ref[0, 0, 1], preferred_element_type=F32)
        y_ref[0, 0, bb * nc:(bb + 1) * nc, :] = (y + jnp.concatenate([ya, yb], axis=1)).astype(BF16)


def _s5_levels(nc):
    return max(1, (nc - 1).bit_length())


def _s5_scan(ug, mats, nb, nc):
    n_levels = _s5_levels(nc)
    pad = max(8, 1 << (n_levels - 1))
    gp = S5_GROUPS // 2
    w = 2 * S5_CHUNK * S5_GROUP
    lev_rows = mats["al_re"].shape[2]
    idx = lambda dd, g: (dd, g, 0, 0)
    return pl.pallas_call(
        functools.partial(_s5_body, nb=nb, nc=nc, pad=pad, n_levels=n_levels),
        grid=(2, gp),
        in_specs=[
            pl.BlockSpec((1, 1, nb * nc, w), idx),
            pl.BlockSpec((1, 1, 2, w // 2, w // 2), lambda dd, g: (dd, g, 0, 0, 0)),
            pl.BlockSpec((1, 1, w, LANES), idx),
            pl.BlockSpec((1, 1, w, LANES), idx),
            pl.BlockSpec((1, 1, LANES, w), idx),
            pl.BlockSpec((1, 1, LANES, w), idx),
            pl.BlockSpec((1, 1, lev_rows, LANES), idx),
            pl.BlockSpec((1, 1, lev_rows, LANES), idx),
        ],
        out_specs=pl.BlockSpec((1, 1, nb * nc, w), idx),
        out_shape=jax.ShapeDtypeStruct((2, gp, nb * nc, w), BF16),
        scratch_shapes=[pltpu.VMEM((nb, pad + nc, LANES), F32), pltpu.VMEM((nb, pad + nc, LANES), F32)],
        compiler_params=_cparams("parallel", "parallel"),
        name="s5_scan",
    )(ug, mats["toep"], mats["in_re"], mats["in_im"], mats["out_re"], mats["out_im"],
      mats["al_re"], mats["al_im"])


def _s5_mixer(pa, mats, dims):
    b, t, tc, d = dims
    ts = t + tc
    nc = ts // S5_CHUNK
    gp = S5_GROUPS // 2
    u = pa[:, SEG_S5 * BRANCH_W:(SEG_S5 + 1) * BRANCH_W]
    ul = u[:b * t].reshape(b, t, BRANCH_W)
    uc = u[b * t:].reshape(b, tc, BRANCH_W)
    seq0 = jnp.concatenate([uc, ul], axis=1)
    seq1 = jnp.concatenate([uc[:, ::-1], ul[:, ::-1]], axis=1)
    seq = jnp.stack([seq0, seq1])
    ug = seq.reshape(2, b, nc, S5_CHUNK, gp, 2, S5_GROUP)
    ug = ug.transpose(0, 4, 1, 2, 5, 3, 6).reshape(2, gp, b * nc, 2 * S5_CHUNK * S5_GROUP)
    yg = _s5_scan(ug, mats, b, nc)
    y = yg.reshape(2, gp, b, nc, 2, S5_CHUNK, S5_GROUP).transpose(0, 2, 3, 5, 1, 4, 6)
    y = y.reshape(2, b, ts, BRANCH_W)
    y0c, y0l = y[0, :, :tc], y[0, :, tc:]
    y1c, y1l = y[1, :, :tc][:, ::-1], y[1, :, tc:][:, ::-1]
    rows = lambda yl, yc: jnp.concatenate([yl.reshape(b * t, BRANCH_W), yc.reshape(b * tc, BRANCH_W)])
    return rows(y0l, y0c), rows(y1l, y1c)


def _log_sigmoid(x):
    return -(jnp.maximum(-x, 0.0) + jnp.log1p(jnp.exp(-jnp.abs(x))))


def _mlstm_body(q_ref, k_ref, v_ref, g_ref, gb_ref, h_ref, c_ref, n_ref, m_ref):
    dd = pl.program_id(1)
    fwd = dd == 0

    @pl.when(pl.program_id(2) == 0)
    def _():
        c_ref[...] = jnp.zeros(c_ref.shape, F32)
        n_ref[...] = jnp.zeros(n_ref.shape, F32)
        m_ref[...] = jnp.zeros(m_ref.shape, F32)

    ll = MLSTM_CHUNK
    row = lax.broadcasted_iota(jnp.int32, (ll, ll), 0)
    col = lax.broadcasted_iota(jnp.int32, (ll, ll), 1)
    order = (row - col) * jnp.where(fwd, 1, -1)
    allowed = order >= 0
    tri = jnp.where(allowed, 1.0, 0.0)
    tri_t = jnp.where(order <= 0, 1.0, 0.0)

    g = g_ref[...] + gb_ref[...]
    lf = _log_sigmoid(g)
    g_t = g.T
    lf_t = lf.T
    outs = []
    for h in range(N_HEADS):
        def pick_col(a, base):
            return jnp.where(fwd, a[:, base + h:base + h + 1], a[:, 8 + base + h:8 + base + h + 1])

        def pick_row(a, base):
            return jnp.where(fwd, a[base + h:base + h + 1, :], a[8 + base + h:8 + base + h + 1, :])

        ig_col, lf_col = pick_col(g, 0), pick_col(lf, 4)
        ig_row, lf_row = pick_row(g_t, 0), pick_row(lf_t, 4)
        cum_col = jnp.sum(tri * lf_row, axis=1, keepdims=True)
        cum_row = jnp.sum(tri_t * lf_col, axis=0, keepdims=True)
        gtot = jnp.sum(lf_row, axis=1, keepdims=True)

        hs = slice(h * HEAD_W, (h + 1) * HEAD_W)
        q = q_ref[:, hs]
        k = k_ref[:, hs]
        v = v_ref[:, hs]
        c0 = c_ref[h]
        n0 = n_ref[h]
        m0 = m_ref[h][:, 0:1]

        dmat = jnp.where(allowed, cum_col - cum_row + ig_row, NEG_BIG)
        inter = cum_col + m0
        m_t = jnp.maximum(inter, jnp.max(dmat, axis=1, keepdims=True))
        pm = jnp.exp(dmat - m_t)
        ei = jnp.exp(inter - m_t)
        qk = lax.dot_general(q, k, (((1,), (1,)), ((), ())), preferred_element_type=F32)
        wq = pm * qk
        qc = lax.dot_general(q, c0.astype(BF16), (((1,), (1,)), ((), ())), preferred_element_type=F32)
        num = jnp.dot(wq.astype(BF16), v, preferred_element_type=F32) + ei * qc
        qf = q.astype(F32)
        den = jnp.sum(wq, axis=1, keepdims=True) + ei * jnp.sum(qf * n0, axis=1, keepdims=True)
        outs.append(num / jnp.maximum(jnp.abs(den), jnp.exp(-m_t)))

        w_col = gtot - cum_col + ig_col
        mw = jnp.max(w_col, axis=0, keepdims=True)
        ew = jnp.exp(w_col - mw)
        kf = k.astype(F32)
        vw = (ew * v.astype(F32)).astype(BF16)
        kv = lax.dot_general(vw, k, (((0,), (0,)), ((), ())), preferred_element_type=F32)
        ks = jnp.sum(ew * kf, axis=0, keepdims=True)
        m_new = jnp.maximum(gtot + m0, mw)
        a = jnp.exp(gtot + m0 - m_new)
        e = jnp.exp(mw - m_new)
        c_ref[h] = a * c0 + e * kv
        n_ref[h] = a * n0 + e * ks
        m_ref[h] = jnp.broadcast_to(m_new, (1, LANES))
    h_ref[0] = jnp.concatenate(outs, axis=1).astype(BF16)


def _mlstm(mq, mk, pa, pg, gate_bias, dims):
    b, t, tc, d = dims
    r = pa.shape[0]
    ll = MLSTM_CHUNK
    nctx, nlat = tc // ll, t // ll
    ctx0 = (b * t) // ll

    def rb(bb, dd, c):
        is_ctx = c < nctx
        cc = jnp.where(dd == 0, c, nctx - 1 - c)
        cl = jnp.where(dd == 0, c - nctx, nlat - 1 - (c - nctx))
        return jnp.where(is_ctx, ctx0 + bb * nctx + cc, bb * nlat + cl)

    return pl.pallas_call(
        _mlstm_body,
        grid=(b, 2, nctx + nlat),
        in_specs=[
            pl.BlockSpec((ll, BRANCH_W), lambda bb, dd, c: (rb(bb, dd, c), 0)),
            pl.BlockSpec((ll, BRANCH_W), lambda bb, dd, c: (rb(bb, dd, c), 0)),
            pl.BlockSpec((ll, BRANCH_W), lambda bb, dd, c: (rb(bb, dd, c), SEG_MV)),
            pl.BlockSpec((ll, LANES), lambda bb, dd, c: (rb(bb, dd, c), 0)),
            pl.BlockSpec((1, LANES), lambda bb, dd, c: (0, 0)),
        ],
        out_specs=pl.BlockSpec((1, ll, BRANCH_W), lambda bb, dd, c: (dd, rb(bb, dd, c), 0)),
        out_shape=jax.ShapeDtypeStruct((2, r, BRANCH_W), BF16),
        scratch_shapes=[
            pltpu.VMEM((N_HEADS, HEAD_W, HEAD_W), F32),
            pltpu.VMEM((N_HEADS, 1, HEAD_W), F32),
            pltpu.VMEM((N_HEADS, 1, LANES), F32),
        ],
        compiler_params=_cparams("parallel", "parallel", "arbitrary"),
        name="mlstm",
    )(mq, mk, pa, pg, gate_bias)


def _merge_body(ya_ref, y0_ref, y1_ref, u_ref, h0_ref, h1_ref, mo_ref, ga_ref, gb_ref, gc_ref, x_ref, m_ref,
                d_ref, wglu_ref, wb_ref, wo_ref, wr_ref, wrt_ref,
                x1_ref, h2_ref, aff_ref, afft_ref):
    mod = m_ref[0]
    ys = d_ref[...] * u_ref[...].astype(F32) + y0_ref[...].astype(F32) + y1_ref[...].astype(F32)
    gl = jax.nn.gelu(ys)
    yb = gl * jax.nn.sigmoid(jnp.dot(gl.astype(BF16), wglu_ref[...], preferred_element_type=F32))
    yc = (h0_ref[0].astype(F32) + h1_ref[0].astype(F32)) * jax.nn.sigmoid(mo_ref[...].astype(F32))
    gate = lambda ref: jax.nn.sigmoid(ref[...].astype(F32))
    mixed = (gate(ga_ref) * jnp.dot(ya_ref[...], wb_ref[0], preferred_element_type=F32)
             + gate(gb_ref) * jnp.dot(yb.astype(BF16), wb_ref[1], preferred_element_type=F32)
             + gate(gc_ref) * jnp.dot(yc.astype(BF16), wb_ref[2], preferred_element_type=F32))
    out = jnp.dot(mixed.astype(BF16), wo_ref[...], preferred_element_type=F32)
    x1 = x_ref[...] + mod[2:3, :] * out
    x1_ref[...] = x1
    h2 = _modulated_norm(x1, mod, 3, 4)
    h2_ref[...] = h2.astype(BF16)
    logits = jnp.dot(h2, wr_ref[...], precision=HI, preferred_element_type=F32)
    lane = lax.broadcasted_iota(jnp.int32, logits.shape, 1)
    logits = jnp.where(lane < N_EXPERTS, logits, NEG_BIG)
    ex = jnp.exp(logits - jnp.max(logits, axis=1, keepdims=True))
    aff_ref[...] = ex / jnp.sum(ex, axis=1, keepdims=True)
    lt = lax.dot_general(wrt_ref[...], h2, (((1,), (1,)), ((), ())), precision=HI,
                         preferred_element_type=F32)
    et = jnp.exp(lt - jnp.max(lt, axis=0, keepdims=True))
    afft_ref[...] = et / jnp.sum(et, axis=0, keepdims=True)


def _merge(ya, y0, y1, pa, hm, xa, mod, s5_d, w_glu, w_branch, w_out, wr_pad, wr_t, dims):
    b, t, tc, d = dims
    r = xa.shape[0]
    tm = 256
    gseg = SEG_GATE * BRANCH_W // d
    rowblk = lambda width, col=0: pl.BlockSpec((tm, width), lambda i: (i, col))
    full2 = lambda shape: pl.BlockSpec(shape, lambda i: (0, 0))
    return pl.pallas_call(
        _merge_body,
        grid=(r // tm,),
        in_specs=[
            rowblk(BRANCH_W), rowblk(BRANCH_W), rowblk(BRANCH_W),
            rowblk(BRANCH_W, SEG_S5),
            pl.BlockSpec((1, tm, BRANCH_W), lambda i: (0, i, 0)),
            pl.BlockSpec((1, tm, BRANCH_W), lambda i: (1, i, 0)),
            rowblk(BRANCH_W, SEG_MO),
            rowblk(d, gseg), rowblk(d, gseg + 1), rowblk(d, gseg + 2),
            rowblk(d),
            pl.BlockSpec((1, N_MOD, d), lambda i: (_group_of_block(i, t // tm, b), 0, 0)),
            full2((1, BRANCH_W)),
            full2((BRANCH_W, BRANCH_W)),
            pl.BlockSpec((3, BRANCH_W, d), lambda i: (0, 0, 0)),
            full2((d, d)),
            full2((d, LANES)),
            full2((N_EXPERTS, d)),
        ],
        out_specs=[
            rowblk(d), rowblk(d), rowblk(LANES),
            pl.BlockSpec((N_EXPERTS, tm), lambda i: (0, i)),
        ],
        out_shape=[
            jax.ShapeDtypeStruct((r, d), F32),
            jax.ShapeDtypeStruct((r, d), BF16),
            jax.ShapeDtypeStruct((r, LANES), F32),
            jax.ShapeDtypeStruct((N_EXPERTS, r), F32),
        ],
        compiler_params=_cparams("parallel"),
        name="merge",
    )(ya, y0, y1, pa, hm, hm, pa, pa, pa, pa, xa, mod, s5_d, w_glu, w_branch, w_out, wr_pad, wr_t)


def _route_body(a_ref, tri_ref, low_ref, pos_ref, offs_ref, *, cap):
    a = a_ref[0]
    e, nb, _ = a.shape
    bits = pltpu.bitcast(a, jnp.int32)

    def count(mask):
        c = jnp.sum(jnp.where(mask, 1.0, 0.0), axis=2, keepdims=True)
        return jnp.sum(c, axis=1, keepdims=True)

    def step(i, thr):
        cand = thr | jnp.left_shift(jnp.int32(1), 30 - i)
        return jnp.where(count(bits >= cand) >= cap, cand, thr)

    thr = lax.fori_loop(0, 31, step, jnp.zeros((e, 1, 1), jnp.int32))
    gt = bits > thr
    eq = bits == thr
    need = cap - count(gt)

    tri = tri_ref[...]
    low = low_ref[...]

    def exclusive_prefix(x):
        x2 = x.reshape(e * nb, LANES).astype(BF16)
        incl = jnp.dot(x2, tri, preferred_element_type=F32)
        before = jnp.sum(jnp.dot(low, x2, preferred_element_type=F32), axis=1, keepdims=True)
        return (incl - x2.astype(F32) + before).reshape(e, nb, LANES), before.reshape(e, nb, 1)

    eq_rank, _ = exclusive_prefix(jnp.where(eq, 1.0, 0.0))
    sel = gt | (eq & (eq_rank < need))
    pos, before = exclusive_prefix(jnp.where(sel, 1.0, 0.0))
    pos_ref[0] = jnp.where(sel, pos, -1.0)
    offs_ref[0] = before.astype(jnp.int32)


def _route(aff3, cap):
    ns, e, nb, _ = aff3.shape
    i = jnp.arange(LANES)
    tri = (i[:, None] <= i[None, :]).astype(BF16)
    r = jnp.arange(e * nb)
    low = ((r[:, None] // nb == r[None, :] // nb) & (r[None, :] < r[:, None])).astype(BF16)
    return pl.pallas_call(
        functools.partial(_route_body, cap=cap),
        grid=(ns,),
        in_specs=[
            pl.BlockSpec((1, e, nb, LANES), lambda s: (s, 0, 0, 0)),
            pl.BlockSpec((LANES, LANES), lambda s: (0, 0)),
            pl.BlockSpec((e * nb, e * nb), lambda s: (0, 0)),
        ],
        out_specs=[
            pl.BlockSpec((1, e, nb, LANES), lambda s: (s, 0, 0, 0)),
            pl.BlockSpec((1, e, nb, 1), lambda s: (s, 0, 0, 0)),
        ],
        out_shape=[
            jax.ShapeDtypeStruct((ns, e, nb, LANES), F32),
            jax.ShapeDtypeStruct((ns, e, nb, 1), jnp.int32),
        ],
        compiler_params=_cparams("parallel"),
        name="route",
    )(aff3, tri, low)


SLOT_ALIGN = 16


def _dispatch_body(st_ref, h_ref, p_ref, o_ref, *, tb, win, nj):
    ns, e, j = pl.program_id(0), pl.program_id(1), pl.program_id(2)

    @pl.when(j == 0)
    def _():
        o_ref[...] = jnp.zeros(o_ref.shape, o_ref.dtype)

    start = st_ref[(ns * N_EXPERTS + e) * nj + j]
    base = pl.multiple_of((start // SLOT_ALIGN) * SLOT_ALIGN, SLOT_ALIGN)
    slot = (base + lax.broadcasted_iota(jnp.int32, (win, tb), 0)).astype(F32)
    onehot = jnp.where(slot == p_ref[0], 1.0, 0.0).astype(BF16)
    rows = jnp.dot(onehot, h_ref[...], preferred_element_type=F32)
    cur = o_ref[0, 0, pl.ds(base, win), :]
    o_ref[0, 0, pl.ds(base, win), :] = cur + rows.astype(o_ref.dtype)


def _dispatch(starts, h2, pos_rows, row0, ns, n, cap, tb):
    d = h2.shape[1]
    nj = n // tb
    win = tb + SLOT_ALIGN
    capp = cap + win
    blk0 = row0 // tb
    grid_spec = pltpu.PrefetchScalarGridSpec(
        num_scalar_prefetch=1,
        grid=(ns, N_EXPERTS, nj),
        in_specs=[
            pl.BlockSpec((tb, d), lambda s, e, j, st: (blk0 + s * nj + j, 0)),
            pl.BlockSpec((1, 1, tb), lambda s, e, j, st: ((s * N_EXPERTS + e) * nj + j, 0, 0)),
        ],
        out_specs=pl.BlockSpec((1, 1, capp, d), lambda s, e, j, st: (s, e, 0, 0)),
    )
    return pl.pallas_call(
        functools.partial(_dispatch_body, tb=tb, win=win, nj=nj),
        grid_spec=grid_spec,
        out_shape=jax.ShapeDtypeStruct((ns, N_EXPERTS, capp, d), BF16),
        compiler_params=_cparams("parallel", "parallel", "arbitrary"),
        name="dispatch",
    )(starts, h2, pos_rows)


def _expert_body(x_ref, wg_ref, wu_ref, wd_ref, y_ref, *, fc):
    x = x_ref[0, 0]
    f = wg_ref.shape[2]
    acc = jnp.zeros((x.shape[0], wd_ref.shape[2]), F32)
    for f0 in range(0, f, fc):
        g = jnp.dot(x, wg_ref[0, :, f0:f0 + fc], preferred_element_type=F32)
        u = jnp.dot(x, wu_ref[0, :, f0:f0 + fc], preferred_element_type=F32)
        hid = (g * jax.nn.sigmoid(g) * u).astype(BF16)
        acc = acc + jnp.dot(hid, wd_ref[0, f0:f0 + fc, :], preferred_element_type=F32)
    y_ref[0, 0] = acc.astype(BF16)


def _experts(xs, w_gate, w_up, w_down, cap):
    ns, e, _, d = xs.shape
    f = w_gate.shape[2]
    ts = min(512, cap)
    return pl.pallas_call(
        functools.partial(_expert_body, fc=min(512, f)),
        grid=(e, ns, cap // ts),
        in_specs=[
            pl.BlockSpec((1, 1, ts, d), lambda ee, s, i: (s, ee, i, 0)),
            pl.BlockSpec((1, d, f), lambda ee, s, i: (ee, 0, 0)),
            pl.BlockSpec((1, d, f), lambda ee, s, i: (ee, 0, 0)),
            pl.BlockSpec((1, f, d), lambda ee, s, i: (ee, 0, 0)),
        ],
        out_specs=pl.BlockSpec((1, 1, ts, d), lambda ee, s, i: (s, ee, i, 0)),
        out_shape=jax.ShapeDtypeStruct((ns, e, cap, d), BF16),
        compiler_params=_cparams("parallel", "parallel", "parallel"),
        name="experts",
    )(xs, w_gate, w_up, w_down)


def _combine_body(st_ref, ya_ref, yb_ref, p_ref, aff_ref, x_ref, m_ref, o_ref, *, sb, nj):
    ns, j, e = pl.program_id(0), pl.program_id(1), pl.program_id(2)

    @pl.when(e == 0)
    def _():
        o_ref[...] = jnp.zeros(o_ref.shape, F32)

    tb = o_ref.shape[0]
    start = st_ref[(ns * N_EXPERTS + e) * nj + j]
    a = start // sb
    lane = lax.broadcasted_iota(jnp.int32, (tb, LANES), 1)
    mine = lane == e
    pos = jnp.sum(jnp.where(mine, p_ref[...], 0.0), axis=1, keepdims=True)
    aff = jnp.sum(jnp.where(mine, aff_ref[...], 0.0), axis=1, keepdims=True)
    slot = (a * sb + lax.broadcasted_iota(jnp.int32, (tb, sb), 1)).astype(F32)
    oh_a = jnp.where(pos == slot, 1.0, 0.0).astype(BF16)
    oh_b = jnp.where(pos == slot + float(sb), 1.0, 0.0).astype(BF16)
    got = (jnp.dot(oh_a, ya_ref[0, 0], preferred_element_type=F32)
           + jnp.dot(oh_b, yb_ref[0, 0], preferred_element_type=F32))
    o_ref[...] += aff * got

    @pl.when(e == N_EXPERTS - 1)
    def _():
        o_ref[...] = x_ref[...] + m_ref[0, 5:6, :] * o_ref[...]


def _combine(starts, ys, pos_t, aff, x1, mod, row0, ns, n, cap, tb, mod_group):
    d = x1.shape[1]
    nj = n // tb
    sb = min(tb, cap)
    nsb = cap // sb
    blk0 = row0 // tb

    def ya_map(s, j, e, st):
        return (s, e, st[(s * N_EXPERTS + e) * nj + j] // sb, 0)

    def yb_map(s, j, e, st):
        return (s, e, jnp.minimum(st[(s * N_EXPERTS + e) * nj + j] // sb + 1, nsb - 1), 0)

    row = lambda s, j, e, st: (blk0 + s * nj + j, 0)
    grid_spec = pltpu.PrefetchScalarGridSpec(
        num_scalar_prefetch=1,
        grid=(ns, nj, N_EXPERTS),
        in_specs=[
            pl.BlockSpec((1, 1, sb, d), ya_map),
            pl.BlockSpec((1, 1, sb, d), yb_map),
            pl.BlockSpec((tb, LANES), row),
            pl.BlockSpec((tb, LANES), row),
            pl.BlockSpec((tb, d), row),
            pl.BlockSpec((1, N_MOD, d), lambda s, j, e, st: (mod_group(s), 0, 0)),
        ],
        out_specs=pl.BlockSpec((tb, d), lambda s, j, e, st: (s * nj + j, 0)),
    )
    return pl.pallas_call(
        functools.partial(_combine_body, sb=sb, nj=nj),
        grid_spec=grid_spec,
        out_shape=jax.ShapeDtypeStruct((ns * n, d), F32),
        compiler_params=_cparams("parallel", "parallel", "arbitrary"),
        name="combine",
    )(starts, ys, ys, pos_t, aff, x1, mod)


def _expert_choice(h2, aff, aff_t, x1, mod, w_gate, w_up, w_down, row0, ns, n, mod_group):
    r, d = h2.shape
    e = N_EXPERTS
    cap = CAPACITY_FACTOR * n // e
    tb = min(256, n)
    n_pad = max(n, 8 * LANES)
    a = aff_t[:, row0:row0 + ns * n].reshape(e, ns, n).transpose(1, 0, 2)
    if n_pad > n:
        a = jnp.concatenate([a, jnp.full((ns, e, n_pad - n), -1.0, F32)], axis=2)
    pos, offs = _route(a.reshape(ns, e, n_pad // LANES, LANES), cap)
    pos = pos.reshape(ns, e, n_pad)[:, :, :n]
    starts = offs.reshape(ns, e, n_pad // LANES)[:, :, :n // LANES]
    starts = starts[:, :, ::tb // LANES].reshape(-1)
    nj = n // tb
    xs = _dispatch(starts, h2, pos.reshape(ns * e * nj, 1, tb), row0, ns, n, cap, tb)
    ys = _experts(xs, w_gate, w_up, w_down, cap)
    pos_t = jnp.pad(pos.transpose(0, 2, 1).reshape(ns * n, e), ((0, 0), (0, LANES - e)), constant_values=-1.0)
    pos_t = jnp.pad(pos_t, ((row0, r - row0 - ns * n), (0, 0)))
    return _combine(starts, ys, pos_t, aff, x1, mod, row0, ns, n, cap, tb, mod_group)


def _rope_tables(b, t, tc):
    n_freq = DIFF_HEAD_DIM // 4
    inv_freq = ROPE_BASE ** (-jnp.arange(n_freq, dtype=F32) / n_freq)
    pos = jnp.arange(t)
    row = (pos // GRID_W).astype(F32)
    col = (pos % GRID_W).astype(F32)
    ang = jnp.concatenate([row[:, None] * inv_freq, col[:, None] * inv_freq], axis=-1)
    cos, sin = jnp.cos(ang), jnp.sin(ang)
    cos_seg = jnp.concatenate([cos, cos], axis=-1)
    sin_seg = jnp.concatenate([-sin, sin], axis=-1)
    cos_t = jnp.tile(cos_seg, (b, LANES // DIFF_HEAD_DIM))
    sin_t = jnp.tile(sin_seg, (b, LANES // DIFF_HEAD_DIM))
    cos_t = jnp.concatenate([cos_t, jnp.ones((b * tc, LANES), F32)])
    sin_t = jnp.concatenate([sin_t, jnp.zeros((b * tc, LANES), F32)])
    return cos_t, sin_t


def kernel(x, c, ctx, c_ctx, w_mod, b_mod, w_in, attn_q_gain, attn_k_gain, attn_lambda, attn_out_gain,
           s5_lam_re, s5_lam_im, s5_log_step, s5_b_re, s5_b_im, s5_c_re, s5_c_im, s5_d, s5_w_glu,
           mlstm_conv_w, mlstm_conv_b, mlstm_i_bias, mlstm_f_bias,
           w_branch, w_out, w_router, w_exp_gate, w_exp_up, w_exp_down):
    b, t, d = x.shape
    tc = ctx.shape[1]
    n_layers = w_mod.shape[0]
    dims = (b, t, tc, d)
    assert b + 1 <= 8 and t % 512 == 0 and tc % 256 == 0 and (b * tc) % 512 == 0

    xa = jnp.concatenate([x.reshape(b * t, d), ctx.reshape(b * tc, d)])
    cvec = jnp.zeros((8, d), F32).at[:b].set(c).at[b].set(c_ctx)
    mod_all = _mod_vectors(cvec, w_mod, b_mod).reshape(n_layers, 8, N_MOD, d)

    n_main = 8 * BRANCH_W
    wa = jnp.concatenate([w_in[:, :, :n_main], w_in[:, :, n_main + N_GATES:]], axis=2).astype(BF16)
    wg = jnp.pad(w_in[:, :, n_main:n_main + N_GATES], ((0, 0), (0, 0), (0, LANES - N_GATES))).astype(BF16)
    cos_t, sin_t = _rope_tables(b, t, tc)
    seg = jnp.arange(BRANCH_W) // DIFF_HEAD_DIM
    seg_ones = (seg[:, None] == seg[None, :]).astype(BF16)
    n_seg = BRANCH_W // DIFF_HEAD_DIM
    gq = jnp.tile(attn_q_gain, (1, n_seg))[:, None, :] * (DIFF_HEAD_DIM ** -0.5 * math.log2(math.e))
    gk = jnp.tile(attn_k_gain, (1, n_seg))[:, None, :]
    conv_w = jnp.pad(mlstm_conv_w, ((0, 0), (0, 8 - CONV_K), (0, 0)))
    gate_bias = jnp.stack([mlstm_i_bias, mlstm_f_bias], axis=2).reshape(n_layers, 1, N_GATES)
    gate_bias = jnp.pad(gate_bias, ((0, 0), (0, 0), (0, LANES - N_GATES)))
    wr_pad = jnp.pad(w_router, ((0, 0), (0, 0), (0, LANES - N_EXPERTS)))
    wr_t = jnp.swapaxes(w_router, 1, 2)
    n_chunks = (t + tc) // S5_CHUNK
    n_levels = _s5_levels(n_chunks)

    for l in range(n_layers):
        with_ctx = l != n_layers - 1
        lam_init = 0.8 - 0.6 * math.exp(-0.3 * l)
        mod = mod_all[l]
        pa, pg = _project(xa, mod, wa[l], wg[l], dims)
        qh, kh, mq, mk = _prepare(pa, cos_t, sin_t, gq[l], gk[l], seg_ones, conv_w[l],
                                  mlstm_conv_b[l][None, :], dims)
        og = attn_out_gain[l][None, :]
        ya_l = _attention(qh, kh, pa, attn_lambda[l], og, lam_init, dims, ctx_queries=False)
        if with_ctx:
            ya_c = _attention(qh, kh, pa, attn_lambda[l], og, lam_init, dims, ctx_queries=True)
        else:
            ya_c = jnp.zeros((b * tc, BRANCH_W), BF16)
        ya = jnp.concatenate([ya_l, ya_c])
        mats = _s5_matrices(s5_lam_re[l], s5_lam_im[l], s5_log_step[l], s5_b_re[l], s5_b_im[l],
                            s5_c_re[l], s5_c_im[l], n_levels)
        y0, y1 = _s5_mixer(pa, mats, dims)
        hm = _mlstm(mq, mk, pa, pg, gate_bias[l], dims)
        x1, h2, aff, aff_t = _merge(ya, y0, y1, pa, hm, xa, mod, s5_d[l][None, :],
                                    s5_w_glu[l].astype(BF16), w_branch[l].astype(BF16),
                                    w_out[l].astype(BF16), wr_pad[l], wr_t[l], dims)
        wge, wue, wde = (w_exp_gate[l].astype(BF16), w_exp_up[l].astype(BF16), w_exp_down[l].astype(BF16))
        x2_l = _expert_choice(h2, aff, aff_t, x1, mod, wge, wue, wde, 0, b, t, lambda s: s)
        if with_ctx:
            x2_c = _expert_choice(h2, aff, aff_t, x1, mod, wge, wue, wde, b * t, b, tc, lambda s: b)
        else:
            x2_c = x1[b * t:]
        xa = jnp.concatenate([x2_l, x2_c])
    return xa[:b * t].reshape(b, t, d)
```

```python
import functools
import math

import jax
import jax.numpy as jnp
from jax import lax
from jax.experimental import pallas as pl
from jax.experimental.pallas import tpu as pltpu

F32 = jnp.float32
BF16 = jnp.bfloat16
HI = lax.Precision.HIGHEST

N_MOD = 6
NORM_EPS = 1e-6
GRID_W = 64
ROPE_BASE = 10000.0
N_HEADS = 4
DIFF_HEAD_DIM = 64
HEAD_W = 128
BRANCH_W = 512
S5_GROUPS = 32
S5_GROUP = 16
S5_STATE = 64
S5_CHUNK = 16
S5_SLAB = 8
S5_VMEM_LIMIT = 58 * 1024 * 1024
MLSTM_CHUNK = 128
CONV_K = 5
N_GATES = 16
N_EXPERTS = 16
CAPACITY_FACTOR = 2
LANES = 128
VMEM_LIMIT = 52 * 1024 * 1024
NEG_BIG = -1e30

SEG_Q, SEG_K, SEG_V, SEG_S5, SEG_MQ, SEG_MK, SEG_MV, SEG_MO, SEG_GATE = range(9)
PA_WIDTH = 8 * BRANCH_W + 3 * 1024


def _cparams(*sem):
    return pltpu.CompilerParams(dimension_semantics=sem, vmem_limit_bytes=VMEM_LIMIT)


def _mod_body(c_ref, w_ref, b_ref, o_ref):
    cv = c_ref[...]
    s = cv * jax.nn.sigmoid(cv)
    o_ref[0] = jnp.dot(s, w_ref[0], precision=HI, preferred_element_type=F32) + b_ref[0]


def _mod_vectors(cvec, w_mod, b_mod):
    n_layers, d, n = w_mod.shape
    tn = n // 4
    return pl.pallas_call(
        _mod_body,
        grid=(n_layers, n // tn),
        in_specs=[
            pl.BlockSpec((8, d), lambda l, j: (0, 0)),
            pl.BlockSpec((1, d, tn), lambda l, j: (l, 0, j)),
            pl.BlockSpec((1, 1, tn), lambda l, j: (l, 0, j)),
        ],
        out_specs=pl.BlockSpec((1, 8, tn), lambda l, j: (l, 0, j)),
        out_shape=jax.ShapeDtypeStruct((n_layers, 8, n), F32),
        compiler_params=_cparams("parallel", "parallel"),
        name="mod_vectors",
    )(cvec, w_mod, b_mod.reshape(n_layers, 1, n))


def _modulated_norm(x, mod, i_shift, i_scale):
    ms = jnp.mean(x * x, axis=-1, keepdims=True)
    xn = x * lax.rsqrt(ms + NORM_EPS)
    return xn * (1.0 + mod[i_scale:i_scale + 1, :]) + mod[i_shift:i_shift + 1, :]


def _proj_body(x_ref, m_ref, w_ref, wg_ref, pa_ref, pg_ref, hn_ref):
    @pl.when(pl.program_id(1) == 0)
    def _():
        hb = _modulated_norm(x_ref[...], m_ref[0], 0, 1).astype(BF16)
        hn_ref[...] = hb
        pg_ref[...] = jnp.dot(hb, wg_ref[...], preferred_element_type=F32)

    pa_ref[...] = jnp.dot(hn_ref[...], w_ref[...], preferred_element_type=F32).astype(BF16)


def _group_of_block(i, blocks_per_sample, n_samples):
    return jnp.minimum(i // blocks_per_sample, n_samples)


def _project(xa, mod, wa, wg, dims):
    b, t, tc, d = dims
    r = xa.shape[0]
    tm = 512
    tn = 1024
    npa = wa.shape[1]
    return pl.pallas_call(
        _proj_body,
        grid=(r // tm, npa // tn),
        in_specs=[
            pl.BlockSpec((tm, d), lambda i, j: (i, 0)),
            pl.BlockSpec((1, N_MOD, d), lambda i, j: (_group_of_block(i, t // tm, b), 0, 0)),
            pl.BlockSpec((d, tn), lambda i, j: (0, j)),
            pl.BlockSpec((d, LANES), lambda i, j: (0, 0)),
        ],
        out_specs=[
            pl.BlockSpec((tm, tn), lambda i, j: (i, j)),
            pl.BlockSpec((tm, LANES), lambda i, j: (i, 0)),
        ],
        out_shape=[
            jax.ShapeDtypeStruct((r, npa), BF16),
            jax.ShapeDtypeStruct((r, LANES), F32),
        ],
        scratch_shapes=[pltpu.VMEM((tm, d), BF16)],
        compiler_params=_cparams("parallel", "arbitrary"),
        name="in_proj",
    )(xa, mod, wa, wg)


def _qk_norm_rope(x_bf, gain, cosf, sinf, seg_ones, first_half):
    x = x_bf.astype(F32)
    x2 = x * x
    hi = x2.astype(BF16)
    lo = (x2 - hi.astype(F32)).astype(BF16)
    ss = (jnp.dot(hi, seg_ones, preferred_element_type=F32)
          + jnp.dot(lo, seg_ones, preferred_element_type=F32))
    xn = x * lax.rsqrt(ss * (1.0 / DIFF_HEAD_DIM) + NORM_EPS) * gain
    half = DIFF_HEAD_DIM // 2
    width = x.shape[1]
    nxt = pltpu.roll(xn, width - half, 1)
    prv = pltpu.roll(xn, half, 1)
    partner = jnp.where(first_half, nxt, prv)
    return xn * cosf + partner * sinf


def _short_conv_silu(prev_ref, cur_ref, next_ref, w, bias, at_start, at_end, out_scale):
    tp = cur_ref.shape[0]
    prev = prev_ref[...].astype(F32)[8:16]
    nxt = next_ref[...].astype(F32)[0:8]
    prev = jnp.where(at_start, 0.0, prev)
    nxt = jnp.where(at_end, 0.0, nxt)
    ext = jnp.concatenate([prev, cur_ref[...].astype(F32), nxt], axis=0)
    acc = bias
    for kk in range(CONV_K):
        off = 8 + kk - CONV_K // 2
        acc = acc + w[kk:kk + 1, :] * ext[off:off + tp]
    y = acc * jax.nn.sigmoid(acc)
    return y * out_scale


def _prep_body(q_ref, k_ref, mqp_ref, mq_ref, mqn_ref, mkp_ref, mk_ref, mkn_ref,
               cos_ref, sin_ref, gq_ref, gk_ref, so_ref, cw_ref, cb_ref,
               qo_ref, ko_ref, mqo_ref, mko_ref, *, b, t, tc, tp):
    cos4 = jnp.concatenate([cos_ref[...]] * 4, axis=1)
    sin4 = jnp.concatenate([sin_ref[...]] * 4, axis=1)
    lane = lax.broadcasted_iota(jnp.int32, (tp, BRANCH_W), 1)
    first_half = (lane % DIFF_HEAD_DIM) < (DIFF_HEAD_DIM // 2)
    seg_ones = so_ref[...]
    qo_ref[...] = _qk_norm_rope(q_ref[...], gq_ref[...], cos4, sin4, seg_ones, first_half).astype(BF16)
    ko_ref[...] = _qk_norm_rope(k_ref[...], gk_ref[...], cos4, sin4, seg_ones, first_half).astype(BF16)

    row0 = pl.program_id(0) * tp
    in_lat = row0 < b * t
    local = jnp.where(in_lat, row0 % t, (row0 - b * t) % tc)
    seq_len = jnp.where(in_lat, t, tc)
    at_start = local == 0
    at_end = local + tp == seq_len
    cw = cw_ref[...]
    cb = cb_ref[...]
    mqo_ref[...] = _short_conv_silu(mqp_ref, mq_ref, mqn_ref, cw[:, :BRANCH_W], cb[:, :BRANCH_W],
                                    at_start, at_end, 1.0).astype(BF16)
    mko_ref[...] = _short_conv_silu(mkp_ref, mk_ref, mkn_ref, cw[:, BRANCH_W:], cb[:, BRANCH_W:],
                                    at_start, at_end, HEAD_W ** -0.5).astype(BF16)


def _prepare(pa, cos_tab, sin_tab, gq, gk, seg_ones, conv_w, conv_b, dims):
    b, t, tc, d = dims
    r = pa.shape[0]
    tp = 256
    halo = 16
    hb = tp // halo
    last_halo = r // halo - 1

    def cur(seg):
        return pl.BlockSpec((tp, BRANCH_W), lambda i: (i, seg))

    def prev(seg):
        return pl.BlockSpec((halo, BRANCH_W), lambda i: (jnp.maximum(i * hb - 1, 0), seg))

    def nxt(seg):
        return pl.BlockSpec((halo, BRANCH_W), lambda i: (jnp.minimum((i + 1) * hb, last_halo), seg))

    full = lambda shape: pl.BlockSpec(shape, lambda i: (0, 0))
    out = jax.ShapeDtypeStruct((r, BRANCH_W), BF16)
    return pl.pallas_call(
        functools.partial(_prep_body, b=b, t=t, tc=tc, tp=tp),
        grid=(r // tp,),
        in_specs=[
            cur(SEG_Q), cur(SEG_K),
            prev(SEG_MQ), cur(SEG_MQ), nxt(SEG_MQ),
            prev(SEG_MK), cur(SEG_MK), nxt(SEG_MK),
            pl.BlockSpec((tp, LANES), lambda i: (i, 0)),
            pl.BlockSpec((tp, LANES), lambda i: (i, 0)),
            full((1, BRANCH_W)), full((1, BRANCH_W)),
            full((BRANCH_W, BRANCH_W)),
            full((8, 2 * BRANCH_W)), full((1, 2 * BRANCH_W)),
        ],
        out_specs=[pl.BlockSpec((tp, BRANCH_W), lambda i: (i, 0))] * 4,
        out_shape=[out, out, out, out],
        compiler_params=_cparams("parallel"),
        name="row_prep",
    )(pa, pa, pa, pa, pa, pa, pa, pa, cos_tab, sin_tab, gq, gk, seg_ones, conv_w, conv_b)


def _attn_body(lam_ref, og_ref, sh_ref, q_ref, k_ref, v_ref, kc_ref, vc_ref, o_ref,
               q0_ref, q1_ref, m_ref, l_ref, acc_ref, *, lam_init, has_ctx, nk, fixed_shift):
    kj = pl.program_id(3)

    def process(kb, vb):
        for mi, qr in enumerate((q0_ref, q1_ref)):
            s = jnp.dot(kb, qr[...], preferred_element_type=F32)
            if fixed_shift:
                p = jnp.exp2(s - sh_ref[0:1, 0:1])
                l_ref[mi] += jnp.sum(p, axis=0, keepdims=True)
                acc_ref[mi] += lax.dot_general(vb, p.astype(BF16), (((0,), (0,)), ((), ())),
                                               preferred_element_type=F32)
            else:
                m_old = m_ref[mi]
                m_new = jnp.maximum(m_old, jnp.max(s, axis=0, keepdims=True))
                alpha = jnp.exp2(m_old - m_new)
                p = jnp.exp2(s - m_new)
                l_ref[mi] = alpha * l_ref[mi] + jnp.sum(p, axis=0, keepdims=True)
                pv = lax.dot_general(vb, p.astype(BF16), (((0,), (0,)), ((), ())),
                                     preferred_element_type=F32)
                acc_ref[mi] = alpha * acc_ref[mi] + pv
                m_ref[mi] = m_new

    @pl.when(kj == 0)
    def _():
        qt = q_ref[...].astype(F32).T.astype(BF16)
        row = lax.broadcasted_iota(jnp.int32, qt.shape, 0)
        zero = jnp.zeros_like(qt)
        q0_ref[...] = jnp.where(row < DIFF_HEAD_DIM, qt, zero)
        q1_ref[...] = jnp.where(row >= DIFF_HEAD_DIM, qt, zero)
        m_ref[...] = jnp.full(m_ref.shape, NEG_BIG, F32)
        l_ref[...] = jnp.zeros(l_ref.shape, F32)
        acc_ref[...] = jnp.zeros(acc_ref.shape, F32)
        if has_ctx:
            process(kc_ref[...], vc_ref[...])

    process(k_ref[...], v_ref[...])

    @pl.when(kj == nk - 1)
    def _():
        lv = lam_ref[...]
        lam = (jnp.exp(jnp.sum(lv[0:1] * lv[1:2], keepdims=True))
               - jnp.exp(jnp.sum(lv[2:3] * lv[3:4], keepdims=True)) + lam_init)
        o = acc_ref[0] / l_ref[0] - lam * (acc_ref[1] / l_ref[1])
        ms = jnp.mean(o * o, axis=0, keepdims=True)
        o = o * lax.rsqrt(ms + NORM_EPS)
        o_ref[...] = (o.T * (og_ref[...] * (1.0 - lam_init))).astype(BF16)


MAX_FIXED_SHIFT = 48.0


def _attention(qh, kh, pa, lam_vecs, out_gain, score_bound, lam_init, dims, *, ctx_queries):
    shift = jnp.full((1, LANES), score_bound, F32)
    run = lambda fixed: _attention_call(qh, kh, pa, lam_vecs, out_gain, shift, lam_init, dims,
                                        ctx_queries=ctx_queries, fixed_shift=fixed)
    return lax.cond(score_bound <= MAX_FIXED_SHIFT, lambda: run(True), lambda: run(False))


def _attention_call(qh, kh, pa, lam_vecs, out_gain, shift, lam_init, dims, *, ctx_queries, fixed_shift):
    b, t, tc, d = dims
    v_col = SEG_V * (BRANCH_W // HEAD_W)
    ctx_blk0 = (b * t) // tc
    if ctx_queries:
        tq = tk = tc
        nq, nk = 1, 1
        q_row = lambda bb, qi: ctx_blk0 + bb
        k_row = lambda bb, kj: ctx_blk0 + bb
        n_rows = b * tc
        o_row = lambda bb, qi: bb
    else:
        tq = min(1024, t)
        tk = min(512, t)
        nq, nk = t // tq, t // tk
        q_row = lambda bb, qi: bb * nq + qi
        k_row = lambda bb, kj: bb * nk + kj
        n_rows = b * t
        o_row = q_row
    body = functools.partial(_attn_body, lam_init=lam_init, has_ctx=not ctx_queries, nk=nk,
                             fixed_shift=fixed_shift)
    return pl.pallas_call(
        body,
        grid=(b, N_HEADS, nq, nk),
        in_specs=[
            pl.BlockSpec((4, DIFF_HEAD_DIM), lambda bb, h, qi, kj: (0, 0)),
            pl.BlockSpec((1, HEAD_W), lambda bb, h, qi, kj: (0, 0)),
            pl.BlockSpec((1, LANES), lambda bb, h, qi, kj: (0, 0)),
            pl.BlockSpec((tq, HEAD_W), lambda bb, h, qi, kj: (q_row(bb, qi), h)),
            pl.BlockSpec((tk, HEAD_W), lambda bb, h, qi, kj: (k_row(bb, kj), h)),
            pl.BlockSpec((tk, HEAD_W), lambda bb, h, qi, kj: (k_row(bb, kj), v_col + h)),
            pl.BlockSpec((tc, HEAD_W), lambda bb, h, qi, kj: (ctx_blk0 + bb, h)),
            pl.BlockSpec((tc, HEAD_W), lambda bb, h, qi, kj: (ctx_blk0 + bb, v_col + h)),
        ],
        out_specs=pl.BlockSpec((tq, HEAD_W), lambda bb, h, qi, kj: (o_row(bb, qi), h)),
        out_shape=jax.ShapeDtypeStruct((n_rows, BRANCH_W), BF16),
        scratch_shapes=[
            pltpu.VMEM((HEAD_W, tq), BF16),
            pltpu.VMEM((HEAD_W, tq), BF16),
            pltpu.VMEM((2, 1, tq), F32),
            pltpu.VMEM((2, 1, tq), F32),
            pltpu.VMEM((2, HEAD_W, tq), F32),
        ],
        compiler_params=_cparams("parallel", "parallel", "parallel", "arbitrary"),
        name=("diff_attn_ctx" if ctx_queries else "diff_attn") + ("_fixed" if fixed_shift else ""),
    )(lam_vecs, out_gain, shift, qh, kh, pa, kh, pa)


def _s5_matrices(lam_re, lam_im, log_step, b_re, b_im, c_re, c_im, n_levels):
    ll, hg, pp, gg = S5_CHUNK, S5_GROUP, S5_STATE, S5_GROUPS
    dt = jnp.exp(log_step)[:, :, None]
    lr, li = lam_re * dt, lam_im * dt

    def a_pow(tau):
        tau = tau.astype(F32)[:, None, None, None]
        mag = jnp.exp(lr * tau)
        return mag * jnp.cos(li * tau), mag * jnp.sin(li * tau)

    ar1, ai1 = a_pow(jnp.ones((1,)))
    nr, ni = ar1[0] - 1.0, ai1[0]
    den = lam_re * lam_re + lam_im * lam_im
    f_re = (nr * lam_re + ni * lam_im) / den
    f_im = (ni * lam_re - nr * lam_im) / den
    bb_re = f_re[..., None] * b_re - f_im[..., None] * b_im
    bb_im = f_re[..., None] * b_im + f_im[..., None] * b_re

    ar, ai = a_pow(jnp.arange(ll + 1))
    ca_re = c_re[None] * ar[:, :, :, None, :] - c_im[None] * ai[:, :, :, None, :]
    ca_im = c_re[None] * ai[:, :, :, None, :] + c_im[None] * ar[:, :, :, None, :]
    kk = (jnp.einsum('tdgop,dgph->tdgoh', ca_re, bb_re, precision=HI)
          - jnp.einsum('tdgop,dgph->tdgoh', ca_im, bb_im, precision=HI))
    ns = gg // S5_SLAB
    eye = jnp.eye(S5_SLAB, dtype=F32)
    jj = jnp.arange(ll)
    tau = jj[None, :] - jj[:, None]

    def toeplitz(dd, lag):
        m = kk[jnp.clip(lag, 0, ll), dd]
        return jnp.where((lag >= 0)[:, :, None, None, None], m, 0.0)

    tt = jnp.stack([toeplitz(0, tau), toeplitz(1, -tau)])
    tt = tt.reshape(2, ll, ll, ns, S5_SLAB, hg, hg)
    width = ll * S5_SLAB * hg
    toep = jnp.einsum('djisgoh,gG->dsjghiGo', tt, eye).reshape(2, ns, width, width)

    pw = jnp.stack([ll - 1 - jj, jj])
    sel = lambda a: jnp.stack([a[pw[0], 0], a[pw[1], 1]])
    s_re, s_im = sel(ar), sel(ai)
    in_re = s_re[..., None] * bb_re[:, None] - s_im[..., None] * bb_im[:, None]
    in_im = s_re[..., None] * bb_im[:, None] + s_im[..., None] * bb_re[:, None]
    n_state = S5_SLAB * pp

    def in_slab(m):
        m = m.reshape(2, ll, ns, S5_SLAB, pp, hg)
        return jnp.einsum('djsgph,gG->dsjghGp', m, eye).reshape(2, ns, width, n_state)

    po = jnp.stack([jj + 1, ll - jj])
    selc = lambda a: jnp.stack([a[po[0], 0], a[po[1], 1]])

    def out_slab(m):
        m = m.reshape(2, ll, ns, S5_SLAB, hg, pp)
        return jnp.einsum('disgop,gG->dsgpiGo', m, eye).reshape(2, ns, n_state, width)

    lev = (ll * (2 ** jnp.arange(n_levels))).astype(F32)
    alr, ali = a_pow(lev)
    pad_lev = (-n_levels) % 8

    def lev_slab(a):
        a = a.transpose(1, 0, 2, 3).reshape(2, n_levels, ns, n_state).transpose(0, 2, 1, 3)
        return jnp.pad(a, ((0, 0), (0, 0), (0, pad_lev), (0, 0)))

    return dict(toep=toep.astype(BF16),
                in_re=in_slab(in_re).astype(BF16), in_im=in_slab(in_im).astype(BF16),
                out_re=out_slab(selc(ca_re)).astype(BF16), out_im=out_slab(-selc(ca_im)).astype(BF16),
                al_re=lev_slab(alr), al_im=lev_slab(ali))


def _s5_body(ul_ref, uc_ref, w_ref, inr_ref, ini_ref, outr_ref, outi_ref, alr_ref, ali_ref,
             yl_ref, yc_ref, sr_ref, si_ref, *, nlat, nctx, pad, n_levels, rev):
    ll = S5_CHUNK
    nc = nlat + nctx
    cat = lambda ref: jnp.concatenate([ref[j] for j in range(ll)], axis=1)
    ulat, uctx = cat(ul_ref), cat(uc_ref)
    u = jnp.concatenate([ulat, uctx] if rev else [uctx, ulat], axis=0)
    lat0, ctx0 = (0, nlat) if rev else (nctx, 0)
    lo = 0 if rev else pad
    zero0 = nc if rev else 0
    zeros = jnp.zeros((pad, sr_ref.shape[1]), F32)
    sr_ref[zero0:zero0 + pad, :] = zeros
    si_ref[zero0:zero0 + pad, :] = zeros
    sr_ref[lo:lo + nc, :] = jnp.dot(u, inr_ref[0], preferred_element_type=F32)
    si_ref[lo:lo + nc, :] = jnp.dot(u, ini_ref[0], preferred_element_type=F32)
    for lev in range(n_levels):
        dd = 1 << lev
        src = lo + dd if rev else lo - dd
        a_r = alr_ref[0, lev:lev + 1, :]
        a_i = ali_ref[0, lev:lev + 1, :]
        cr, ci = sr_ref[lo:lo + nc, :], si_ref[lo:lo + nc, :]
        pr, pi = sr_ref[src:src + nc, :], si_ref[src:src + nc, :]
        sr_ref[lo:lo + nc, :] = cr + a_r * pr - a_i * pi
        si_ref[lo:lo + nc, :] = ci + a_r * pi + a_i * pr
    ent = lo + 1 if rev else lo - 1
    er = sr_ref[ent:ent + nc, :].astype(BF16)
    ei = si_ref[ent:ent + nc, :].astype(BF16)
    for ib in range(ll // 2):
        cols = slice(ib * 2 * LANES, (ib + 1) * 2 * LANES)
        y = (jnp.dot(u, w_ref[0, :, cols], preferred_element_type=F32)
             + jnp.dot(er, outr_ref[0, :, cols], preferred_element_type=F32)
             + jnp.dot(ei, outi_ref[0, :, cols], preferred_element_type=F32)).astype(BF16)
        for k in range(2):
            yl_ref[2 * ib + k] = y[lat0:lat0 + nlat, k * LANES:(k + 1) * LANES]
            yc_ref[2 * ib + k] = y[ctx0:ctx0 + nctx, k * LANES:(k + 1) * LANES]


def _s5_levels(nc):
    return max(1, (nc - 1).bit_length())


def _s5_scan(u3, mats, dims, *, rev):
    b, t, tc, d = dims
    ll = S5_CHUNK
    nlat, nctx = t // ll, tc // ll
    n_levels = _s5_levels(nlat + nctx)
    pad = max(8, 1 << (n_levels - 1))
    ns = S5_GROUPS // S5_SLAB
    n_state = S5_SLAB * S5_STATE
    width = ll * LANES
    lev_rows = mats["al_re"].shape[1]
    once = pl.Buffered(1)
    per_slab = lambda shape: pl.BlockSpec((1,) + shape, lambda s, bb: (s, 0, 0), pipeline_mode=once)
    return pl.pallas_call(
        functools.partial(_s5_body, nlat=nlat, nctx=nctx, pad=pad, n_levels=n_levels, rev=rev),
        grid=(ns, b),
        in_specs=[
            pl.BlockSpec((ll, nlat, LANES), lambda s, bb: (0, bb, s), pipeline_mode=once),
            pl.BlockSpec((ll, nctx, LANES), lambda s, bb: (0, b * t // tc + bb, s)),
            per_slab((width, width)),
            per_slab((width, n_state)), per_slab((width, n_state)),
            per_slab((n_state, width)), per_slab((n_state, width)),
            per_slab((lev_rows, n_state)), per_slab((lev_rows, n_state)),
        ],
        out_specs=[
            pl.BlockSpec((ll, nlat, LANES), lambda s, bb: (0, bb, s)),
            pl.BlockSpec((ll, nctx, LANES), lambda s, bb: (0, bb, s)),
        ],
        out_shape=[
            jax.ShapeDtypeStruct((ll, b * nlat, BRANCH_W), BF16),
            jax.ShapeDtypeStruct((ll, b * nctx, BRANCH_W), BF16),
        ],
        scratch_shapes=[pltpu.VMEM((pad + nlat + nctx, n_state), F32)] * 2,
        compiler_params=pltpu.CompilerParams(dimension_semantics=("parallel", "parallel"),
                                             vmem_limit_bytes=S5_VMEM_LIMIT),
        name="s5_scan_bwd" if rev else "s5_scan_fwd",
    )(u3, u3, mats["toep"], mats["in_re"], mats["in_im"], mats["out_re"], mats["out_im"],
      mats["al_re"], mats["al_im"])


def _s5_mixer(pa, mats, dims):
    r = pa.shape[0]
    ll = S5_CHUNK
    u = pa[:, SEG_S5 * BRANCH_W:(SEG_S5 + 1) * BRANCH_W]
    u3 = u.reshape(r // ll, ll, BRANCH_W).transpose(1, 0, 2)
    ys = []
    for dd in range(2):
        yl, yc = _s5_scan(u3, {k: v[dd] for k, v in mats.items()}, dims, rev=bool(dd))
        ys.append(jnp.concatenate([yl, yc], axis=1).transpose(1, 0, 2).reshape(r, BRANCH_W))
    return ys


def _log_sigmoid(x):
    return -(jnp.maximum(-x, 0.0) + jnp.log1p(jnp.exp(-jnp.abs(x))))


def _mlstm_body(q_ref, k_ref, v_ref, g_ref, gb_ref, h_ref, c_ref, n_ref, m_ref):
    dd = pl.program_id(1)
    fwd = dd == 0

    @pl.when(pl.program_id(2) == 0)
    def _():
        c_ref[...] = jnp.zeros(c_ref.shape, F32)
        n_ref[...] = jnp.zeros(n_ref.shape, F32)
        m_ref[...] = jnp.zeros(m_ref.shape, F32)

    ll = MLSTM_CHUNK
    row = lax.broadcasted_iota(jnp.int32, (ll, ll), 0)
    col = lax.broadcasted_iota(jnp.int32, (ll, ll), 1)
    order = (row - col) * jnp.where(fwd, 1, -1)
    allowed = order >= 0
    tri = jnp.where(allowed, 1.0, 0.0)
    tri_t = jnp.where(order <= 0, 1.0, 0.0)

    g = g_ref[...] + gb_ref[...]
    lf = _log_sigmoid(g)
    g_t = g.T
    lf_t = lf.T
    outs = []
    for h in range(N_HEADS):
        def pick_col(a, base):
            return jnp.where(fwd, a[:, base + h:base + h + 1], a[:, 8 + base + h:8 + base + h + 1])

        def pick_row(a, base):
            return jnp.where(fwd, a[base + h:base + h + 1, :], a[8 + base + h:8 + base + h + 1, :])

        ig_col, lf_col = pick_col(g, 0), pick_col(lf, 4)
        ig_row, lf_row = pick_row(g_t, 0), pick_row(lf_t, 4)
        cum_col = jnp.sum(tri * lf_row, axis=1, keepdims=True)
        cum_row = jnp.sum(tri_t * lf_col, axis=0, keepdims=True)
        gtot = jnp.sum(lf_row, axis=1, keepdims=True)

        hs = slice(h * HEAD_W, (h + 1) * HEAD_W)
        q = q_ref[:, hs]
        k = k_ref[:, hs]
        v = v_ref[:, hs]
        c0 = c_ref[h]
        n0 = n_ref[h]
        m0 = m_ref[h][:, 0:1]

        dmat = jnp.where(allowed, cum_col - cum_row + ig_row, NEG_BIG)
        inter = cum_col + m0
        m_t = jnp.maximum(inter, jnp.max(dmat, axis=1, keepdims=True))
        pm = jnp.exp(dmat - m_t)
        ei = jnp.exp(inter - m_t)
        qk = lax.dot_general(q, k, (((1,), (1,)), ((), ())), preferred_element_type=F32)
        wq = pm * qk
        qc = lax.dot_general(q, c0.astype(BF16), (((1,), (1,)), ((), ())), preferred_element_type=F32)
        num = jnp.dot(wq.astype(BF16), v, preferred_element_type=F32) + ei * qc
        qf = q.astype(F32)
        den = jnp.sum(wq, axis=1, keepdims=True) + ei * jnp.sum(qf * n0, axis=1, keepdims=True)
        outs.append(num / jnp.maximum(jnp.abs(den), jnp.exp(-m_t)))

        w_col = gtot - cum_col + ig_col
        mw = jnp.max(w_col, axis=0, keepdims=True)
        ew = jnp.exp(w_col - mw)
        kf = k.astype(F32)
        vw = (ew * v.astype(F32)).astype(BF16)
        kv = lax.dot_general(vw, k, (((0,), (0,)), ((), ())), preferred_element_type=F32)
        ks = jnp.sum(ew * kf, axis=0, keepdims=True)
        m_new = jnp.maximum(gtot + m0, mw)
        a = jnp.exp(gtot + m0 - m_new)
        e = jnp.exp(mw - m_new)
        c_ref[h] = a * c0 + e * kv
        n_ref[h] = a * n0 + e * ks
        m_ref[h] = jnp.broadcast_to(m_new, (1, LANES))
    h_ref[0] = jnp.concatenate(outs, axis=1).astype(BF16)


def _mlstm(mq, mk, pa, pg, gate_bias, dims):
    b, t, tc, d = dims
    r = pa.shape[0]
    ll = MLSTM_CHUNK
    nctx, nlat = tc // ll, t // ll
    ctx0 = (b * t) // ll

    def rb(bb, dd, c):
        is_ctx = c < nctx
        cc = jnp.where(dd == 0, c, nctx - 1 - c)
        cl = jnp.where(dd == 0, c - nctx, nlat - 1 - (c - nctx))
        return jnp.where(is_ctx, ctx0 + bb * nctx + cc, bb * nlat + cl)

    return pl.pallas_call(
        _mlstm_body,
        grid=(b, 2, nctx + nlat),
        in_specs=[
            pl.BlockSpec((ll, BRANCH_W), lambda bb, dd, c: (rb(bb, dd, c), 0)),
            pl.BlockSpec((ll, BRANCH_W), lambda bb, dd, c: (rb(bb, dd, c), 0)),
            pl.BlockSpec((ll, BRANCH_W), lambda bb, dd, c: (rb(bb, dd, c), SEG_MV)),
            pl.BlockSpec((ll, LANES), lambda bb, dd, c: (rb(bb, dd, c), 0)),
            pl.BlockSpec((1, LANES), lambda bb, dd, c: (0, 0)),
        ],
        out_specs=pl.BlockSpec((1, ll, BRANCH_W), lambda bb, dd, c: (dd, rb(bb, dd, c), 0)),
        out_shape=jax.ShapeDtypeStruct((2, r, BRANCH_W), BF16),
        scratch_shapes=[
            pltpu.VMEM((N_HEADS, HEAD_W, HEAD_W), F32),
            pltpu.VMEM((N_HEADS, 1, HEAD_W), F32),
            pltpu.VMEM((N_HEADS, 1, LANES), F32),
        ],
        compiler_params=_cparams("parallel", "parallel", "arbitrary"),
        name="mlstm",
    )(mq, mk, pa, pg, gate_bias)


def _merge_body(ya_ref, y0_ref, y1_ref, u_ref, h0_ref, h1_ref, mo_ref, ga_ref, gb_ref, gc_ref, x_ref, m_ref,
                d_ref, wglu_ref, wb_ref, wo_ref, wr_ref, wrt_ref,
                x1_ref, h2_ref, aff_ref, afft_ref):
    mod = m_ref[0]
    ys = d_ref[...] * u_ref[...].astype(F32) + y0_ref[...].astype(F32) + y1_ref[...].astype(F32)
    gl = jax.nn.gelu(ys)
    yb = gl * jax.nn.sigmoid(jnp.dot(gl.astype(BF16), wglu_ref[...], preferred_element_type=F32))
    yc = (h0_ref[0].astype(F32) + h1_ref[0].astype(F32)) * jax.nn.sigmoid(mo_ref[...].astype(F32))
    gate = lambda ref: jax.nn.sigmoid(ref[...].astype(F32))
    mixed = (gate(ga_ref) * jnp.dot(ya_ref[...], wb_ref[0], preferred_element_type=F32)
             + gate(gb_ref) * jnp.dot(yb.astype(BF16), wb_ref[1], preferred_element_type=F32)
             + gate(gc_ref) * jnp.dot(yc.astype(BF16), wb_ref[2], preferred_element_type=F32))
    out = jnp.dot(mixed.astype(BF16), wo_ref[...], preferred_element_type=F32)
    x1 = x_ref[...] + mod[2:3, :] * out
    x1_ref[...] = x1
    h2 = _modulated_norm(x1, mod, 3, 4)
    h2_ref[...] = h2.astype(BF16)
    logits = jnp.dot(h2, wr_ref[...], precision=HI, preferred_element_type=F32)
    lane = lax.broadcasted_iota(jnp.int32, logits.shape, 1)
    logits = jnp.where(lane < N_EXPERTS, logits, NEG_BIG)
    ex = jnp.exp(logits - jnp.max(logits, axis=1, keepdims=True))
    aff_ref[...] = ex / jnp.sum(ex, axis=1, keepdims=True)
    lt = lax.dot_general(wrt_ref[...], h2, (((1,), (1,)), ((), ())), precision=HI,
                         preferred_element_type=F32)
    et = jnp.exp(lt - jnp.max(lt, axis=0, keepdims=True))
    afft_ref[...] = et / jnp.sum(et, axis=0, keepdims=True)


def _merge(ya, y0, y1, pa, hm, xa, mod, s5_d, w_glu, w_branch, w_out, wr_pad, wr_t, dims):
    b, t, tc, d = dims
    r = xa.shape[0]
    tm = 256
    gseg = SEG_GATE * BRANCH_W // d
    rowblk = lambda width, col=0: pl.BlockSpec((tm, width), lambda i: (i, col))
    full2 = lambda shape: pl.BlockSpec(shape, lambda i: (0, 0))
    return pl.pallas_call(
        _merge_body,
        grid=(r // tm,),
        in_specs=[
            rowblk(BRANCH_W), rowblk(BRANCH_W), rowblk(BRANCH_W),
            rowblk(BRANCH_W, SEG_S5),
            pl.BlockSpec((1, tm, BRANCH_W), lambda i: (0, i, 0)),
            pl.BlockSpec((1, tm, BRANCH_W), lambda i: (1, i, 0)),
            rowblk(BRANCH_W, SEG_MO),
            rowblk(d, gseg), rowblk(d, gseg + 1), rowblk(d, gseg + 2),
            rowblk(d),
            pl.BlockSpec((1, N_MOD, d), lambda i: (_group_of_block(i, t // tm, b), 0, 0)),
            full2((1, BRANCH_W)),
            full2((BRANCH_W, BRANCH_W)),
            pl.BlockSpec((3, BRANCH_W, d), lambda i: (0, 0, 0)),
            full2((d, d)),
            full2((d, LANES)),
            full2((N_EXPERTS, d)),
        ],
        out_specs=[
            rowblk(d), rowblk(d), rowblk(LANES),
            pl.BlockSpec((N_EXPERTS, tm), lambda i: (0, i)),
        ],
        out_shape=[
            jax.ShapeDtypeStruct((r, d), F32),
            jax.ShapeDtypeStruct((r, d), BF16),
            jax.ShapeDtypeStruct((r, LANES), F32),
            jax.ShapeDtypeStruct((N_EXPERTS, r), F32),
        ],
        compiler_params=_cparams("parallel"),
        name="merge",
    )(ya, y0, y1, pa, hm, hm, pa, pa, pa, pa, xa, mod, s5_d, w_glu, w_branch, w_out, wr_pad, wr_t)


def _route_body(a_ref, tri_ref, low_ref, pos_ref, offs_ref, *, cap):
    a = a_ref[0]
    e, nb, _ = a.shape
    bits = pltpu.bitcast(a, jnp.int32)

    def count(mask):
        c = jnp.sum(jnp.where(mask, 1.0, 0.0), axis=2, keepdims=True)
        return jnp.sum(c, axis=1, keepdims=True)

    def step(i, thr):
        cand = thr | jnp.left_shift(jnp.int32(1), 30 - i)
        return jnp.where(count(bits >= cand) >= cap, cand, thr)

    thr = lax.fori_loop(0, 31, step, jnp.zeros((e, 1, 1), jnp.int32))
    gt = bits > thr
    eq = bits == thr
    need = cap - count(gt)

    tri = tri_ref[...]
    low = low_ref[...]

    def exclusive_prefix(x):
        x2 = x.reshape(e * nb, LANES).astype(BF16)
        incl = jnp.dot(x2, tri, preferred_element_type=F32)
        before = jnp.sum(jnp.dot(low, x2, preferred_element_type=F32), axis=1, keepdims=True)
        return (incl - x2.astype(F32) + before).reshape(e, nb, LANES), before.reshape(e, nb, 1)

    eq_rank, _ = exclusive_prefix(jnp.where(eq, 1.0, 0.0))
    sel = gt | (eq & (eq_rank < need))
    pos, before = exclusive_prefix(jnp.where(sel, 1.0, 0.0))
    pos_ref[0] = jnp.where(sel, pos, -1.0)
    offs_ref[0] = before.astype(jnp.int32)


def _route(aff3, cap):
    ns, e, nb, _ = aff3.shape
    i = jnp.arange(LANES)
    tri = (i[:, None] <= i[None, :]).astype(BF16)
    r = jnp.arange(e * nb)
    low = ((r[:, None] // nb == r[None, :] // nb) & (r[None, :] < r[:, None])).astype(BF16)
    return pl.pallas_call(
        functools.partial(_route_body, cap=cap),
        grid=(ns,),
        in_specs=[
            pl.BlockSpec((1, e, nb, LANES), lambda s: (s, 0, 0, 0)),
            pl.BlockSpec((LANES, LANES), lambda s: (0, 0)),
            pl.BlockSpec((e * nb, e * nb), lambda s: (0, 0)),
        ],
        out_specs=[
            pl.BlockSpec((1, e, nb, LANES), lambda s: (s, 0, 0, 0)),
            pl.BlockSpec((1, e, nb, 1), lambda s: (s, 0, 0, 0)),
        ],
        out_shape=[
            jax.ShapeDtypeStruct((ns, e, nb, LANES), F32),
            jax.ShapeDtypeStruct((ns, e, nb, 1), jnp.int32),
        ],
        compiler_params=_cparams("parallel"),
        name="route",
    )(aff3, tri, low)


SLOT_ALIGN = 16


def _dispatch_body(st_ref, h_ref, p_ref, o_ref, *, tb, win, nj, sub):
    ns, e, j = pl.program_id(0), pl.program_id(1), pl.program_id(2)

    @pl.when(j == 0)
    def _():
        o_ref[...] = jnp.zeros(o_ref.shape, o_ref.dtype)

    for k in range(sub):
        jb = j * sub + k
        start = st_ref[(ns * N_EXPERTS + e) * (nj + 1) + jb]
        end = st_ref[(ns * N_EXPERTS + e) * (nj + 1) + jb + 1]

        @pl.when(end > start)
        def _():
            base = pl.multiple_of((start // SLOT_ALIGN) * SLOT_ALIGN, SLOT_ALIGN)
            slot = (base + lax.broadcasted_iota(jnp.int32, (win, tb), 0)).astype(F32)
            onehot = jnp.where(slot == p_ref[0, :, k * tb:(k + 1) * tb], 1.0, 0.0).astype(BF16)
            rows = jnp.dot(onehot, h_ref[k * tb:(k + 1) * tb, :], preferred_element_type=F32)
            cur = o_ref[0, 0, pl.ds(base, win), :]
            o_ref[0, 0, pl.ds(base, win), :] = cur + rows.astype(o_ref.dtype)


def _dispatch(starts, h2, pos, row0, ns, n, cap, tb):
    d = h2.shape[1]
    nj = n // tb
    sub = min(4, nj)
    njs = nj // sub
    win = tb + SLOT_ALIGN
    capp = cap + win
    blk0 = row0 // (sub * tb)
    pos_rows = pos.reshape(ns * N_EXPERTS * njs, 1, sub * tb)
    grid_spec = pltpu.PrefetchScalarGridSpec(
        num_scalar_prefetch=1,
        grid=(ns, N_EXPERTS, njs),
        in_specs=[
            pl.BlockSpec((sub * tb, d), lambda s, e, j, st: (blk0 + s * njs + j, 0)),
            pl.BlockSpec((1, 1, sub * tb), lambda s, e, j, st: ((s * N_EXPERTS + e) * njs + j, 0, 0)),
        ],
        out_specs=pl.BlockSpec((1, 1, capp, d), lambda s, e, j, st: (s, e, 0, 0)),
    )
    return pl.pallas_call(
        functools.partial(_dispatch_body, tb=tb, win=win, nj=nj, sub=sub),
        grid_spec=grid_spec,
        out_shape=jax.ShapeDtypeStruct((ns, N_EXPERTS, capp, d), BF16),
        compiler_params=_cparams("parallel", "parallel", "arbitrary"),
        name="dispatch",
    )(starts, h2, pos_rows)


def _expert_body(x_ref, wg_ref, wu_ref, wd_ref, y_ref, *, fc):
    x = x_ref[0, 0]
    f = wg_ref.shape[2]
    acc = jnp.zeros((x.shape[0], wd_ref.shape[2]), F32)
    for f0 in range(0, f, fc):
        g = jnp.dot(x, wg_ref[0, :, f0:f0 + fc], preferred_element_type=F32)
        u = jnp.dot(x, wu_ref[0, :, f0:f0 + fc], preferred_element_type=F32)
        hid = (g * jax.nn.sigmoid(g) * u).astype(BF16)
        acc = acc + jnp.dot(hid, wd_ref[0, f0:f0 + fc, :], preferred_element_type=F32)
    y_ref[0, 0] = acc.astype(BF16)


def _experts(xs, w_gate, w_up, w_down, cap):
    ns, e, _, d = xs.shape
    f = w_gate.shape[2]
    ts = min(512, cap)
    return pl.pallas_call(
        functools.partial(_expert_body, fc=min(512, f)),
        grid=(e, ns, cap // ts),
        in_specs=[
            pl.BlockSpec((1, 1, ts, d), lambda ee, s, i: (s, ee, i, 0)),
            pl.BlockSpec((1, d, f), lambda ee, s, i: (ee, 0, 0)),
            pl.BlockSpec((1, d, f), lambda ee, s, i: (ee, 0, 0)),
            pl.BlockSpec((1, f, d), lambda ee, s, i: (ee, 0, 0)),
        ],
        out_specs=pl.BlockSpec((1, 1, ts, d), lambda ee, s, i: (s, ee, i, 0)),
        out_shape=jax.ShapeDtypeStruct((ns, e, cap, d), BF16),
        compiler_params=_cparams("parallel", "parallel", "parallel"),
        name="experts",
    )(xs, w_gate, w_up, w_down)


def _combine_body(st_ref, *refs, sb, nj, nwin):
    y_refs = refs[:nwin]
    p_ref, aff_ref, x_ref, m_ref, o_ref = refs[nwin:]
    ns, j, e = pl.program_id(0), pl.program_id(1), pl.program_id(2)

    @pl.when(e == 0)
    def _():
        o_ref[...] = jnp.zeros(o_ref.shape, F32)

    tb = o_ref.shape[0]
    start = st_ref[(ns * N_EXPERTS + e) * (nj + 1) + j]
    end = st_ref[(ns * N_EXPERTS + e) * (nj + 1) + j + 1]
    a = start // sb
    lane = lax.broadcasted_iota(jnp.int32, (tb, LANES), 1)
    mine = lane == e
    pos = jnp.sum(jnp.where(mine, p_ref[...], 0.0), axis=1, keepdims=True)
    aff = jnp.sum(jnp.where(mine, aff_ref[...], 0.0), axis=1, keepdims=True)
    for w, y_ref in enumerate(y_refs):
        first = (a + w) * sb

        @pl.when((end > start) & (end > first))
        def _():
            slot = (first + lax.broadcasted_iota(jnp.int32, (tb, sb), 1)).astype(F32)
            onehot = jnp.where(pos == slot, 1.0, 0.0).astype(BF16)
            o_ref[...] += aff * jnp.dot(onehot, y_ref[0, 0], preferred_element_type=F32)

    @pl.when(e == N_EXPERTS - 1)
    def _():
        o_ref[...] = x_ref[...] + m_ref[0, 5:6, :] * o_ref[...]


def _combine(starts, ys, pos_t, aff, x1, mod, row0, ns, n, cap, tb, mod_group):
    d = x1.shape[1]
    nj = n // tb
    span = min(tb, cap)
    sb = min(LANES, cap)
    nwin = span // sb + 1
    nsb = cap // sb
    blk0 = row0 // tb

    def window(w):
        def index(s, j, e, st):
            first = st[(s * N_EXPERTS + e) * (nj + 1) + j] // sb
            return (s, e, jnp.minimum(first + w, nsb - 1), 0)
        return pl.BlockSpec((1, 1, sb, d), index)

    row = lambda s, j, e, st: (blk0 + s * nj + j, 0)
    grid_spec = pltpu.PrefetchScalarGridSpec(
        num_scalar_prefetch=1,
        grid=(ns, nj, N_EXPERTS),
        in_specs=[window(w) for w in range(nwin)] + [
            pl.BlockSpec((tb, LANES), row),
            pl.BlockSpec((tb, LANES), row),
            pl.BlockSpec((tb, d), row),
            pl.BlockSpec((1, N_MOD, d), lambda s, j, e, st: (mod_group(s), 0, 0)),
        ],
        out_specs=pl.BlockSpec((tb, d), lambda s, j, e, st: (s * nj + j, 0)),
    )
    return pl.pallas_call(
        functools.partial(_combine_body, sb=sb, nj=nj, nwin=nwin),
        grid_spec=grid_spec,
        out_shape=jax.ShapeDtypeStruct((ns * n, d), F32),
        compiler_params=_cparams("parallel", "parallel", "arbitrary"),
        name="combine",
    )(starts, *([ys] * nwin), pos_t, aff, x1, mod)


def _expert_choice(h2, aff, aff_t, x1, mod, w_gate, w_up, w_down, row0, ns, n, mod_group):
    r, d = h2.shape
    e = N_EXPERTS
    cap = CAPACITY_FACTOR * n // e
    tb = min(256, n)
    n_pad = max(n, 8 * LANES)
    a = aff_t[:, row0:row0 + ns * n].reshape(e, ns, n).transpose(1, 0, 2)
    if n_pad > n:
        a = jnp.concatenate([a, jnp.full((ns, e, n_pad - n), -1.0, F32)], axis=2)
    pos, offs = _route(a.reshape(ns, e, n_pad // LANES, LANES), cap)
    pos = pos.reshape(ns, e, n_pad)[:, :, :n]
    starts = offs.reshape(ns, e, n_pad // LANES)[:, :, :n // LANES:tb // LANES]
    starts = jnp.concatenate([starts, jnp.full((ns, e, 1), cap, jnp.int32)], axis=2).reshape(-1)
    xs = _dispatch(starts, h2, pos, row0, ns, n, cap, tb)
    ys = _experts(xs, w_gate, w_up, w_down, cap)
    pos_t = jnp.pad(pos.transpose(0, 2, 1).reshape(ns * n, e), ((0, 0), (0, LANES - e)), constant_values=-1.0)
    pos_t = jnp.pad(pos_t, ((row0, r - row0 - ns * n), (0, 0)))
    return _combine(starts, ys, pos_t, aff, x1, mod, row0, ns, n, cap, tb, mod_group)


def _rope_tables(b, t, tc):
    n_freq = DIFF_HEAD_DIM // 4
    inv_freq = ROPE_BASE ** (-jnp.arange(n_freq, dtype=F32) / n_freq)
    pos = jnp.arange(t)
    row = (pos // GRID_W).astype(F32)
    col = (pos % GRID_W).astype(F32)
    ang = jnp.concatenate([row[:, None] * inv_freq, col[:, None] * inv_freq], axis=-1)
    cos, sin = jnp.cos(ang), jnp.sin(ang)
    cos_seg = jnp.concatenate([cos, cos], axis=-1)
    sin_seg = jnp.concatenate([-sin, sin], axis=-1)
    cos_t = jnp.tile(cos_seg, (b, LANES // DIFF_HEAD_DIM))
    sin_t = jnp.tile(sin_seg, (b, LANES // DIFF_HEAD_DIM))
    cos_t = jnp.concatenate([cos_t, jnp.ones((b * tc, LANES), F32)])
    sin_t = jnp.concatenate([sin_t, jnp.zeros((b * tc, LANES), F32)])
    return cos_t, sin_t


def kernel(x, c, ctx, c_ctx, w_mod, b_mod, w_in, attn_q_gain, attn_k_gain, attn_lambda, attn_out_gain,
           s5_lam_re, s5_lam_im, s5_log_step, s5_b_re, s5_b_im, s5_c_re, s5_c_im, s5_d, s5_w_glu,
           mlstm_conv_w, mlstm_conv_b, mlstm_i_bias, mlstm_f_bias,
           w_branch, w_out, w_router, w_exp_gate, w_exp_up, w_exp_down):
    b, t, d = x.shape
    tc = ctx.shape[1]
    n_layers = w_mod.shape[0]
    dims = (b, t, tc, d)
    assert b + 1 <= 8 and t % 512 == 0 and tc % 256 == 0 and (b * tc) % 512 == 0

    xa = jnp.concatenate([x.reshape(b * t, d), ctx.reshape(b * tc, d)])
    cvec = jnp.zeros((8, d), F32).at[:b].set(c).at[b].set(c_ctx)
    mod_all = _mod_vectors(cvec, w_mod, b_mod).reshape(n_layers, 8, N_MOD, d)

    n_main = 8 * BRANCH_W
    wa = jnp.concatenate([w_in[:, :, :n_main], w_in[:, :, n_main + N_GATES:]], axis=2).astype(BF16)
    wg = jnp.pad(w_in[:, :, n_main:n_main + N_GATES], ((0, 0), (0, 0), (0, LANES - N_GATES))).astype(BF16)
    cos_t, sin_t = _rope_tables(b, t, tc)
    seg = jnp.arange(BRANCH_W) // DIFF_HEAD_DIM
    seg_ones = (seg[:, None] == seg[None, :]).astype(BF16)
    n_seg = BRANCH_W // DIFF_HEAD_DIM
    gq = jnp.tile(attn_q_gain, (1, n_seg))[:, None, :] * (DIFF_HEAD_DIM ** -0.5 * math.log2(math.e))
    gk = jnp.tile(attn_k_gain, (1, n_seg))[:, None, :]
    conv_w = jnp.pad(mlstm_conv_w, ((0, 0), (0, 8 - CONV_K), (0, 0)))
    gate_bias = jnp.stack([mlstm_i_bias, mlstm_f_bias], axis=2).reshape(n_layers, 1, N_GATES)
    gate_bias = jnp.pad(gate_bias, ((0, 0), (0, 0), (0, LANES - N_GATES)))
    wr_pad = jnp.pad(w_router, ((0, 0), (0, 0), (0, LANES - N_EXPERTS)))
    wr_t = jnp.swapaxes(w_router, 1, 2)
    n_chunks = (t + tc) // S5_CHUNK
    n_levels = _s5_levels(n_chunks)

    for l in range(n_layers):
        with_ctx = l != n_layers - 1
        lam_init = 0.8 - 0.6 * math.exp(-0.3 * l)
        mod = mod_all[l]
        pa, pg = _project(xa, mod, wa[l], wg[l], dims)
        qh, kh, mq, mk = _prepare(pa, cos_t, sin_t, gq[l], gk[l], seg_ones, conv_w[l],
                                  mlstm_conv_b[l][None, :], dims)
        og = attn_out_gain[l][None, :]
        bound = 1.01 * DIFF_HEAD_DIM * jnp.max(jnp.abs(gq[l])) * jnp.max(jnp.abs(gk[l]))
        ya_l = _attention(qh, kh, pa, attn_lambda[l], og, bound, lam_init, dims, ctx_queries=False)
        if with_ctx:
            ya_c = _attention(qh, kh, pa, attn_lambda[l], og, bound, lam_init, dims, ctx_queries=True)
        else:
            ya_c = jnp.zeros((b * tc, BRANCH_W), BF16)
        ya = jnp.concatenate([ya_l, ya_c])
        mats = _s5_matrices(s5_lam_re[l], s5_lam_im[l], s5_log_step[l], s5_b_re[l], s5_b_im[l],
                            s5_c_re[l], s5_c_im[l], n_levels)
        y0, y1 = _s5_mixer(pa, mats, dims)
        hm = _mlstm(mq, mk, pa, pg, gate_bias[l], dims)
        x1, h2, aff, aff_t = _merge(ya, y0, y1, pa, hm, xa, mod, s5_d[l][None, :],
                                    s5_w_glu[l].astype(BF16), w_branch[l].astype(BF16),
                                    w_out[l].astype(BF16), wr_pad[l], wr_t[l], dims)
        wge, wue, wde = (w_exp_gate[l].astype(BF16), w_exp_up[l].astype(BF16), w_exp_down[l].astype(BF16))
        x2_l = _expert_choice(h2, aff, aff_t, x1, mod, wge, wue, wde, 0, b, t, lambda s: s)
        if with_ctx:
            x2_c = _expert_choice(h2, aff, aff_t, x1, mod, wge, wue, wde, b * t, b, tc, lambda s: b)
        else:
            x2_c = x1[b * t:]
        xa = jnp.concatenate([x2_l, x2_c])
    return xa[:b * t].reshape(b, t, d)
```

```python
import functools
import math

import jax
import jax.numpy as jnp
from jax import lax
from jax.experimental import pallas as pl
from jax.experimental.pallas import tpu as pltpu

F32 = jnp.float32
BF16 = jnp.bfloat16
HI = lax.Precision.HIGHEST

N_MOD = 6
NORM_EPS = 1e-6
GRID_W = 64
ROPE_BASE = 10000.0
N_HEADS = 4
DIFF_HEAD_DIM = 64
HEAD_W = 128
BRANCH_W = 512
S5_GROUPS = 32
S5_GROUP = 16
S5_STATE = 64
S5_CHUNK = 16
S5_SLAB = 8
S5_VMEM_LIMIT = 58 * 1024 * 1024
MLSTM_CHUNK = 128
CONV_K = 5
N_GATES = 16
N_EXPERTS = 16
CAPACITY_FACTOR = 2
LANES = 128
VMEM_LIMIT = 52 * 1024 * 1024
NEG_BIG = -1e30

SEG_Q, SEG_K, SEG_V, SEG_S5, SEG_MQ, SEG_MK, SEG_MV, SEG_MO, SEG_GATE = range(9)
PA_WIDTH = 8 * BRANCH_W + 3 * 1024


def _cparams(*sem):
    return pltpu.CompilerParams(dimension_semantics=sem, vmem_limit_bytes=VMEM_LIMIT)


def _mod_body(c_ref, w_ref, b_ref, o_ref):
    cv = c_ref[...]
    s = cv * jax.nn.sigmoid(cv)
    o_ref[0] = jnp.dot(s, w_ref[0], precision=HI, preferred_element_type=F32) + b_ref[0]


def _mod_vectors(cvec, w_mod, b_mod):
    n_layers, d, n = w_mod.shape
    tn = n // 4
    return pl.pallas_call(
        _mod_body,
        grid=(n_layers, n // tn),
        in_specs=[
            pl.BlockSpec((8, d), lambda l, j: (0, 0)),
            pl.BlockSpec((1, d, tn), lambda l, j: (l, 0, j)),
            pl.BlockSpec((1, 1, tn), lambda l, j: (l, 0, j)),
        ],
        out_specs=pl.BlockSpec((1, 8, tn), lambda l, j: (l, 0, j)),
        out_shape=jax.ShapeDtypeStruct((n_layers, 8, n), F32),
        compiler_params=_cparams("parallel", "parallel"),
        name="mod_vectors",
    )(cvec, w_mod, b_mod.reshape(n_layers, 1, n))


def _modulated_norm(x, mod, i_shift, i_scale):
    ms = jnp.mean(x * x, axis=-1, keepdims=True)
    xn = x * lax.rsqrt(ms + NORM_EPS)
    return xn * (1.0 + mod[i_scale:i_scale + 1, :]) + mod[i_shift:i_shift + 1, :]


def _proj_body(x_ref, m_ref, w_ref, wg_ref, pa_ref, pg_ref, hn_ref):
    @pl.when(pl.program_id(1) == 0)
    def _():
        hb = _modulated_norm(x_ref[...], m_ref[0], 0, 1).astype(BF16)
        hn_ref[...] = hb
        pg_ref[...] = jnp.dot(hb, wg_ref[...], preferred_element_type=F32)

    pa_ref[...] = jnp.dot(hn_ref[...], w_ref[...], preferred_element_type=F32).astype(BF16)


def _group_of_block(i, blocks_per_sample, n_samples):
    return jnp.minimum(i // blocks_per_sample, n_samples)


def _project(xa, mod, wa, wg, dims):
    b, t, tc, d = dims
    r = xa.shape[0]
    tm = 512
    tn = 1024
    npa = wa.shape[1]
    return pl.pallas_call(
        _proj_body,
        grid=(r // tm, npa // tn),
        in_specs=[
            pl.BlockSpec((tm, d), lambda i, j: (i, 0)),
            pl.BlockSpec((1, N_MOD, d), lambda i, j: (_group_of_block(i, t // tm, b), 0, 0)),
            pl.BlockSpec((d, tn), lambda i, j: (0, j)),
            pl.BlockSpec((d, LANES), lambda i, j: (0, 0)),
        ],
        out_specs=[
            pl.BlockSpec((tm, tn), lambda i, j: (i, j)),
            pl.BlockSpec((tm, LANES), lambda i, j: (i, 0)),
        ],
        out_shape=[
            jax.ShapeDtypeStruct((r, npa), BF16),
            jax.ShapeDtypeStruct((r, LANES), F32),
        ],
        scratch_shapes=[pltpu.VMEM((tm, d), BF16)],
        compiler_params=_cparams("parallel", "arbitrary"),
        name="in_proj",
    )(xa, mod, wa, wg)


def _qk_norm_rope(x_bf, gain, cosf, sinf, seg_ones, first_half):
    x = x_bf.astype(F32)
    x2 = x * x
    hi = x2.astype(BF16)
    lo = (x2 - hi.astype(F32)).astype(BF16)
    ss = (jnp.dot(hi, seg_ones, preferred_element_type=F32)
          + jnp.dot(lo, seg_ones, preferred_element_type=F32))
    xn = x * lax.rsqrt(ss * (1.0 / DIFF_HEAD_DIM) + NORM_EPS) * gain
    half = DIFF_HEAD_DIM // 2
    width = x.shape[1]
    nxt = pltpu.roll(xn, width - half, 1)
    prv = pltpu.roll(xn, half, 1)
    partner = jnp.where(first_half, nxt, prv)
    return xn * cosf + partner * sinf


def _short_conv_silu(prev_ref, cur_ref, next_ref, w, bias, at_start, at_end, out_scale):
    tp = cur_ref.shape[0]
    prev = prev_ref[...].astype(F32)[8:16]
    nxt = next_ref[...].astype(F32)[0:8]
    prev = jnp.where(at_start, 0.0, prev)
    nxt = jnp.where(at_end, 0.0, nxt)
    ext = jnp.concatenate([prev, cur_ref[...].astype(F32), nxt], axis=0)
    acc = bias
    for kk in range(CONV_K):
        off = 8 + kk - CONV_K // 2
        acc = acc + w[kk:kk + 1, :] * ext[off:off + tp]
    y = acc * jax.nn.sigmoid(acc)
    return y * out_scale


def _prep_body(q_ref, k_ref, mqp_ref, mq_ref, mqn_ref, mkp_ref, mk_ref, mkn_ref,
               cos_ref, sin_ref, gq_ref, gk_ref, so_ref, cw_ref, cb_ref,
               qo_ref, ko_ref, mqo_ref, mko_ref, *, b, t, tc, tp):
    cos4 = jnp.concatenate([cos_ref[...]] * 4, axis=1)
    sin4 = jnp.concatenate([sin_ref[...]] * 4, axis=1)
    lane = lax.broadcasted_iota(jnp.int32, (tp, BRANCH_W), 1)
    first_half = (lane % DIFF_HEAD_DIM) < (DIFF_HEAD_DIM // 2)
    seg_ones = so_ref[...]
    qo_ref[...] = _qk_norm_rope(q_ref[...], gq_ref[...], cos4, sin4, seg_ones, first_half).astype(BF16)
    ko_ref[...] = _qk_norm_rope(k_ref[...], gk_ref[...], cos4, sin4, seg_ones, first_half).astype(BF16)

    row0 = pl.program_id(0) * tp
    in_lat = row0 < b * t
    local = jnp.where(in_lat, row0 % t, (row0 - b * t) % tc)
    seq_len = jnp.where(in_lat, t, tc)
    at_start = local == 0
    at_end = local + tp == seq_len
    cw = cw_ref[...]
    cb = cb_ref[...]
    mqo_ref[...] = _short_conv_silu(mqp_ref, mq_ref, mqn_ref, cw[:, :BRANCH_W], cb[:, :BRANCH_W],
                                    at_start, at_end, 1.0).astype(BF16)
    mko_ref[...] = _short_conv_silu(mkp_ref, mk_ref, mkn_ref, cw[:, BRANCH_W:], cb[:, BRANCH_W:],
                                    at_start, at_end, HEAD_W ** -0.5).astype(BF16)


def _prepare(pa, cos_tab, sin_tab, gq, gk, seg_ones, conv_w, conv_b, dims):
    b, t, tc, d = dims
    r = pa.shape[0]
    tp = 256
    halo = 16
    hb = tp // halo
    last_halo = r // halo - 1

    def cur(seg):
        return pl.BlockSpec((tp, BRANCH_W), lambda i: (i, seg))

    def prev(seg):
        return pl.BlockSpec((halo, BRANCH_W), lambda i: (jnp.maximum(i * hb - 1, 0), seg))

    def nxt(seg):
        return pl.BlockSpec((halo, BRANCH_W), lambda i: (jnp.minimum((i + 1) * hb, last_halo), seg))

    full = lambda shape: pl.BlockSpec(shape, lambda i: (0, 0))
    out = jax.ShapeDtypeStruct((r, BRANCH_W), BF16)
    return pl.pallas_call(
        functools.partial(_prep_body, b=b, t=t, tc=tc, tp=tp),
        grid=(r // tp,),
        in_specs=[
            cur(SEG_Q), cur(SEG_K),
            prev(SEG_MQ), cur(SEG_MQ), nxt(SEG_MQ),
            prev(SEG_MK), cur(SEG_MK), nxt(SEG_MK),
            pl.BlockSpec((tp, LANES), lambda i: (i, 0)),
            pl.BlockSpec((tp, LANES), lambda i: (i, 0)),
            full((1, BRANCH_W)), full((1, BRANCH_W)),
            full((BRANCH_W, BRANCH_W)),
            full((8, 2 * BRANCH_W)), full((1, 2 * BRANCH_W)),
        ],
        out_specs=[pl.BlockSpec((tp, BRANCH_W), lambda i: (i, 0))] * 4,
        out_shape=[out, out, out, out],
        compiler_params=_cparams("parallel"),
        name="row_prep",
    )(pa, pa, pa, pa, pa, pa, pa, pa, cos_tab, sin_tab, gq, gk, seg_ones, conv_w, conv_b)


def _attn_body(lam_ref, og_ref, sh_ref, q_ref, k_ref, v_ref, kc_ref, vc_ref, o_ref,
               q0_ref, q1_ref, m_ref, l_ref, acc_ref, *, lam_init, has_ctx, nk, fixed_shift):
    kj = pl.program_id(3)

    def process(kb, vb):
        for mi, qr in enumerate((q0_ref, q1_ref)):
            s = jnp.dot(kb, qr[...], preferred_element_type=F32)
            if fixed_shift:
                p = jnp.exp2(s - sh_ref[0:1, 0:1])
                l_ref[mi] += jnp.sum(p, axis=0, keepdims=True)
                acc_ref[mi] += lax.dot_general(vb, p.astype(BF16), (((0,), (0,)), ((), ())),
                                               preferred_element_type=F32)
            else:
                m_old = m_ref[mi]
                m_new = jnp.maximum(m_old, jnp.max(s, axis=0, keepdims=True))
                alpha = jnp.exp2(m_old - m_new)
                p = jnp.exp2(s - m_new)
                l_ref[mi] = alpha * l_ref[mi] + jnp.sum(p, axis=0, keepdims=True)
                pv = lax.dot_general(vb, p.astype(BF16), (((0,), (0,)), ((), ())),
                                     preferred_element_type=F32)
                acc_ref[mi] = alpha * acc_ref[mi] + pv
                m_ref[mi] = m_new

    @pl.when(kj == 0)
    def _():
        qt = q_ref[...].astype(F32).T.astype(BF16)
        row = lax.broadcasted_iota(jnp.int32, qt.shape, 0)
        zero = jnp.zeros_like(qt)
        q0_ref[...] = jnp.where(row < DIFF_HEAD_DIM, qt, zero)
        q1_ref[...] = jnp.where(row >= DIFF_HEAD_DIM, qt, zero)
        m_ref[...] = jnp.full(m_ref.shape, NEG_BIG, F32)
        l_ref[...] = jnp.zeros(l_ref.shape, F32)
        acc_ref[...] = jnp.zeros(acc_ref.shape, F32)
        if has_ctx:
            process(kc_ref[...], vc_ref[...])

    process(k_ref[...], v_ref[...])

    @pl.when(kj == nk - 1)
    def _():
        lv = lam_ref[...]
        lam = (jnp.exp(jnp.sum(lv[0:1] * lv[1:2], keepdims=True))
               - jnp.exp(jnp.sum(lv[2:3] * lv[3:4], keepdims=True)) + lam_init)
        o = acc_ref[0] / l_ref[0] - lam * (acc_ref[1] / l_ref[1])
        ms = jnp.mean(o * o, axis=0, keepdims=True)
        o = o * lax.rsqrt(ms + NORM_EPS)
        o_ref[...] = (o.T * (og_ref[...] * (1.0 - lam_init))).astype(BF16)


MAX_FIXED_SHIFT = 48.0


def _attention(qh, kh, pa, lam_vecs, out_gain, score_bound, lam_init, dims, *, ctx_queries):
    shift = jnp.full((1, LANES), score_bound, F32)
    run = lambda fixed: _attention_call(qh, kh, pa, lam_vecs, out_gain, shift, lam_init, dims,
                                        ctx_queries=ctx_queries, fixed_shift=fixed)
    return lax.cond(score_bound <= MAX_FIXED_SHIFT, lambda: run(True), lambda: run(False))


def _attention_call(qh, kh, pa, lam_vecs, out_gain, shift, lam_init, dims, *, ctx_queries, fixed_shift):
    b, t, tc, d = dims
    v_col = SEG_V * (BRANCH_W // HEAD_W)
    ctx_blk0 = (b * t) // tc
    if ctx_queries:
        tq = tk = tc
        nq, nk = 1, 1
        q_row = lambda bb, qi: ctx_blk0 + bb
        k_row = lambda bb, kj: ctx_blk0 + bb
        n_rows = b * tc
        o_row = lambda bb, qi: bb
    else:
        tq = min(1024, t)
        tk = min(1024, t)
        nq, nk = t // tq, t // tk
        q_row = lambda bb, qi: bb * nq + qi
        k_row = lambda bb, kj: bb * nk + kj
        n_rows = b * t
        o_row = q_row
    body = functools.partial(_attn_body, lam_init=lam_init, has_ctx=not ctx_queries, nk=nk,
                             fixed_shift=fixed_shift)
    return pl.pallas_call(
        body,
        grid=(b, N_HEADS, nq, nk),
        in_specs=[
            pl.BlockSpec((4, DIFF_HEAD_DIM), lambda bb, h, qi, kj: (0, 0)),
            pl.BlockSpec((1, HEAD_W), lambda bb, h, qi, kj: (0, 0)),
            pl.BlockSpec((1, LANES), lambda bb, h, qi, kj: (0, 0)),
            pl.BlockSpec((tq, HEAD_W), lambda bb, h, qi, kj: (q_row(bb, qi), h)),
            pl.BlockSpec((tk, HEAD_W), lambda bb, h, qi, kj: (k_row(bb, kj), h)),
            pl.BlockSpec((tk, HEAD_W), lambda bb, h, qi, kj: (k_row(bb, kj), v_col + h)),
            pl.BlockSpec((tc, HEAD_W), lambda bb, h, qi, kj: (ctx_blk0 + bb, h)),
            pl.BlockSpec((tc, HEAD_W), lambda bb, h, qi, kj: (ctx_blk0 + bb, v_col + h)),
        ],
        out_specs=pl.BlockSpec((tq, HEAD_W), lambda bb, h, qi, kj: (o_row(bb, qi), h)),
        out_shape=jax.ShapeDtypeStruct((n_rows, BRANCH_W), BF16),
        scratch_shapes=[
            pltpu.VMEM((HEAD_W, tq), BF16),
            pltpu.VMEM((HEAD_W, tq), BF16),
            pltpu.VMEM((2, 1, tq), F32),
            pltpu.VMEM((2, 1, tq), F32),
            pltpu.VMEM((2, HEAD_W, tq), F32),
        ],
        compiler_params=_cparams("parallel", "parallel", "parallel", "arbitrary"),
        name=("diff_attn_ctx" if ctx_queries else "diff_attn") + ("_fixed" if fixed_shift else ""),
    )(lam_vecs, out_gain, shift, qh, kh, pa, kh, pa)


def _s5_matrices(lam_re, lam_im, log_step, b_re, b_im, c_re, c_im, n_levels):
    ll, hg, pp, gg = S5_CHUNK, S5_GROUP, S5_STATE, S5_GROUPS
    dt = jnp.exp(log_step)[:, :, None]
    lr, li = lam_re * dt, lam_im * dt

    def a_pow(tau):
        tau = tau.astype(F32)[:, None, None, None]
        mag = jnp.exp(lr * tau)
        return mag * jnp.cos(li * tau), mag * jnp.sin(li * tau)

    ar1, ai1 = a_pow(jnp.ones((1,)))
    nr, ni = ar1[0] - 1.0, ai1[0]
    den = lam_re * lam_re + lam_im * lam_im
    f_re = (nr * lam_re + ni * lam_im) / den
    f_im = (ni * lam_re - nr * lam_im) / den
    bb_re = f_re[..., None] * b_re - f_im[..., None] * b_im
    bb_im = f_re[..., None] * b_im + f_im[..., None] * b_re

    ar, ai = a_pow(jnp.arange(ll + 1))
    ca_re = c_re[None] * ar[:, :, :, None, :] - c_im[None] * ai[:, :, :, None, :]
    ca_im = c_re[None] * ai[:, :, :, None, :] + c_im[None] * ar[:, :, :, None, :]
    kk = (jnp.einsum('tdgop,dgph->tdgoh', ca_re, bb_re, precision=HI)
          - jnp.einsum('tdgop,dgph->tdgoh', ca_im, bb_im, precision=HI))
    ns, sl = gg // S5_SLAB, S5_SLAB
    n_state = sl * pp
    width = ll * LANES
    jj = jnp.arange(ll)
    grp = jnp.arange(sl)
    same_go = (grp[:, None] == jnp.arange(LANES)[None, :] // hg).astype(F32)
    same_gp = (grp[:, None] == jnp.arange(n_state)[None, :] // pp).astype(F32)

    kt = kk[:ll].transpose(1, 0, 2, 4, 3).reshape(2, ll, ns, sl, hg, hg)
    kt = kt.transpose(0, 2, 1, 4, 3, 5).reshape(2, ns, ll, hg, LANES)
    kbd = (kt[:, :, :, None] * same_go[None, None, None, :, None, :]).reshape(2, ns, ll, LANES, LANES)

    pw = jnp.stack([ll - 1 - jj, jj])
    sel = lambda a: jnp.stack([a[pw[0], 0], a[pw[1], 1]]).reshape(2, ll, ns, n_state).transpose(0, 2, 1, 3)
    s_re, s_im = sel(ar), sel(ai)
    lane_bb = lambda m: m.reshape(2, ns, sl, pp, hg).transpose(0, 1, 4, 2, 3).reshape(2, ns, hg, n_state)
    t_re, t_im = lane_bb(bb_re), lane_bb(bb_im)

    def in_slab(a, bmat, c, dmat, sign):
        v = a[:, :, :, None, :] * bmat[:, :, None, :, :] + sign * c[:, :, :, None, :] * dmat[:, :, None, :, :]
        v = v[:, :, :, None] * same_gp[None, None, None, :, None, :]
        return v.reshape(2, ns, width, n_state)

    po = jnp.stack([jj + 1, ll - jj])

    def out_slab_t(a):
        m = jnp.stack([a[po[0], 0], a[po[1], 1]])
        m = m.reshape(2, ll, ns, sl, hg, pp).transpose(0, 2, 1, 4, 3, 5).reshape(2, ns, ll, hg, n_state)
        v = m[:, :, :, None] * same_gp[None, None, None, :, None, :]
        return v.reshape(2, ns, width, n_state)

    lev = (ll * (2 ** jnp.arange(n_levels))).astype(F32)
    alr, ali = a_pow(lev)
    pad_lev = (-n_levels) % 8

    def lev_slab(a):
        a = a.transpose(1, 0, 2, 3).reshape(2, n_levels, ns, n_state).transpose(0, 2, 1, 3)
        return jnp.pad(a, ((0, 0), (0, 0), (0, pad_lev), (0, 0)))

    return dict(kbd=kbd.astype(BF16),
                in_re=in_slab(s_re, t_re, s_im, t_im, -1.0).astype(BF16),
                in_im=in_slab(s_re, t_im, s_im, t_re, 1.0).astype(BF16),
                out_re_t=out_slab_t(ca_re).astype(BF16), out_im_t=out_slab_t(-ca_im).astype(BF16),
                al_re=lev_slab(alr), al_im=lev_slab(ali))


def _s5_body(ul_ref, uc_ref, kbd_ref, inr_ref, ini_ref, outr_ref, outi_ref, alr_ref, ali_ref,
             yl_ref, yc_ref, sr_ref, si_ref, w_ref, *, nlat, nctx, pad, n_levels, rev):
    ll = S5_CHUNK
    nc = nlat + nctx

    @pl.when(pl.program_id(1) == 0)
    def _():
        zero = jnp.zeros((LANES, LANES), BF16)
        for j in range(ll):
            for i in range(ll):
                lag = j - i if rev else i - j
                w_ref[j * LANES:(j + 1) * LANES, i * LANES:(i + 1) * LANES] = (
                    kbd_ref[0, 0, lag] if lag >= 0 else zero)

    cat = lambda ref: jnp.concatenate([ref[j] for j in range(ll)], axis=1)
    ulat, uctx = cat(ul_ref), cat(uc_ref)
    u = jnp.concatenate([ulat, uctx] if rev else [uctx, ulat], axis=0)
    lat0, ctx0 = (0, nlat) if rev else (nctx, 0)
    lo = 0 if rev else pad
    zero0 = nc if rev else 0
    zeros = jnp.zeros((pad, sr_ref.shape[1]), F32)
    sr_ref[zero0:zero0 + pad, :] = zeros
    si_ref[zero0:zero0 + pad, :] = zeros
    sr_ref[lo:lo + nc, :] = jnp.dot(u, inr_ref[0, 0], preferred_element_type=F32)
    si_ref[lo:lo + nc, :] = jnp.dot(u, ini_ref[0, 0], preferred_element_type=F32)
    for lev in range(n_levels):
        dd = 1 << lev
        src = lo + dd if rev else lo - dd
        a_r = alr_ref[0, 0, lev:lev + 1, :]
        a_i = ali_ref[0, 0, lev:lev + 1, :]
        cr, ci = sr_ref[lo:lo + nc, :], si_ref[lo:lo + nc, :]
        pr, pi = sr_ref[src:src + nc, :], si_ref[src:src + nc, :]
        sr_ref[lo:lo + nc, :] = cr + a_r * pr - a_i * pi
        si_ref[lo:lo + nc, :] = ci + a_r * pi + a_i * pr
    ent = lo + 1 if rev else lo - 1
    er = sr_ref[ent:ent + nc, :].astype(BF16)
    ei = si_ref[ent:ent + nc, :].astype(BF16)
    nt = (((1,), (1,)), ((), ()))
    for ib in range(ll // 2):
        cols = slice(ib * 2 * LANES, (ib + 1) * 2 * LANES)
        y = (jnp.dot(u, w_ref[:, cols], preferred_element_type=F32)
             + lax.dot_general(er, outr_ref[0, 0, cols, :], nt, preferred_element_type=F32)
             + lax.dot_general(ei, outi_ref[0, 0, cols, :], nt, preferred_element_type=F32)).astype(BF16)
        for k in range(2):
            yl_ref[2 * ib + k] = y[lat0:lat0 + nlat, k * LANES:(k + 1) * LANES]
            yc_ref[2 * ib + k] = y[ctx0:ctx0 + nctx, k * LANES:(k + 1) * LANES]


def _s5_levels(nc):
    return max(1, (nc - 1).bit_length())


def _s5_scan(u3, mats, dims, *, rev):
    b, t, tc, d = dims
    ll = S5_CHUNK
    nlat, nctx = t // ll, tc // ll
    n_levels = _s5_levels(nlat + nctx)
    pad = max(8, 1 << (n_levels - 1))
    ns = S5_GROUPS // S5_SLAB
    n_state = S5_SLAB * S5_STATE
    width = ll * LANES
    lev_rows = mats["al_re"].shape[2]
    once = pl.Buffered(1)
    dd = int(rev)
    per_slab = lambda shape: pl.BlockSpec((1, 1) + shape, lambda s, bb: (dd, s) + (0,) * len(shape),
                                          pipeline_mode=once)
    return pl.pallas_call(
        functools.partial(_s5_body, nlat=nlat, nctx=nctx, pad=pad, n_levels=n_levels, rev=rev),
        grid=(ns, b),
        in_specs=[
            pl.BlockSpec((ll, nlat, LANES), lambda s, bb: (0, bb, s), pipeline_mode=once),
            pl.BlockSpec((ll, nctx, LANES), lambda s, bb: (0, b * t // tc + bb, s)),
            per_slab((ll, LANES, LANES)),
            per_slab((width, n_state)), per_slab((width, n_state)),
            per_slab((width, n_state)), per_slab((width, n_state)),
            per_slab((lev_rows, n_state)), per_slab((lev_rows, n_state)),
        ],
        out_specs=[
            pl.BlockSpec((ll, nlat, LANES), lambda s, bb: (0, bb, s)),
            pl.BlockSpec((ll, nctx, LANES), lambda s, bb: (0, bb, s)),
        ],
        out_shape=[
            jax.ShapeDtypeStruct((ll, b * nlat, BRANCH_W), BF16),
            jax.ShapeDtypeStruct((ll, b * nctx, BRANCH_W), BF16),
        ],
        scratch_shapes=[pltpu.VMEM((pad + nlat + nctx, n_state), F32)] * 2
        + [pltpu.VMEM((width, width), BF16)],
        compiler_params=pltpu.CompilerParams(dimension_semantics=("parallel", "arbitrary"),
                                             vmem_limit_bytes=S5_VMEM_LIMIT),
        name="s5_scan_bwd" if rev else "s5_scan_fwd",
    )(u3, u3, mats["kbd"], mats["in_re"], mats["in_im"], mats["out_re_t"], mats["out_im_t"],
      mats["al_re"], mats["al_im"])


def _s5_mixer(pa, mats, dims):
    r = pa.shape[0]
    ll = S5_CHUNK
    u = pa[:, SEG_S5 * BRANCH_W:(SEG_S5 + 1) * BRANCH_W]
    u3 = u.reshape(r // ll, ll, BRANCH_W).transpose(1, 0, 2)
    ys = []
    for dd in range(2):
        yl, yc = _s5_scan(u3, mats, dims, rev=bool(dd))
        ys.append(jnp.concatenate([yl, yc], axis=1).transpose(1, 0, 2).reshape(r, BRANCH_W))
    return ys


def _log_sigmoid(x):
    return -(jnp.maximum(-x, 0.0) + jnp.log1p(jnp.exp(-jnp.abs(x))))


def _mlstm_body(q_ref, k_ref, v_ref, g_ref, gb_ref, h_ref, c_ref, n_ref, m_ref):
    dd = pl.program_id(1)
    fwd = dd == 0

    @pl.when(pl.program_id(2) == 0)
    def _():
        c_ref[...] = jnp.zeros(c_ref.shape, F32)
        n_ref[...] = jnp.zeros(n_ref.shape, F32)
        m_ref[...] = jnp.zeros(m_ref.shape, F32)

    ll = MLSTM_CHUNK
    row = lax.broadcasted_iota(jnp.int32, (ll, ll), 0)
    col = lax.broadcasted_iota(jnp.int32, (ll, ll), 1)
    order = (row - col) * jnp.where(fwd, 1, -1)
    allowed = order >= 0
    tri = jnp.where(allowed, 1.0, 0.0)
    tri_t = jnp.where(order <= 0, 1.0, 0.0)

    g = g_ref[...] + gb_ref[...]
    lf = _log_sigmoid(g)
    g_t = g.T
    lf_t = lf.T
    outs = []
    for h in range(N_HEADS):
        def pick_col(a, base):
            return jnp.where(fwd, a[:, base + h:base + h + 1], a[:, 8 + base + h:8 + base + h + 1])

        def pick_row(a, base):
            return jnp.where(fwd, a[base + h:base + h + 1, :], a[8 + base + h:8 + base + h + 1, :])

        ig_col, lf_col = pick_col(g, 0), pick_col(lf, 4)
        ig_row, lf_row = pick_row(g_t, 0), pick_row(lf_t, 4)
        cum_col = jnp.sum(tri * lf_row, axis=1, keepdims=True)
        cum_row = jnp.sum(tri_t * lf_col, axis=0, keepdims=True)
        gtot = jnp.sum(lf_row, axis=1, keepdims=True)

        hs = slice(h * HEAD_W, (h + 1) * HEAD_W)
        q = q_ref[:, hs]
        k = k_ref[:, hs]
        v = v_ref[:, hs]
        c0 = c_ref[h]
        n0 = n_ref[h]
        m0 = m_ref[h][:, 0:1]

        dmat = jnp.where(allowed, cum_col - cum_row + ig_row, NEG_BIG)
        inter = cum_col + m0
        m_t = jnp.maximum(inter, jnp.max(dmat, axis=1, keepdims=True))
        pm = jnp.exp(dmat - m_t)
        ei = jnp.exp(inter - m_t)
        qk = lax.dot_general(q, k, (((1,), (1,)), ((), ())), preferred_element_type=F32)
        wq = pm * qk
        qc = lax.dot_general(q, c0.astype(BF16), (((1,), (1,)), ((), ())), preferred_element_type=F32)
        num = jnp.dot(wq.astype(BF16), v, preferred_element_type=F32) + ei * qc
        qf = q.astype(F32)
        den = jnp.sum(wq, axis=1, keepdims=True) + ei * jnp.sum(qf * n0, axis=1, keepdims=True)
        outs.append(num / jnp.maximum(jnp.abs(den), jnp.exp(-m_t)))

        w_col = gtot - cum_col + ig_col
        mw = jnp.max(w_col, axis=0, keepdims=True)
        ew = jnp.exp(w_col - mw)
        kf = k.astype(F32)
        vw = (ew * v.astype(F32)).astype(BF16)
        kv = lax.dot_general(vw, k, (((0,), (0,)), ((), ())), preferred_element_type=F32)
        ks = jnp.sum(ew * kf, axis=0, keepdims=True)
        m_new = jnp.maximum(gtot + m0, mw)
        a = jnp.exp(gtot + m0 - m_new)
        e = jnp.exp(mw - m_new)
        c_ref[h] = a * c0 + e * kv
        n_ref[h] = a * n0 + e * ks
        m_ref[h] = jnp.broadcast_to(m_new, (1, LANES))
    h_ref[0] = jnp.concatenate(outs, axis=1).astype(BF16)


def _mlstm(mq, mk, pa, pg, gate_bias, dims):
    b, t, tc, d = dims
    r = pa.shape[0]
    ll = MLSTM_CHUNK
    nctx, nlat = tc // ll, t // ll
    ctx0 = (b * t) // ll

    def rb(bb, dd, c):
        is_ctx = c < nctx
        cc = jnp.where(dd == 0, c, nctx - 1 - c)
        cl = jnp.where(dd == 0, c - nctx, nlat - 1 - (c - nctx))
        return jnp.where(is_ctx, ctx0 + bb * nctx + cc, bb * nlat + cl)

    return pl.pallas_call(
        _mlstm_body,
        grid=(b, 2, nctx + nlat),
        in_specs=[
            pl.BlockSpec((ll, BRANCH_W), lambda bb, dd, c: (rb(bb, dd, c), 0)),
            pl.BlockSpec((ll, BRANCH_W), lambda bb, dd, c: (rb(bb, dd, c), 0)),
            pl.BlockSpec((ll, BRANCH_W), lambda bb, dd, c: (rb(bb, dd, c), SEG_MV)),
            pl.BlockSpec((ll, LANES), lambda bb, dd, c: (rb(bb, dd, c), 0)),
            pl.BlockSpec((1, LANES), lambda bb, dd, c: (0, 0)),
        ],
        out_specs=pl.BlockSpec((1, ll, BRANCH_W), lambda bb, dd, c: (dd, rb(bb, dd, c), 0)),
        out_shape=jax.ShapeDtypeStruct((2, r, BRANCH_W), BF16),
        scratch_shapes=[
            pltpu.VMEM((N_HEADS, HEAD_W, HEAD_W), F32),
            pltpu.VMEM((N_HEADS, 1, HEAD_W), F32),
            pltpu.VMEM((N_HEADS, 1, LANES), F32),
        ],
        compiler_params=_cparams("parallel", "parallel", "arbitrary"),
        name="mlstm",
    )(mq, mk, pa, pg, gate_bias)


def _merge_body(ya_ref, y0_ref, y1_ref, u_ref, h0_ref, h1_ref, mo_ref, ga_ref, gb_ref, gc_ref, x_ref, m_ref,
                d_ref, wglu_ref, wb_ref, wo_ref, wr_ref, wrt_ref,
                x1_ref, h2_ref, aff_ref, afft_ref):
    mod = m_ref[0]
    ys = d_ref[...] * u_ref[...].astype(F32) + y0_ref[...].astype(F32) + y1_ref[...].astype(F32)
    gl = jax.nn.gelu(ys)
    yb = gl * jax.nn.sigmoid(jnp.dot(gl.astype(BF16), wglu_ref[...], preferred_element_type=F32))
    yc = (h0_ref[0].astype(F32) + h1_ref[0].astype(F32)) * jax.nn.sigmoid(mo_ref[...].astype(F32))
    gate = lambda ref: jax.nn.sigmoid(ref[...].astype(F32))
    mixed = (gate(ga_ref) * jnp.dot(ya_ref[...], wb_ref[0], preferred_element_type=F32)
             + gate(gb_ref) * jnp.dot(yb.astype(BF16), wb_ref[1], preferred_element_type=F32)
             + gate(gc_ref) * jnp.dot(yc.astype(BF16), wb_ref[2], preferred_element_type=F32))
    out = jnp.dot(mixed.astype(BF16), wo_ref[...], preferred_element_type=F32)
    x1 = x_ref[...] + mod[2:3, :] * out
    x1_ref[...] = x1
    h2 = _modulated_norm(x1, mod, 3, 4)
    h2_ref[...] = h2.astype(BF16)
    logits = jnp.dot(h2, wr_ref[...], precision=HI, preferred_element_type=F32)
    lane = lax.broadcasted_iota(jnp.int32, logits.shape, 1)
    logits = jnp.where(lane < N_EXPERTS, logits, NEG_BIG)
    ex = jnp.exp(logits - jnp.max(logits, axis=1, keepdims=True))
    aff_ref[...] = ex / jnp.sum(ex, axis=1, keepdims=True)
    lt = lax.dot_general(wrt_ref[...], h2, (((1,), (1,)), ((), ())), precision=HI,
                         preferred_element_type=F32)
    et = jnp.exp(lt - jnp.max(lt, axis=0, keepdims=True))
    afft_ref[...] = et / jnp.sum(et, axis=0, keepdims=True)


def _merge(ya, y0, y1, pa, hm, xa, mod, s5_d, w_glu, w_branch, w_out, wr_pad, wr_t, dims):
    b, t, tc, d = dims
    r = xa.shape[0]
    tm = 256
    gseg = SEG_GATE * BRANCH_W // d
    rowblk = lambda width, col=0: pl.BlockSpec((tm, width), lambda i: (i, col))
    full2 = lambda shape: pl.BlockSpec(shape, lambda i: (0, 0))
    return pl.pallas_call(
        _merge_body,
        grid=(r // tm,),
        in_specs=[
            rowblk(BRANCH_W), rowblk(BRANCH_W), rowblk(BRANCH_W),
            rowblk(BRANCH_W, SEG_S5),
            pl.BlockSpec((1, tm, BRANCH_W), lambda i: (0, i, 0)),
            pl.BlockSpec((1, tm, BRANCH_W), lambda i: (1, i, 0)),
            rowblk(BRANCH_W, SEG_MO),
            rowblk(d, gseg), rowblk(d, gseg + 1), rowblk(d, gseg + 2),
            rowblk(d),
            pl.BlockSpec((1, N_MOD, d), lambda i: (_group_of_block(i, t // tm, b), 0, 0)),
            full2((1, BRANCH_W)),
            full2((BRANCH_W, BRANCH_W)),
            pl.BlockSpec((3, BRANCH_W, d), lambda i: (0, 0, 0)),
            full2((d, d)),
            full2((d, LANES)),
            full2((N_EXPERTS, d)),
        ],
        out_specs=[
            rowblk(d), rowblk(d), rowblk(LANES),
            pl.BlockSpec((N_EXPERTS, tm), lambda i: (0, i)),
        ],
        out_shape=[
            jax.ShapeDtypeStruct((r, d), F32),
            jax.ShapeDtypeStruct((r, d), BF16),
            jax.ShapeDtypeStruct((r, LANES), F32),
            jax.ShapeDtypeStruct((N_EXPERTS, r), F32),
        ],
        compiler_params=_cparams("parallel"),
        name="merge",
    )(ya, y0, y1, pa, hm, hm, pa, pa, pa, pa, xa, mod, s5_d, w_glu, w_branch, w_out, wr_pad, wr_t)


def _route_body(a_ref, tri_ref, low_ref, pos_ref, offs_ref, *, cap):
    a = a_ref[0]
    e, nb, _ = a.shape
    bits = pltpu.bitcast(a, jnp.int32)

    def count(mask):
        c = jnp.sum(jnp.where(mask, 1.0, 0.0), axis=2, keepdims=True)
        return jnp.sum(c, axis=1, keepdims=True)

    def step(i, thr):
        cand = thr | jnp.left_shift(jnp.int32(1), 30 - i)
        return jnp.where(count(bits >= cand) >= cap, cand, thr)

    thr = lax.fori_loop(0, 31, step, jnp.zeros((e, 1, 1), jnp.int32))
    gt = bits > thr
    eq = bits == thr
    need = cap - count(gt)

    tri = tri_ref[...]
    low = low_ref[...]

    def exclusive_prefix(x):
        x2 = x.reshape(e * nb, LANES).astype(BF16)
        incl = jnp.dot(x2, tri, preferred_element_type=F32)
        before = jnp.sum(jnp.dot(low, x2, preferred_element_type=F32), axis=1, keepdims=True)
        return (incl - x2.astype(F32) + before).reshape(e, nb, LANES), before.reshape(e, nb, 1)

    eq_rank, _ = exclusive_prefix(jnp.where(eq, 1.0, 0.0))
    sel = gt | (eq & (eq_rank < need))
    pos, before = exclusive_prefix(jnp.where(sel, 1.0, 0.0))
    pos_ref[0] = jnp.where(sel, pos, -1.0)
    offs_ref[0] = before.astype(jnp.int32)


def _route(aff3, cap):
    ns, e, nb, _ = aff3.shape
    i = jnp.arange(LANES)
    tri = (i[:, None] <= i[None, :]).astype(BF16)
    r = jnp.arange(e * nb)
    low = ((r[:, None] // nb == r[None, :] // nb) & (r[None, :] < r[:, None])).astype(BF16)
    return pl.pallas_call(
        functools.partial(_route_body, cap=cap),
        grid=(ns,),
        in_specs=[
            pl.BlockSpec((1, e, nb, LANES), lambda s: (s, 0, 0, 0)),
            pl.BlockSpec((LANES, LANES), lambda s: (0, 0)),
            pl.BlockSpec((e * nb, e * nb), lambda s: (0, 0)),
        ],
        out_specs=[
            pl.BlockSpec((1, e, nb, LANES), lambda s: (s, 0, 0, 0)),
            pl.BlockSpec((1, e, nb, 1), lambda s: (s, 0, 0, 0)),
        ],
        out_shape=[
            jax.ShapeDtypeStruct((ns, e, nb, LANES), F32),
            jax.ShapeDtypeStruct((ns, e, nb, 1), jnp.int32),
        ],
        compiler_params=_cparams("parallel"),
        name="route",
    )(aff3, tri, low)


SLOT_ALIGN = 16


def _dispatch_body(st_ref, h_ref, p_ref, o_ref, *, tb, win, nj, sub):
    ns, e, j = pl.program_id(0), pl.program_id(1), pl.program_id(2)

    @pl.when(j == 0)
    def _():
        o_ref[...] = jnp.zeros(o_ref.shape, o_ref.dtype)

    for k in range(sub):
        jb = j * sub + k
        start = st_ref[(ns * N_EXPERTS + e) * (nj + 1) + jb]
        end = st_ref[(ns * N_EXPERTS + e) * (nj + 1) + jb + 1]

        @pl.when(end > start)
        def _():
            base = pl.multiple_of((start // SLOT_ALIGN) * SLOT_ALIGN, SLOT_ALIGN)
            slot = (base + lax.broadcasted_iota(jnp.int32, (win, tb), 0)).astype(F32)
            onehot = jnp.where(slot == p_ref[0, :, k * tb:(k + 1) * tb], 1.0, 0.0).astype(BF16)
            rows = jnp.dot(onehot, h_ref[k * tb:(k + 1) * tb, :], preferred_element_type=F32)
            cur = o_ref[0, 0, pl.ds(base, win), :]
            o_ref[0, 0, pl.ds(base, win), :] = cur + rows.astype(o_ref.dtype)


def _dispatch(starts, h2, pos, row0, ns, n, cap, tb):
    d = h2.shape[1]
    nj = n // tb
    sub = min(4, nj)
    njs = nj // sub
    win = tb + SLOT_ALIGN
    capp = cap + win
    blk0 = row0 // (sub * tb)
    pos_rows = pos.reshape(ns * N_EXPERTS * njs, 1, sub * tb)
    grid_spec = pltpu.PrefetchScalarGridSpec(
        num_scalar_prefetch=1,
        grid=(ns, N_EXPERTS, njs),
        in_specs=[
            pl.BlockSpec((sub * tb, d), lambda s, e, j, st: (blk0 + s * njs + j, 0)),
            pl.BlockSpec((1, 1, sub * tb), lambda s, e, j, st: ((s * N_EXPERTS + e) * njs + j, 0, 0)),
        ],
        out_specs=pl.BlockSpec((1, 1, capp, d), lambda s, e, j, st: (s, e, 0, 0)),
    )
    return pl.pallas_call(
        functools.partial(_dispatch_body, tb=tb, win=win, nj=nj, sub=sub),
        grid_spec=grid_spec,
        out_shape=jax.ShapeDtypeStruct((ns, N_EXPERTS, capp, d), BF16),
        compiler_params=_cparams("parallel", "parallel", "arbitrary"),
        name="dispatch",
    )(starts, h2, pos_rows)


def _expert_body(x_ref, wg_ref, wu_ref, wd_ref, y_ref, *, fc):
    x = x_ref[0, 0]
    f = wg_ref.shape[2]
    acc = jnp.zeros((x.shape[0], wd_ref.shape[2]), F32)
    for f0 in range(0, f, fc):
        g = jnp.dot(x, wg_ref[0, :, f0:f0 + fc], preferred_element_type=F32)
        u = jnp.dot(x, wu_ref[0, :, f0:f0 + fc], preferred_element_type=F32)
        hid = (g * jax.nn.sigmoid(g) * u).astype(BF16)
        acc = acc + jnp.dot(hid, wd_ref[0, f0:f0 + fc, :], preferred_element_type=F32)
    y_ref[0, 0] = acc.astype(BF16)


def _experts(xs, w_gate, w_up, w_down, cap):
    ns, e, _, d = xs.shape
    f = w_gate.shape[2]
    ts = min(512, cap)
    return pl.pallas_call(
        functools.partial(_expert_body, fc=min(512, f)),
        grid=(e, ns, cap // ts),
        in_specs=[
            pl.BlockSpec((1, 1, ts, d), lambda ee, s, i: (s, ee, i, 0)),
            pl.BlockSpec((1, d, f), lambda ee, s, i: (ee, 0, 0)),
            pl.BlockSpec((1, d, f), lambda ee, s, i: (ee, 0, 0)),
            pl.BlockSpec((1, f, d), lambda ee, s, i: (ee, 0, 0)),
        ],
        out_specs=pl.BlockSpec((1, 1, ts, d), lambda ee, s, i: (s, ee, i, 0)),
        out_shape=jax.ShapeDtypeStruct((ns, e, cap, d), BF16),
        compiler_params=_cparams("parallel", "parallel", "parallel"),
        name="experts",
    )(xs, w_gate, w_up, w_down)


COMBINE_EXPERTS = 4


def _combine_body(st_ref, *refs, sb, nj, nwin):
    y_refs = refs[:COMBINE_EXPERTS * nwin]
    p_ref, aff_ref, x_ref, m_ref, o_ref = refs[COMBINE_EXPERTS * nwin:]
    ns, j, eg = pl.program_id(0), pl.program_id(1), pl.program_id(2)

    @pl.when(eg == 0)
    def _():
        o_ref[...] = jnp.zeros(o_ref.shape, F32)

    tb = o_ref.shape[0]
    lane = lax.broadcasted_iota(jnp.int32, (tb, LANES), 1)
    for k in range(COMBINE_EXPERTS):
        e = eg * COMBINE_EXPERTS + k
        start = st_ref[(ns * N_EXPERTS + e) * (nj + 1) + j]
        end = st_ref[(ns * N_EXPERTS + e) * (nj + 1) + j + 1]
        a = start // sb

        @pl.when(end > start)
        def _():
            mine = lane == e
            pos = jnp.sum(jnp.where(mine, p_ref[...], 0.0), axis=1, keepdims=True)
            aff = jnp.sum(jnp.where(mine, aff_ref[...], 0.0), axis=1, keepdims=True)
            for w in range(nwin):
                first = (a + w) * sb

                @pl.when(end > first)
                def _():
                    slot = (first + lax.broadcasted_iota(jnp.int32, (tb, sb), 1)).astype(F32)
                    onehot = jnp.where(pos == slot, 1.0, 0.0).astype(BF16)
                    got = jnp.dot(onehot, y_refs[k * nwin + w][0, 0], preferred_element_type=F32)
                    o_ref[...] += aff * got

    @pl.when(eg == N_EXPERTS // COMBINE_EXPERTS - 1)
    def _():
        o_ref[...] = x_ref[...] + m_ref[0, 5:6, :] * o_ref[...]


def _combine(starts, ys, pos_t, aff, x1, mod, row0, ns, n, cap, tb, mod_group):
    d = x1.shape[1]
    nj = n // tb
    span = min(tb, cap)
    sb = min(LANES, cap)
    nwin = span // sb + 1
    nsb = cap // sb
    blk0 = row0 // tb

    def window(k, w):
        def index(s, j, eg, st):
            e = eg * COMBINE_EXPERTS + k
            first = st[(s * N_EXPERTS + e) * (nj + 1) + j] // sb
            return (s, e, jnp.minimum(first + w, nsb - 1), 0)
        return pl.BlockSpec((1, 1, sb, d), index)

    row = lambda s, j, eg, st: (blk0 + s * nj + j, 0)
    windows = [window(k, w) for k in range(COMBINE_EXPERTS) for w in range(nwin)]
    grid_spec = pltpu.PrefetchScalarGridSpec(
        num_scalar_prefetch=1,
        grid=(ns, nj, N_EXPERTS // COMBINE_EXPERTS),
        in_specs=windows + [
            pl.BlockSpec((tb, LANES), row),
            pl.BlockSpec((tb, LANES), row),
            pl.BlockSpec((tb, d), row),
            pl.BlockSpec((1, N_MOD, d), lambda s, j, eg, st: (mod_group(s), 0, 0)),
        ],
        out_specs=pl.BlockSpec((tb, d), lambda s, j, eg, st: (s * nj + j, 0)),
    )
    return pl.pallas_call(
        functools.partial(_combine_body, sb=sb, nj=nj, nwin=nwin),
        grid_spec=grid_spec,
        out_shape=jax.ShapeDtypeStruct((ns * n, d), F32),
        compiler_params=_cparams("parallel", "parallel", "arbitrary"),
        name="combine",
    )(starts, *([ys] * len(windows)), pos_t, aff, x1, mod)


def _expert_choice(h2, aff, aff_t, x1, mod, w_gate, w_up, w_down, row0, ns, n, mod_group):
    r, d = h2.shape
    e = N_EXPERTS
    cap = CAPACITY_FACTOR * n // e
    tb = min(256, n)
    n_pad = max(n, 8 * LANES)
    a = aff_t[:, row0:row0 + ns * n].reshape(e, ns, n).transpose(1, 0, 2)
    if n_pad > n:
        a = jnp.concatenate([a, jnp.full((ns, e, n_pad - n), -1.0, F32)], axis=2)
    pos, offs = _route(a.reshape(ns, e, n_pad // LANES, LANES), cap)
    pos = pos.reshape(ns, e, n_pad)[:, :, :n]
    starts = offs.reshape(ns, e, n_pad // LANES)[:, :, :n // LANES:tb // LANES]
    starts = jnp.concatenate([starts, jnp.full((ns, e, 1), cap, jnp.int32)], axis=2).reshape(-1)
    xs = _dispatch(starts, h2, pos, row0, ns, n, cap, tb)
    ys = _experts(xs, w_gate, w_up, w_down, cap)
    pos_t = jnp.pad(pos.transpose(0, 2, 1).reshape(ns * n, e), ((0, 0), (0, LANES - e)), constant_values=-1.0)
    pos_t = jnp.pad(pos_t, ((row0, r - row0 - ns * n), (0, 0)))
    return _combine(starts, ys, pos_t, aff, x1, mod, row0, ns, n, cap, tb, mod_group)


def _rope_tables(b, t, tc):
    n_freq = DIFF_HEAD_DIM // 4
    inv_freq = ROPE_BASE ** (-jnp.arange(n_freq, dtype=F32) / n_freq)
    pos = jnp.arange(t)
    row = (pos // GRID_W).astype(F32)
    col = (pos % GRID_W).astype(F32)
    ang = jnp.concatenate([row[:, None] * inv_freq, col[:, None] * inv_freq], axis=-1)
    cos, sin = jnp.cos(ang), jnp.sin(ang)
    cos_seg = jnp.concatenate([cos, cos], axis=-1)
    sin_seg = jnp.concatenate([-sin, sin], axis=-1)
    cos_t = jnp.tile(cos_seg, (b, LANES // DIFF_HEAD_DIM))
    sin_t = jnp.tile(sin_seg, (b, LANES // DIFF_HEAD_DIM))
    cos_t = jnp.concatenate([cos_t, jnp.ones((b * tc, LANES), F32)])
    sin_t = jnp.concatenate([sin_t, jnp.zeros((b * tc, LANES), F32)])
    return cos_t, sin_t


def kernel(x, c, ctx, c_ctx, w_mod, b_mod, w_in, attn_q_gain, attn_k_gain, attn_lambda, attn_out_gain,
           s5_lam_re, s5_lam_im, s5_log_step, s5_b_re, s5_b_im, s5_c_re, s5_c_im, s5_d, s5_w_glu,
           mlstm_conv_w, mlstm_conv_b, mlstm_i_bias, mlstm_f_bias,
           w_branch, w_out, w_router, w_exp_gate, w_exp_up, w_exp_down):
    b, t, d = x.shape
    tc = ctx.shape[1]
    n_layers = w_mod.shape[0]
    dims = (b, t, tc, d)
    assert b + 1 <= 8 and t % 512 == 0 and tc % 256 == 0 and (b * tc) % 512 == 0

    xa = jnp.concatenate([x.reshape(b * t, d), ctx.reshape(b * tc, d)])
    cvec = jnp.zeros((8, d), F32).at[:b].set(c).at[b].set(c_ctx)
    mod_all = _mod_vectors(cvec, w_mod, b_mod).reshape(n_layers, 8, N_MOD, d)

    n_main = 8 * BRANCH_W
    wa = jnp.concatenate([w_in[:, :, :n_main], w_in[:, :, n_main + N_GATES:]], axis=2).astype(BF16)
    wg = jnp.pad(w_in[:, :, n_main:n_main + N_GATES], ((0, 0), (0, 0), (0, LANES - N_GATES))).astype(BF16)
    cos_t, sin_t = _rope_tables(b, t, tc)
    seg = jnp.arange(BRANCH_W) // DIFF_HEAD_DIM
    seg_ones = (seg[:, None] == seg[None, :]).astype(BF16)
    n_seg = BRANCH_W // DIFF_HEAD_DIM
    gq = jnp.tile(attn_q_gain, (1, n_seg))[:, None, :] * (DIFF_HEAD_DIM ** -0.5 * math.log2(math.e))
    gk = jnp.tile(attn_k_gain, (1, n_seg))[:, None, :]
    conv_w = jnp.pad(mlstm_conv_w, ((0, 0), (0, 8 - CONV_K), (0, 0)))
    gate_bias = jnp.stack([mlstm_i_bias, mlstm_f_bias], axis=2).reshape(n_layers, 1, N_GATES)
    gate_bias = jnp.pad(gate_bias, ((0, 0), (0, 0), (0, LANES - N_GATES)))
    wr_pad = jnp.pad(w_router, ((0, 0), (0, 0), (0, LANES - N_EXPERTS)))
    wr_t = jnp.swapaxes(w_router, 1, 2)
    n_chunks = (t + tc) // S5_CHUNK
    n_levels = _s5_levels(n_chunks)

    for l in range(n_layers):
        with_ctx = l != n_layers - 1
        lam_init = 0.8 - 0.6 * math.exp(-0.3 * l)
        mod = mod_all[l]
        pa, pg = _project(xa, mod, wa[l], wg[l], dims)
        qh, kh, mq, mk = _prepare(pa, cos_t, sin_t, gq[l], gk[l], seg_ones, conv_w[l],
                                  mlstm_conv_b[l][None, :], dims)
        og = attn_out_gain[l][None, :]
        bound = 1.01 * DIFF_HEAD_DIM * jnp.max(jnp.abs(gq[l])) * jnp.max(jnp.abs(gk[l]))
        ya_l = _attention(qh, kh, pa, attn_lambda[l], og, bound, lam_init, dims, ctx_queries=False)
        if with_ctx:
            ya_c = _attention(qh, kh, pa, attn_lambda[l], og, bound, lam_init, dims, ctx_queries=True)
        else:
            ya_c = jnp.zeros((b * tc, BRANCH_W), BF16)
        ya = jnp.concatenate([ya_l, ya_c])
        mats = _s5_matrices(s5_lam_re[l], s5_lam_im[l], s5_log_step[l], s5_b_re[l], s5_b_im[l],
                            s5_c_re[l], s5_c_im[l], n_levels)
        y0, y1 = _s5_mixer(pa, mats, dims)
        hm = _mlstm(mq, mk, pa, pg, gate_bias[l], dims)
        x1, h2, aff, aff_t = _merge(ya, y0, y1, pa, hm, xa, mod, s5_d[l][None, :],
                                    s5_w_glu[l].astype(BF16), w_branch[l].astype(BF16),
                                    w_out[l].astype(BF16), wr_pad[l], wr_t[l], dims)
        wge, wue, wde = (w_exp_gate[l].astype(BF16), w_exp_up[l].astype(BF16), w_exp_down[l].astype(BF16))
        x2_l = _expert_choice(h2, aff, aff_t, x1, mod, wge, wue, wde, 0, b, t, lambda s: s)
        if with_ctx:
            x2_c = _expert_choice(h2, aff, aff_t, x1, mod, wge, wue, wde, b * t, b, tc, lambda s: b)
        else:
            x2_c = x1[b * t:]
        xa = jnp.concatenate([x2_l, x2_c])
    return xa[:b * t].reshape(b, t, d)
```

```python
import functools
import math

import jax
import jax.numpy as jnp
from jax import lax
from jax.experimental import pallas as pl
from jax.experimental.pallas import tpu as pltpu

F32 = jnp.float32
BF16 = jnp.bfloat16
HI = lax.Precision.HIGHEST

N_MOD = 6
NORM_EPS = 1e-6
GRID_W = 64
ROPE_BASE = 10000.0
N_HEADS = 4
DIFF_HEAD_DIM = 64
HEAD_W = 128
BRANCH_W = 512
S5_GROUPS = 32
S5_GROUP = 16
S5_STATE = 64
S5_CHUNK = 16
S5_SLAB = 8
S5_VMEM_LIMIT = 58 * 1024 * 1024
MLSTM_CHUNK = 128
CONV_K = 5
N_GATES = 16
N_EXPERTS = 16
CAPACITY_FACTOR = 2
LANES = 128
VMEM_LIMIT = 52 * 1024 * 1024
NEG_BIG = -1e30

SEG_Q, SEG_K, SEG_V, SEG_S5, SEG_MQ, SEG_MK, SEG_MV, SEG_MO, SEG_GATE = range(9)
PA_WIDTH = 8 * BRANCH_W + 3 * 1024


def _cparams(*sem):
    return pltpu.CompilerParams(dimension_semantics=sem, vmem_limit_bytes=VMEM_LIMIT)


def _mod_body(c_ref, w_ref, b_ref, o_ref):
    cv = c_ref[...]
    s = cv * jax.nn.sigmoid(cv)
    o_ref[0] = jnp.dot(s, w_ref[0], precision=HI, preferred_element_type=F32) + b_ref[0]


def _mod_vectors(cvec, w_mod, b_mod):
    n_layers, d, n = w_mod.shape
    tn = n // 4
    return pl.pallas_call(
        _mod_body,
        grid=(n_layers, n // tn),
        in_specs=[
            pl.BlockSpec((8, d), lambda l, j: (0, 0)),
            pl.BlockSpec((1, d, tn), lambda l, j: (l, 0, j)),
            pl.BlockSpec((1, 1, tn), lambda l, j: (l, 0, j)),
        ],
        out_specs=pl.BlockSpec((1, 8, tn), lambda l, j: (l, 0, j)),
        out_shape=jax.ShapeDtypeStruct((n_layers, 8, n), F32),
        compiler_params=_cparams("parallel", "parallel"),
        name="mod_vectors",
    )(cvec, w_mod, b_mod.reshape(n_layers, 1, n))


def _modulated_norm(x, mod, i_shift, i_scale):
    ms = jnp.mean(x * x, axis=-1, keepdims=True)
    xn = x * lax.rsqrt(ms + NORM_EPS)
    return xn * (1.0 + mod[i_scale:i_scale + 1, :]) + mod[i_shift:i_shift + 1, :]


def _proj_body(x_ref, m_ref, w_ref, wg_ref, pa_ref, pg_ref, *, tn):
    hb = _modulated_norm(x_ref[...], m_ref[0], 0, 1).astype(BF16)
    pg_ref[...] = jnp.dot(hb, wg_ref[...], preferred_element_type=F32)
    for c0 in range(0, pa_ref.shape[1], tn):
        pa_ref[:, c0:c0 + tn] = jnp.dot(hb, w_ref[:, c0:c0 + tn], preferred_element_type=F32).astype(BF16)


def _group_of_block(i, blocks_per_sample, n_samples):
    return jnp.minimum(i // blocks_per_sample, n_samples)


def _project(xa, mod, wa, wg, dims):
    b, t, tc, d = dims
    r = xa.shape[0]
    tm = 512
    npa = wa.shape[1]
    once = pl.Buffered(1)
    return pl.pallas_call(
        functools.partial(_proj_body, tn=1024),
        grid=(r // tm,),
        in_specs=[
            pl.BlockSpec((tm, d), lambda i: (i, 0)),
            pl.BlockSpec((1, N_MOD, d), lambda i: (_group_of_block(i, t // tm, b), 0, 0)),
            pl.BlockSpec((d, npa), lambda i: (0, 0), pipeline_mode=once),
            pl.BlockSpec((d, LANES), lambda i: (0, 0), pipeline_mode=once),
        ],
        out_specs=[
            pl.BlockSpec((tm, npa), lambda i: (i, 0)),
            pl.BlockSpec((tm, LANES), lambda i: (i, 0)),
        ],
        out_shape=[
            jax.ShapeDtypeStruct((r, npa), BF16),
            jax.ShapeDtypeStruct((r, LANES), F32),
        ],
        compiler_params=_cparams("parallel"),
        name="in_proj",
    )(xa, mod, wa, wg)


def _qk_norm_rope(x_bf, gain, cosf, sinf, seg_ones, first_half):
    x = x_bf.astype(F32)
    x2 = x * x
    hi = x2.astype(BF16)
    lo = (x2 - hi.astype(F32)).astype(BF16)
    ss = (jnp.dot(hi, seg_ones, preferred_element_type=F32)
          + jnp.dot(lo, seg_ones, preferred_element_type=F32))
    xn = x * lax.rsqrt(ss * (1.0 / DIFF_HEAD_DIM) + NORM_EPS) * gain
    half = DIFF_HEAD_DIM // 2
    width = x.shape[1]
    nxt = pltpu.roll(xn, width - half, 1)
    prv = pltpu.roll(xn, half, 1)
    partner = jnp.where(first_half, nxt, prv)
    return xn * cosf + partner * sinf


def _short_conv_silu(prev_ref, cur_ref, next_ref, w, bias, at_start, at_end, out_scale):
    tp = cur_ref.shape[0]
    prev = prev_ref[...].astype(F32)[8:16]
    nxt = next_ref[...].astype(F32)[0:8]
    prev = jnp.where(at_start, 0.0, prev)
    nxt = jnp.where(at_end, 0.0, nxt)
    ext = jnp.concatenate([prev, cur_ref[...].astype(F32), nxt], axis=0)
    acc = bias
    for kk in range(CONV_K):
        off = 8 + kk - CONV_K // 2
        acc = acc + w[kk:kk + 1, :] * ext[off:off + tp]
    y = acc * jax.nn.sigmoid(acc)
    return y * out_scale


def _prep_body(q_ref, k_ref, mqp_ref, mq_ref, mqn_ref, mkp_ref, mk_ref, mkn_ref,
               cos_ref, sin_ref, gq_ref, gk_ref, so_ref, cw_ref, cb_ref,
               qo_ref, ko_ref, mqo_ref, mko_ref, *, b, t, tc, tp):
    cos4 = jnp.concatenate([cos_ref[...]] * 4, axis=1)
    sin4 = jnp.concatenate([sin_ref[...]] * 4, axis=1)
    lane = lax.broadcasted_iota(jnp.int32, (tp, BRANCH_W), 1)
    first_half = (lane % DIFF_HEAD_DIM) < (DIFF_HEAD_DIM // 2)
    seg_ones = so_ref[...]
    qo_ref[...] = _qk_norm_rope(q_ref[...], gq_ref[...], cos4, sin4, seg_ones, first_half).astype(BF16)
    ko_ref[...] = _qk_norm_rope(k_ref[...], gk_ref[...], cos4, sin4, seg_ones, first_half).astype(BF16)

    row0 = pl.program_id(0) * tp
    in_lat = row0 < b * t
    local = jnp.where(in_lat, row0 % t, (row0 - b * t) % tc)
    seq_len = jnp.where(in_lat, t, tc)
    at_start = local == 0
    at_end = local + tp == seq_len
    cw = cw_ref[...]
    cb = cb_ref[...]
    mqo_ref[...] = _short_conv_silu(mqp_ref, mq_ref, mqn_ref, cw[:, :BRANCH_W], cb[:, :BRANCH_W],
                                    at_start, at_end, 1.0).astype(BF16)
    mko_ref[...] = _short_conv_silu(mkp_ref, mk_ref, mkn_ref, cw[:, BRANCH_W:], cb[:, BRANCH_W:],
                                    at_start, at_end, HEAD_W ** -0.5).astype(BF16)


def _prepare(pa, cos_tab, sin_tab, gq, gk, seg_ones, conv_w, conv_b, dims):
    b, t, tc, d = dims
    r = pa.shape[0]
    tp = 256
    halo = 16
    hb = tp // halo
    last_halo = r // halo - 1

    def cur(seg):
        return pl.BlockSpec((tp, BRANCH_W), lambda i: (i, seg))

    def prev(seg):
        return pl.BlockSpec((halo, BRANCH_W), lambda i: (jnp.maximum(i * hb - 1, 0), seg))

    def nxt(seg):
        return pl.BlockSpec((halo, BRANCH_W), lambda i: (jnp.minimum((i + 1) * hb, last_halo), seg))

    full = lambda shape: pl.BlockSpec(shape, lambda i: (0, 0))
    out = jax.ShapeDtypeStruct((r, BRANCH_W), BF16)
    return pl.pallas_call(
        functools.partial(_prep_body, b=b, t=t, tc=tc, tp=tp),
        grid=(r // tp,),
        in_specs=[
            cur(SEG_Q), cur(SEG_K),
            prev(SEG_MQ), cur(SEG_MQ), nxt(SEG_MQ),
            prev(SEG_MK), cur(SEG_MK), nxt(SEG_MK),
            pl.BlockSpec((tp, LANES), lambda i: (i, 0)),
            pl.BlockSpec((tp, LANES), lambda i: (i, 0)),
            full((1, BRANCH_W)), full((1, BRANCH_W)),
            full((BRANCH_W, BRANCH_W)),
            full((8, 2 * BRANCH_W)), full((1, 2 * BRANCH_W)),
        ],
        out_specs=[pl.BlockSpec((tp, BRANCH_W), lambda i: (i, 0))] * 4,
        out_shape=[out, out, out, out],
        compiler_params=_cparams("parallel"),
        name="row_prep",
    )(pa, pa, pa, pa, pa, pa, pa, pa, cos_tab, sin_tab, gq, gk, seg_ones, conv_w, conv_b)


def _attn_body(lam_ref, og_ref, sh_ref, q_ref, k_ref, v_ref, kc_ref, vc_ref, o_ref,
               q0_ref, q1_ref, m_ref, l_ref, acc_ref, *, lam_init, has_ctx, nk, fixed_shift):
    kj = pl.program_id(3)

    def process(kb, vb):
        for mi, qr in enumerate((q0_ref, q1_ref)):
            s = jnp.dot(kb, qr[...], preferred_element_type=F32)
            if fixed_shift:
                p = jnp.exp2(s - sh_ref[0:1, 0:1])
                l_ref[mi] += jnp.sum(p, axis=0, keepdims=True)
                acc_ref[mi] += lax.dot_general(vb, p.astype(BF16), (((0,), (0,)), ((), ())),
                                               preferred_element_type=F32)
            else:
                m_old = m_ref[mi]
                m_new = jnp.maximum(m_old, jnp.max(s, axis=0, keepdims=True))
                alpha = jnp.exp2(m_old - m_new)
                p = jnp.exp2(s - m_new)
                l_ref[mi] = alpha * l_ref[mi] + jnp.sum(p, axis=0, keepdims=True)
                pv = lax.dot_general(vb, p.astype(BF16), (((0,), (0,)), ((), ())),
                                     preferred_element_type=F32)
                acc_ref[mi] = alpha * acc_ref[mi] + pv
                m_ref[mi] = m_new

    @pl.when(kj == 0)
    def _():
        qt = q_ref[...].astype(F32).T.astype(BF16)
        row = lax.broadcasted_iota(jnp.int32, qt.shape, 0)
        zero = jnp.zeros_like(qt)
        q0_ref[...] = jnp.where(row < DIFF_HEAD_DIM, qt, zero)
        q1_ref[...] = jnp.where(row >= DIFF_HEAD_DIM, qt, zero)
        m_ref[...] = jnp.full(m_ref.shape, NEG_BIG, F32)
        l_ref[...] = jnp.zeros(l_ref.shape, F32)
        acc_ref[...] = jnp.zeros(acc_ref.shape, F32)
        if has_ctx:
            process(kc_ref[...], vc_ref[...])

    process(k_ref[...], v_ref[...])

    @pl.when(kj == nk - 1)
    def _():
        lv = lam_ref[...]
        lam = (jnp.exp(jnp.sum(lv[0:1] * lv[1:2], keepdims=True))
               - jnp.exp(jnp.sum(lv[2:3] * lv[3:4], keepdims=True)) + lam_init)
        o = acc_ref[0] / l_ref[0] - lam * (acc_ref[1] / l_ref[1])
        ms = jnp.mean(o * o, axis=0, keepdims=True)
        o = o * lax.rsqrt(ms + NORM_EPS)
        o_ref[...] = (o.T * (og_ref[...] * (1.0 - lam_init))).astype(BF16)


MAX_FIXED_SHIFT = 48.0


def _attention(qh, kh, pa, lam_vecs, out_gain, score_bound, lam_init, dims, *, ctx_queries):
    shift = jnp.full((1, LANES), score_bound, F32)
    run = lambda fixed: _attention_call(qh, kh, pa, lam_vecs, out_gain, shift, lam_init, dims,
                                        ctx_queries=ctx_queries, fixed_shift=fixed)
    return lax.cond(score_bound <= MAX_FIXED_SHIFT, lambda: run(True), lambda: run(False))


def _attention_call(qh, kh, pa, lam_vecs, out_gain, shift, lam_init, dims, *, ctx_queries, fixed_shift):
    b, t, tc, d = dims
    v_col = SEG_V * (BRANCH_W // HEAD_W)
    ctx_blk0 = (b * t) // tc
    if ctx_queries:
        tq = tk = tc
        nq, nk = 1, 1
        q_row = lambda bb, qi: ctx_blk0 + bb
        k_row = lambda bb, kj: ctx_blk0 + bb
        n_rows = b * tc
        o_row = lambda bb, qi: bb
    else:
        tq = min(2048, t)
        tk = min(1024, t)
        nq, nk = t // tq, t // tk
        q_row = lambda bb, qi: bb * nq + qi
        k_row = lambda bb, kj: bb * nk + kj
        n_rows = b * t
        o_row = q_row
    body = functools.partial(_attn_body, lam_init=lam_init, has_ctx=not ctx_queries, nk=nk,
                             fixed_shift=fixed_shift)
    return pl.pallas_call(
        body,
        grid=(b, N_HEADS, nq, nk),
        in_specs=[
            pl.BlockSpec((4, DIFF_HEAD_DIM), lambda bb, h, qi, kj: (0, 0)),
            pl.BlockSpec((1, HEAD_W), lambda bb, h, qi, kj: (0, 0)),
            pl.BlockSpec((1, LANES), lambda bb, h, qi, kj: (0, 0)),
            pl.BlockSpec((tq, HEAD_W), lambda bb, h, qi, kj: (q_row(bb, qi), h)),
            pl.BlockSpec((tk, HEAD_W), lambda bb, h, qi, kj: (k_row(bb, kj), h)),
            pl.BlockSpec((tk, HEAD_W), lambda bb, h, qi, kj: (k_row(bb, kj), v_col + h)),
            pl.BlockSpec((tc, HEAD_W), lambda bb, h, qi, kj: (ctx_blk0 + bb, h)),
            pl.BlockSpec((tc, HEAD_W), lambda bb, h, qi, kj: (ctx_blk0 + bb, v_col + h)),
        ],
        out_specs=pl.BlockSpec((tq, HEAD_W), lambda bb, h, qi, kj: (o_row(bb, qi), h)),
        out_shape=jax.ShapeDtypeStruct((n_rows, BRANCH_W), BF16),
        scratch_shapes=[
            pltpu.VMEM((HEAD_W, tq), BF16),
            pltpu.VMEM((HEAD_W, tq), BF16),
            pltpu.VMEM((2, 1, tq), F32),
            pltpu.VMEM((2, 1, tq), F32),
            pltpu.VMEM((2, HEAD_W, tq), F32),
        ],
        compiler_params=_cparams("parallel", "parallel", "parallel", "arbitrary"),
        name=("diff_attn_ctx" if ctx_queries else "diff_attn") + ("_fixed" if fixed_shift else ""),
    )(lam_vecs, out_gain, shift, qh, kh, pa, kh, pa)


def _s5_matrices(lam_re, lam_im, log_step, b_re, b_im, c_re, c_im, n_levels):
    ll, hg, pp, gg = S5_CHUNK, S5_GROUP, S5_STATE, S5_GROUPS
    dt = jnp.exp(log_step)[:, :, None]
    lr, li = lam_re * dt, lam_im * dt

    def a_pow(tau):
        tau = tau.astype(F32)[:, None, None, None]
        mag = jnp.exp(lr * tau)
        return mag * jnp.cos(li * tau), mag * jnp.sin(li * tau)

    ar1, ai1 = a_pow(jnp.ones((1,)))
    nr, ni = ar1[0] - 1.0, ai1[0]
    den = lam_re * lam_re + lam_im * lam_im
    f_re = (nr * lam_re + ni * lam_im) / den
    f_im = (ni * lam_re - nr * lam_im) / den
    bb_re = f_re[..., None] * b_re - f_im[..., None] * b_im
    bb_im = f_re[..., None] * b_im + f_im[..., None] * b_re

    ar, ai = a_pow(jnp.arange(ll + 1))
    ca_re = c_re[None] * ar[:, :, :, None, :] - c_im[None] * ai[:, :, :, None, :]
    ca_im = c_re[None] * ai[:, :, :, None, :] + c_im[None] * ar[:, :, :, None, :]
    kk = (jnp.einsum('tdgop,dgph->tdgoh', ca_re, bb_re, precision=HI)
          - jnp.einsum('tdgop,dgph->tdgoh', ca_im, bb_im, precision=HI))
    ns, sl = gg // S5_SLAB, S5_SLAB
    n_state = sl * pp
    width = ll * LANES
    jj = jnp.arange(ll)
    grp = jnp.arange(sl)
    same_go = (grp[:, None] == jnp.arange(LANES)[None, :] // hg).astype(F32)
    same_gp = (grp[:, None] == jnp.arange(n_state)[None, :] // pp).astype(F32)

    kt = kk[:ll].transpose(1, 0, 2, 4, 3).reshape(2, ll, ns, sl, hg, hg)
    kt = kt.transpose(0, 2, 1, 4, 3, 5).reshape(2, ns, ll, hg, LANES)
    kbd = (kt[:, :, :, None] * same_go[None, None, None, :, None, :]).reshape(2, ns, ll, LANES, LANES)

    pw = jnp.stack([ll - 1 - jj, jj])
    sel = lambda a: jnp.stack([a[pw[0], 0], a[pw[1], 1]]).reshape(2, ll, ns, n_state).transpose(0, 2, 1, 3)
    s_re, s_im = sel(ar), sel(ai)
    lane_bb = lambda m: m.reshape(2, ns, sl, pp, hg).transpose(0, 1, 4, 2, 3).reshape(2, ns, hg, n_state)
    t_re, t_im = lane_bb(bb_re), lane_bb(bb_im)

    def in_slab(a, bmat, c, dmat, sign):
        v = a[:, :, :, None, :] * bmat[:, :, None, :, :] + sign * c[:, :, :, None, :] * dmat[:, :, None, :, :]
        v = v[:, :, :, None] * same_gp[None, None, None, :, None, :]
        return v.reshape(2, ns, width, n_state)

    po = jnp.stack([jj + 1, ll - jj])

    def out_slab_t(a):
        m = jnp.stack([a[po[0], 0], a[po[1], 1]])
        m = m.reshape(2, ll, ns, sl, hg, pp).transpose(0, 2, 1, 4, 3, 5).reshape(2, ns, ll, hg, n_state)
        v = m[:, :, :, None] * same_gp[None, None, None, :, None, :]
        return v.reshape(2, ns, width, n_state)

    lev = (ll * (2 ** jnp.arange(n_levels))).astype(F32)
    alr, ali = a_pow(lev)
    pad_lev = (-n_levels) % 8

    def lev_slab(a):
        a = a.transpose(1, 0, 2, 3).reshape(2, n_levels, ns, n_state).transpose(0, 2, 1, 3)
        return jnp.pad(a, ((0, 0), (0, 0), (0, pad_lev), (0, 0)))

    return dict(kbd=kbd.astype(BF16),
                in_re=in_slab(s_re, t_re, s_im, t_im, -1.0).astype(BF16),
                in_im=in_slab(s_re, t_im, s_im, t_re, 1.0).astype(BF16),
                out_re_t=out_slab_t(ca_re).astype(BF16), out_im_t=out_slab_t(-ca_im).astype(BF16),
                al_re=lev_slab(alr), al_im=lev_slab(ali))


def _s5_body(ul_ref, uc_ref, kbd_ref, inr_ref, ini_ref, outr_ref, outi_ref, alr_ref, ali_ref,
             yl_ref, yc_ref, sr_ref, si_ref, w_ref, *, nlat, nctx, pad, n_levels, rev):
    ll = S5_CHUNK
    nc = nlat + nctx

    @pl.when(pl.program_id(1) == 0)
    def _():
        zero = jnp.zeros((LANES, LANES), BF16)
        for j in range(ll):
            for i in range(ll):
                lag = j - i if rev else i - j
                w_ref[j * LANES:(j + 1) * LANES, i * LANES:(i + 1) * LANES] = (
                    kbd_ref[0, 0, lag] if lag >= 0 else zero)

    cat = lambda ref: jnp.concatenate([ref[j] for j in range(ll)], axis=1)
    ulat, uctx = cat(ul_ref), cat(uc_ref)
    u = jnp.concatenate([ulat, uctx] if rev else [uctx, ulat], axis=0)
    lat0, ctx0 = (0, nlat) if rev else (nctx, 0)
    lo = 0 if rev else pad
    zero0 = nc if rev else 0
    zeros = jnp.zeros((pad, sr_ref.shape[1]), F32)
    sr_ref[zero0:zero0 + pad, :] = zeros
    si_ref[zero0:zero0 + pad, :] = zeros
    sr_ref[lo:lo + nc, :] = jnp.dot(u, inr_ref[0, 0], preferred_element_type=F32)
    si_ref[lo:lo + nc, :] = jnp.dot(u, ini_ref[0, 0], preferred_element_type=F32)
    for lev in range(n_levels):
        dd = 1 << lev
        src = lo + dd if rev else lo - dd
        a_r = alr_ref[0, 0, lev:lev + 1, :]
        a_i = ali_ref[0, 0, lev:lev + 1, :]
        cr, ci = sr_ref[lo:lo + nc, :], si_ref[lo:lo + nc, :]
        pr, pi = sr_ref[src:src + nc, :], si_ref[src:src + nc, :]
        sr_ref[lo:lo + nc, :] = cr + a_r * pr - a_i * pi
        si_ref[lo:lo + nc, :] = ci + a_r * pi + a_i * pr
    ent = lo + 1 if rev else lo - 1
    er = sr_ref[ent:ent + nc, :].astype(BF16)
    ei = si_ref[ent:ent + nc, :].astype(BF16)
    nt = (((1,), (1,)), ((), ()))
    for ib in range(ll // 2):
        cols = slice(ib * 2 * LANES, (ib + 1) * 2 * LANES)
        y = (jnp.dot(u, w_ref[:, cols], preferred_element_type=F32)
             + lax.dot_general(er, outr_ref[0, 0, cols, :], nt, preferred_element_type=F32)
             + lax.dot_general(ei, outi_ref[0, 0, cols, :], nt, preferred_element_type=F32)).astype(BF16)
        for k in range(2):
            yl_ref[2 * ib + k] = y[lat0:lat0 + nlat, k * LANES:(k + 1) * LANES]
            yc_ref[2 * ib + k] = y[ctx0:ctx0 + nctx, k * LANES:(k + 1) * LANES]


def _s5_levels(nc):
    return max(1, (nc - 1).bit_length())


def _s5_scan(u3, mats, dims, *, rev):
    b, t, tc, d = dims
    ll = S5_CHUNK
    nlat, nctx = t // ll, tc // ll
    n_levels = _s5_levels(nlat + nctx)
    pad = max(8, 1 << (n_levels - 1))
    ns = S5_GROUPS // S5_SLAB
    n_state = S5_SLAB * S5_STATE
    width = ll * LANES
    lev_rows = mats["al_re"].shape[2]
    once = pl.Buffered(1)
    dd = int(rev)
    per_slab = lambda shape: pl.BlockSpec((1, 1) + shape, lambda s, bb: (dd, s) + (0,) * len(shape),
                                          pipeline_mode=once)
    return pl.pallas_call(
        functools.partial(_s5_body, nlat=nlat, nctx=nctx, pad=pad, n_levels=n_levels, rev=rev),
        grid=(ns, b),
        in_specs=[
            pl.BlockSpec((ll, nlat, LANES), lambda s, bb: (0, bb, s), pipeline_mode=once),
            pl.BlockSpec((ll, nctx, LANES), lambda s, bb: (0, b * t // tc + bb, s)),
            per_slab((ll, LANES, LANES)),
            per_slab((width, n_state)), per_slab((width, n_state)),
            per_slab((width, n_state)), per_slab((width, n_state)),
            per_slab((lev_rows, n_state)), per_slab((lev_rows, n_state)),
        ],
        out_specs=[
            pl.BlockSpec((ll, nlat, LANES), lambda s, bb: (0, bb, s)),
            pl.BlockSpec((ll, nctx, LANES), lambda s, bb: (0, bb, s)),
        ],
        out_shape=[
            jax.ShapeDtypeStruct((ll, b * nlat, BRANCH_W), BF16),
            jax.ShapeDtypeStruct((ll, b * nctx, BRANCH_W), BF16),
        ],
        scratch_shapes=[pltpu.VMEM((pad + nlat + nctx, n_state), F32)] * 2
        + [pltpu.VMEM((width, width), BF16)],
        compiler_params=pltpu.CompilerParams(dimension_semantics=("parallel", "arbitrary"),
                                             vmem_limit_bytes=S5_VMEM_LIMIT),
        name="s5_scan_bwd" if rev else "s5_scan_fwd",
    )(u3, u3, mats["kbd"], mats["in_re"], mats["in_im"], mats["out_re_t"], mats["out_im_t"],
      mats["al_re"], mats["al_im"])


def _s5_mixer(pa, mats, dims):
    r = pa.shape[0]
    ll = S5_CHUNK
    u = pa[:, SEG_S5 * BRANCH_W:(SEG_S5 + 1) * BRANCH_W]
    u3 = u.reshape(r // ll, ll, BRANCH_W).transpose(1, 0, 2)
    ys = []
    for dd in range(2):
        yl, yc = _s5_scan(u3, mats, dims, rev=bool(dd))
        ys.append(jnp.concatenate([yl, yc], axis=1).transpose(1, 0, 2).reshape(r, BRANCH_W))
    return ys


def _log_sigmoid(x):
    return -(jnp.maximum(-x, 0.0) + jnp.log1p(jnp.exp(-jnp.abs(x))))


def _mlstm_body(q_ref, k_ref, v_ref, g_ref, gb_ref, h_ref, c_ref, n_ref, m_ref):
    dd = pl.program_id(1)
    fwd = dd == 0

    @pl.when(pl.program_id(2) == 0)
    def _():
        c_ref[...] = jnp.zeros(c_ref.shape, F32)
        n_ref[...] = jnp.zeros(n_ref.shape, F32)
        m_ref[...] = jnp.zeros(m_ref.shape, F32)

    ll = MLSTM_CHUNK
    row = lax.broadcasted_iota(jnp.int32, (ll, ll), 0)
    col = lax.broadcasted_iota(jnp.int32, (ll, ll), 1)
    sign = jnp.where(fwd, 1, -1)
    before = (col - row) * sign >= 0
    before_f = jnp.where(before, 1.0, 0.0)
    after_f = jnp.where((row - col) * sign >= 0, 1.0, 0.0)

    g = g_ref[...] + gb_ref[...]
    lf = _log_sigmoid(g)
    g_t = g.T
    lf_t = lf.T
    cum_cols = jnp.dot(after_f, lf, precision=HI, preferred_element_type=F32)
    cum_rows = jnp.dot(lf_t, before_f, precision=HI, preferred_element_type=F32)
    nt = (((1,), (1,)), ((), ()))
    outs = []
    for h in range(N_HEADS):
        def pick_col(a, base):
            return jnp.where(fwd, a[:, base + h:base + h + 1], a[:, 8 + base + h:8 + base + h + 1])

        def pick_row(a, base):
            return jnp.where(fwd, a[base + h:base + h + 1, :], a[8 + base + h:8 + base + h + 1, :])

        ig_col, cum_col = pick_col(g, 0), pick_col(cum_cols, 4)
        ig_row, lf_row, cum_row = pick_row(g_t, 0), pick_row(lf_t, 4), pick_row(cum_rows, 4)
        gtot = jnp.sum(lf_row, axis=1, keepdims=True)

        hs = slice(h * HEAD_W, (h + 1) * HEAD_W)
        q = q_ref[:, hs]
        k = k_ref[:, hs]
        v_t = v_ref[:, hs].astype(F32).T.astype(BF16)
        c0 = c_ref[h]
        n0 = n_ref[h]
        m0 = m_ref[h][:, 0:1]

        dmat = jnp.where(before, cum_row + (ig_col - cum_col), NEG_BIG)
        inter = cum_row + m0
        m_t = jnp.maximum(inter, jnp.max(dmat, axis=0, keepdims=True))
        pm = jnp.exp(dmat - m_t)
        ei = jnp.exp(inter - m_t)
        kq = lax.dot_general(k, q, nt, preferred_element_type=F32)
        wq = pm * kq
        cq = lax.dot_general(c0.astype(BF16), q, nt, preferred_element_type=F32)
        num = jnp.dot(v_t, wq.astype(BF16), preferred_element_type=F32) + ei * cq
        nq = lax.dot_general(jnp.broadcast_to(n0, (8, HEAD_W)).astype(BF16), q, nt,
                             preferred_element_type=F32)[0:1]
        den = jnp.sum(wq, axis=0, keepdims=True) + ei * nq
        outs.append((num / jnp.maximum(jnp.abs(den), jnp.exp(-m_t))).T)

        w_row = gtot - cum_row + ig_row
        mw = jnp.max(w_row, axis=1, keepdims=True)
        ew = jnp.exp(w_row - mw)
        vw = (v_t.astype(F32) * ew).astype(BF16)
        kv = jnp.dot(vw, k, preferred_element_type=F32)
        ks = jnp.dot(jnp.broadcast_to(ew, (8, ll)).astype(BF16), k, preferred_element_type=F32)[0:1]
        m_new = jnp.maximum(gtot + m0, mw)
        a = jnp.exp(gtot + m0 - m_new)
        e = jnp.exp(mw - m_new)
        c_ref[h] = a * c0 + e * kv
        n_ref[h] = a * n0 + e * ks
        m_ref[h] = jnp.broadcast_to(m_new, (1, LANES))
    h_ref[0] = jnp.concatenate(outs, axis=1).astype(BF16)


def _mlstm(mq, mk, pa, pg, gate_bias, dims):
    b, t, tc, d = dims
    r = pa.shape[0]
    ll = MLSTM_CHUNK
    nctx, nlat = tc // ll, t // ll
    ctx0 = (b * t) // ll

    def rb(bb, dd, c):
        is_ctx = c < nctx
        cc = jnp.where(dd == 0, c, nctx - 1 - c)
        cl = jnp.where(dd == 0, c - nctx, nlat - 1 - (c - nctx))
        return jnp.where(is_ctx, ctx0 + bb * nctx + cc, bb * nlat + cl)

    return pl.pallas_call(
        _mlstm_body,
        grid=(b, 2, nctx + nlat),
        in_specs=[
            pl.BlockSpec((ll, BRANCH_W), lambda bb, dd, c: (rb(bb, dd, c), 0)),
            pl.BlockSpec((ll, BRANCH_W), lambda bb, dd, c: (rb(bb, dd, c), 0)),
            pl.BlockSpec((ll, BRANCH_W), lambda bb, dd, c: (rb(bb, dd, c), SEG_MV)),
            pl.BlockSpec((ll, LANES), lambda bb, dd, c: (rb(bb, dd, c), 0)),
            pl.BlockSpec((1, LANES), lambda bb, dd, c: (0, 0)),
        ],
        out_specs=pl.BlockSpec((1, ll, BRANCH_W), lambda bb, dd, c: (dd, rb(bb, dd, c), 0)),
        out_shape=jax.ShapeDtypeStruct((2, r, BRANCH_W), BF16),
        scratch_shapes=[
            pltpu.VMEM((N_HEADS, HEAD_W, HEAD_W), F32),
            pltpu.VMEM((N_HEADS, 1, HEAD_W), F32),
            pltpu.VMEM((N_HEADS, 1, LANES), F32),
        ],
        compiler_params=_cparams("parallel", "parallel", "arbitrary"),
        name="mlstm",
    )(mq, mk, pa, pg, gate_bias)


def _merge_body(ya_ref, y0_ref, y1_ref, u_ref, h0_ref, h1_ref, mo_ref, ga_ref, gb_ref, gc_ref, x_ref, m_ref,
                d_ref, wglu_ref, wb_ref, wo_ref, wr_ref, wrt_ref,
                x1_ref, h2_ref, aff_ref, afft_ref):
    mod = m_ref[0]
    ys = d_ref[...] * u_ref[...].astype(F32) + y0_ref[...].astype(F32) + y1_ref[...].astype(F32)
    gl = jax.nn.gelu(ys)
    yb = gl * jax.nn.sigmoid(jnp.dot(gl.astype(BF16), wglu_ref[...], preferred_element_type=F32))
    yc = (h0_ref[0].astype(F32) + h1_ref[0].astype(F32)) * jax.nn.sigmoid(mo_ref[...].astype(F32))
    gate = lambda ref: jax.nn.sigmoid(ref[...].astype(F32))
    mixed = (gate(ga_ref) * jnp.dot(ya_ref[...], wb_ref[0], preferred_element_type=F32)
             + gate(gb_ref) * jnp.dot(yb.astype(BF16), wb_ref[1], preferred_element_type=F32)
             + gate(gc_ref) * jnp.dot(yc.astype(BF16), wb_ref[2], preferred_element_type=F32))
    out = jnp.dot(mixed.astype(BF16), wo_ref[...], preferred_element_type=F32)
    x1 = x_ref[...] + mod[2:3, :] * out
    x1_ref[...] = x1
    h2 = _modulated_norm(x1, mod, 3, 4)
    h2_ref[...] = h2.astype(BF16)
    logits = jnp.dot(h2, wr_ref[...], precision=HI, preferred_element_type=F32)
    lane = lax.broadcasted_iota(jnp.int32, logits.shape, 1)
    logits = jnp.where(lane < N_EXPERTS, logits, NEG_BIG)
    ex = jnp.exp(logits - jnp.max(logits, axis=1, keepdims=True))
    aff_ref[...] = ex / jnp.sum(ex, axis=1, keepdims=True)
    lt = lax.dot_general(wrt_ref[...], h2, (((1,), (1,)), ((), ())), precision=HI,
                         preferred_element_type=F32)
    et = jnp.exp(lt - jnp.max(lt, axis=0, keepdims=True))
    afft_ref[...] = et / jnp.sum(et, axis=0, keepdims=True)


def _merge(ya, y0, y1, pa, hm, xa, mod, s5_d, w_glu, w_branch, w_out, wr_pad, wr_t, dims):
    b, t, tc, d = dims
    r = xa.shape[0]
    tm = 256
    gseg = SEG_GATE * BRANCH_W // d
    rowblk = lambda width, col=0: pl.BlockSpec((tm, width), lambda i: (i, col))
    full2 = lambda shape: pl.BlockSpec(shape, lambda i: (0, 0))
    return pl.pallas_call(
        _merge_body,
        grid=(r // tm,),
        in_specs=[
            rowblk(BRANCH_W), rowblk(BRANCH_W), rowblk(BRANCH_W),
            rowblk(BRANCH_W, SEG_S5),
            pl.BlockSpec((1, tm, BRANCH_W), lambda i: (0, i, 0)),
            pl.BlockSpec((1, tm, BRANCH_W), lambda i: (1, i, 0)),
            rowblk(BRANCH_W, SEG_MO),
            rowblk(d, gseg), rowblk(d, gseg + 1), rowblk(d, gseg + 2),
            rowblk(d),
            pl.BlockSpec((1, N_MOD, d), lambda i: (_group_of_block(i, t // tm, b), 0, 0)),
            full2((1, BRANCH_W)),
            full2((BRANCH_W, BRANCH_W)),
            pl.BlockSpec((3, BRANCH_W, d), lambda i: (0, 0, 0)),
            full2((d, d)),
            full2((d, LANES)),
            full2((N_EXPERTS, d)),
        ],
        out_specs=[
            rowblk(d), rowblk(d), rowblk(LANES),
            pl.BlockSpec((N_EXPERTS, tm), lambda i: (0, i)),
        ],
        out_shape=[
            jax.ShapeDtypeStruct((r, d), F32),
            jax.ShapeDtypeStruct((r, d), BF16),
            jax.ShapeDtypeStruct((r, LANES), F32),
            jax.ShapeDtypeStruct((N_EXPERTS, r), F32),
        ],
        compiler_params=_cparams("parallel"),
        name="merge",
    )(ya, y0, y1, pa, hm, hm, pa, pa, pa, pa, xa, mod, s5_d, w_glu, w_branch, w_out, wr_pad, wr_t)


def _route_body(a_ref, tri_ref, low_ref, pos_ref, offs_ref, *, cap):
    a = a_ref[0]
    e, nb, _ = a.shape
    bits = pltpu.bitcast(a, jnp.int32)

    def count(mask):
        c = jnp.sum(jnp.where(mask, 1.0, 0.0), axis=2, keepdims=True)
        return jnp.sum(c, axis=1, keepdims=True)

    def step(i, thr):
        cand = thr | jnp.left_shift(jnp.int32(1), 30 - i)
        return jnp.where(count(bits >= cand) >= cap, cand, thr)

    thr = lax.fori_loop(0, 31, step, jnp.zeros((e, 1, 1), jnp.int32))
    gt = bits > thr
    eq = bits == thr
    need = cap - count(gt)

    tri = tri_ref[...]
    low = low_ref[...]

    def exclusive_prefix(x):
        x2 = x.reshape(e * nb, LANES).astype(BF16)
        incl = jnp.dot(x2, tri, preferred_element_type=F32)
        before = jnp.sum(jnp.dot(low, x2, preferred_element_type=F32), axis=1, keepdims=True)
        return (incl - x2.astype(F32) + before).reshape(e, nb, LANES), before.reshape(e, nb, 1)

    eq_rank, _ = exclusive_prefix(jnp.where(eq, 1.0, 0.0))
    sel = gt | (eq & (eq_rank < need))
    pos, before = exclusive_prefix(jnp.where(sel, 1.0, 0.0))
    pos_ref[0] = jnp.where(sel, pos, -1.0)
    offs_ref[0] = before.astype(jnp.int32)


def _route(aff3, cap):
    ns, e, nb, _ = aff3.shape
    i = jnp.arange(LANES)
    tri = (i[:, None] <= i[None, :]).astype(BF16)
    r = jnp.arange(e * nb)
    low = ((r[:, None] // nb == r[None, :] // nb) & (r[None, :] < r[:, None])).astype(BF16)
    return pl.pallas_call(
        functools.partial(_route_body, cap=cap),
        grid=(ns,),
        in_specs=[
            pl.BlockSpec((1, e, nb, LANES), lambda s: (s, 0, 0, 0)),
            pl.BlockSpec((LANES, LANES), lambda s: (0, 0)),
            pl.BlockSpec((e * nb, e * nb), lambda s: (0, 0)),
        ],
        out_specs=[
            pl.BlockSpec((1, e, nb, LANES), lambda s: (s, 0, 0, 0)),
            pl.BlockSpec((1, e, nb, 1), lambda s: (s, 0, 0, 0)),
        ],
        out_shape=[
            jax.ShapeDtypeStruct((ns, e, nb, LANES), F32),
            jax.ShapeDtypeStruct((ns, e, nb, 1), jnp.int32),
        ],
        compiler_params=_cparams("parallel"),
        name="route",
    )(aff3, tri, low)


SLOT_ALIGN = 16


def _dispatch_body(st_ref, h_ref, p_ref, o_ref, *, tb, win, nj, sub):
    ns, e, j = pl.program_id(0), pl.program_id(1), pl.program_id(2)

    @pl.when(j == 0)
    def _():
        o_ref[...] = jnp.zeros(o_ref.shape, o_ref.dtype)

    for k in range(sub):
        jb = j * sub + k
        start = st_ref[(ns * N_EXPERTS + e) * (nj + 1) + jb]
        end = st_ref[(ns * N_EXPERTS + e) * (nj + 1) + jb + 1]

        @pl.when(end > start)
        def _():
            base = pl.multiple_of((start // SLOT_ALIGN) * SLOT_ALIGN, SLOT_ALIGN)
            slot = (base + lax.broadcasted_iota(jnp.int32, (win, tb), 0)).astype(F32)
            onehot = jnp.where(slot == p_ref[0, :, k * tb:(k + 1) * tb], 1.0, 0.0).astype(BF16)
            rows = jnp.dot(onehot, h_ref[k * tb:(k + 1) * tb, :], preferred_element_type=F32)
            cur = o_ref[0, 0, pl.ds(base, win), :]
            o_ref[0, 0, pl.ds(base, win), :] = cur + rows.astype(o_ref.dtype)


def _dispatch(starts, h2, pos, row0, ns, n, cap, tb):
    d = h2.shape[1]
    nj = n // tb
    sub = min(4, nj)
    njs = nj // sub
    win = tb + SLOT_ALIGN
    capp = cap + win
    blk0 = row0 // (sub * tb)
    pos_rows = pos.reshape(ns * N_EXPERTS * njs, 1, sub * tb)
    grid_spec = pltpu.PrefetchScalarGridSpec(
        num_scalar_prefetch=1,
        grid=(ns, N_EXPERTS, njs),
        in_specs=[
            pl.BlockSpec((sub * tb, d), lambda s, e, j, st: (blk0 + s * njs + j, 0)),
            pl.BlockSpec((1, 1, sub * tb), lambda s, e, j, st: ((s * N_EXPERTS + e) * njs + j, 0, 0)),
        ],
        out_specs=pl.BlockSpec((1, 1, capp, d), lambda s, e, j, st: (s, e, 0, 0)),
    )
    return pl.pallas_call(
        functools.partial(_dispatch_body, tb=tb, win=win, nj=nj, sub=sub),
        grid_spec=grid_spec,
        out_shape=jax.ShapeDtypeStruct((ns, N_EXPERTS, capp, d), BF16),
        compiler_params=_cparams("parallel", "parallel", "arbitrary"),
        name="dispatch",
    )(starts, h2, pos_rows)


def _expert_body(x_ref, wg_ref, wu_ref, wd_ref, y_ref, *, fc):
    x = x_ref[0, 0]
    f = wg_ref.shape[2]
    acc = jnp.zeros((x.shape[0], wd_ref.shape[2]), F32)
    for f0 in range(0, f, fc):
        g = jnp.dot(x, wg_ref[0, :, f0:f0 + fc], preferred_element_type=F32)
        u = jnp.dot(x, wu_ref[0, :, f0:f0 + fc], preferred_element_type=F32)
        hid = (g * jax.nn.sigmoid(g) * u).astype(BF16)
        acc = acc + jnp.dot(hid, wd_ref[0, f0:f0 + fc, :], preferred_element_type=F32)
    y_ref[0, 0] = acc.astype(BF16)


def _experts(xs, w_gate, w_up, w_down, cap):
    ns, e, _, d = xs.shape
    f = w_gate.shape[2]
    ts = min(512, cap)
    return pl.pallas_call(
        functools.partial(_expert_body, fc=min(512, f)),
        grid=(e, ns, cap // ts),
        in_specs=[
            pl.BlockSpec((1, 1, ts, d), lambda ee, s, i: (s, ee, i, 0)),
            pl.BlockSpec((1, d, f), lambda ee, s, i: (ee, 0, 0)),
            pl.BlockSpec((1, d, f), lambda ee, s, i: (ee, 0, 0)),
            pl.BlockSpec((1, f, d), lambda ee, s, i: (ee, 0, 0)),
        ],
        out_specs=pl.BlockSpec((1, 1, ts, d), lambda ee, s, i: (s, ee, i, 0)),
        out_shape=jax.ShapeDtypeStruct((ns, e, cap, d), BF16),
        compiler_params=_cparams("parallel", "parallel", "parallel"),
        name="experts",
    )(xs, w_gate, w_up, w_down)


COMBINE_EXPERTS = 8


def _combine_body(st_ref, *refs, sb, nj, nwin):
    y_refs = refs[:COMBINE_EXPERTS * nwin]
    p_ref, aff_ref, x_ref, m_ref, o_ref = refs[COMBINE_EXPERTS * nwin:]
    ns, j, eg = pl.program_id(0), pl.program_id(1), pl.program_id(2)

    @pl.when(eg == 0)
    def _():
        o_ref[...] = jnp.zeros(o_ref.shape, F32)

    tb = o_ref.shape[0]
    lane = lax.broadcasted_iota(jnp.int32, (tb, LANES), 1)
    for k in range(COMBINE_EXPERTS):
        e = eg * COMBINE_EXPERTS + k
        start = st_ref[(ns * N_EXPERTS + e) * (nj + 1) + j]
        end = st_ref[(ns * N_EXPERTS + e) * (nj + 1) + j + 1]
        a = start // sb

        @pl.when(end > start)
        def _():
            mine = lane == e
            pos = jnp.sum(jnp.where(mine, p_ref[...], 0.0), axis=1, keepdims=True)
            aff = jnp.sum(jnp.where(mine, aff_ref[...], 0.0), axis=1, keepdims=True)
            for w in range(nwin):
                first = (a + w) * sb

                @pl.when(end > first)
                def _():
                    slot = (first + lax.broadcasted_iota(jnp.int32, (tb, sb), 1)).astype(F32)
                    onehot = jnp.where(pos == slot, 1.0, 0.0).astype(BF16)
                    got = jnp.dot(onehot, y_refs[k * nwin + w][0, 0], preferred_element_type=F32)
                    o_ref[...] += aff * got

    @pl.when(eg == N_EXPERTS // COMBINE_EXPERTS - 1)
    def _():
        o_ref[...] = x_ref[...] + m_ref[0, 5:6, :] * o_ref[...]


def _combine(starts, ys, pos_t, aff, x1, mod, row0, ns, n, cap, tb, mod_group):
    d = x1.shape[1]
    nj = n // tb
    span = min(tb, cap)
    sb = min(LANES, cap)
    nwin = span // sb + 1
    nsb = cap // sb
    blk0 = row0 // tb

    def window(k, w):
        def index(s, j, eg, st):
            e = eg * COMBINE_EXPERTS + k
            first = st[(s * N_EXPERTS + e) * (nj + 1) + j] // sb
            return (s, e, jnp.minimum(first + w, nsb - 1), 0)
        return pl.BlockSpec((1, 1, sb, d), index)

    row = lambda s, j, eg, st: (blk0 + s * nj + j, 0)
    windows = [window(k, w) for k in range(COMBINE_EXPERTS) for w in range(nwin)]
    grid_spec = pltpu.PrefetchScalarGridSpec(
        num_scalar_prefetch=1,
        grid=(ns, nj, N_EXPERTS // COMBINE_EXPERTS),
        in_specs=windows + [
            pl.BlockSpec((tb, LANES), row),
            pl.BlockSpec((tb, LANES), row),
            pl.BlockSpec((tb, d), row),
            pl.BlockSpec((1, N_MOD, d), lambda s, j, eg, st: (mod_group(s), 0, 0)),
        ],
        out_specs=pl.BlockSpec((tb, d), lambda s, j, eg, st: (s * nj + j, 0)),
    )
    return pl.pallas_call(
        functools.partial(_combine_body, sb=sb, nj=nj, nwin=nwin),
        grid_spec=grid_spec,
        out_shape=jax.ShapeDtypeStruct((ns * n, d), F32),
        compiler_params=_cparams("parallel", "parallel", "arbitrary"),
        name="combine",
    )(starts, *([ys] * len(windows)), pos_t, aff, x1, mod)


def _expert_choice(h2, aff, aff_t, x1, mod, w_gate, w_up, w_down, row0, ns, n, mod_group):
    r, d = h2.shape
    e = N_EXPERTS
    cap = CAPACITY_FACTOR * n // e
    tb = min(256, n)
    n_pad = max(n, 8 * LANES)
    a = aff_t[:, row0:row0 + ns * n].reshape(e, ns, n).transpose(1, 0, 2)
    if n_pad > n:
        a = jnp.concatenate([a, jnp.full((ns, e, n_pad - n), -1.0, F32)], axis=2)
    pos, offs = _route(a.reshape(ns, e, n_pad // LANES, LANES), cap)
    pos = pos.reshape(ns, e, n_pad)[:, :, :n]
    starts = offs.reshape(ns, e, n_pad // LANES)[:, :, :n // LANES:tb // LANES]
    starts = jnp.concatenate([starts, jnp.full((ns, e, 1), cap, jnp.int32)], axis=2).reshape(-1)
    xs = _dispatch(starts, h2, pos, row0, ns, n, cap, tb)
    ys = _experts(xs, w_gate, w_up, w_down, cap)
    pos_t = jnp.pad(pos.transpose(0, 2, 1).reshape(ns * n, e), ((0, 0), (0, LANES - e)), constant_values=-1.0)
    pos_t = jnp.pad(pos_t, ((row0, r - row0 - ns * n), (0, 0)))
    return _combine(starts, ys, pos_t, aff, x1, mod, row0, ns, n, cap, tb, mod_group)


def _rope_tables(b, t, tc):
    n_freq = DIFF_HEAD_DIM // 4
    inv_freq = ROPE_BASE ** (-jnp.arange(n_freq, dtype=F32) / n_freq)
    pos = jnp.arange(t)
    row = (pos // GRID_W).astype(F32)
    col = (pos % GRID_W).astype(F32)
    ang = jnp.concatenate([row[:, None] * inv_freq, col[:, None] * inv_freq], axis=-1)
    cos, sin = jnp.cos(ang), jnp.sin(ang)
    cos_seg = jnp.concatenate([cos, cos], axis=-1)
    sin_seg = jnp.concatenate([-sin, sin], axis=-1)
    cos_t = jnp.tile(cos_seg, (b, LANES // DIFF_HEAD_DIM))
    sin_t = jnp.tile(sin_seg, (b, LANES // DIFF_HEAD_DIM))
    cos_t = jnp.concatenate([cos_t, jnp.ones((b * tc, LANES), F32)])
    sin_t = jnp.concatenate([sin_t, jnp.zeros((b * tc, LANES), F32)])
    return cos_t, sin_t


def kernel(x, c, ctx, c_ctx, w_mod, b_mod, w_in, attn_q_gain, attn_k_gain, attn_lambda, attn_out_gain,
           s5_lam_re, s5_lam_im, s5_log_step, s5_b_re, s5_b_im, s5_c_re, s5_c_im, s5_d, s5_w_glu,
           mlstm_conv_w, mlstm_conv_b, mlstm_i_bias, mlstm_f_bias,
           w_branch, w_out, w_router, w_exp_gate, w_exp_up, w_exp_down):
    b, t, d = x.shape
    tc = ctx.shape[1]
    n_layers = w_mod.shape[0]
    dims = (b, t, tc, d)
    assert b + 1 <= 8 and t % 512 == 0 and tc % 256 == 0 and (b * tc) % 512 == 0

    xa = jnp.concatenate([x.reshape(b * t, d), ctx.reshape(b * tc, d)])
    cvec = jnp.zeros((8, d), F32).at[:b].set(c).at[b].set(c_ctx)
    mod_all = _mod_vectors(cvec, w_mod, b_mod).reshape(n_layers, 8, N_MOD, d)

    n_main = 8 * BRANCH_W
    wa = jnp.concatenate([w_in[:, :, :n_main], w_in[:, :, n_main + N_GATES:]], axis=2).astype(BF16)
    wg = jnp.pad(w_in[:, :, n_main:n_main + N_GATES], ((0, 0), (0, 0), (0, LANES - N_GATES))).astype(BF16)
    cos_t, sin_t = _rope_tables(b, t, tc)
    seg = jnp.arange(BRANCH_W) // DIFF_HEAD_DIM
    seg_ones = (seg[:, None] == seg[None, :]).astype(BF16)
    n_seg = BRANCH_W // DIFF_HEAD_DIM
    gq = jnp.tile(attn_q_gain, (1, n_seg))[:, None, :] * (DIFF_HEAD_DIM ** -0.5 * math.log2(math.e))
    gk = jnp.tile(attn_k_gain, (1, n_seg))[:, None, :]
    conv_w = jnp.pad(mlstm_conv_w, ((0, 0), (0, 8 - CONV_K), (0, 0)))
    gate_bias = jnp.stack([mlstm_i_bias, mlstm_f_bias], axis=2).reshape(n_layers, 1, N_GATES)
    gate_bias = jnp.pad(gate_bias, ((0, 0), (0, 0), (0, LANES - N_GATES)))
    wr_pad = jnp.pad(w_router, ((0, 0), (0, 0), (0, LANES - N_EXPERTS)))
    wr_t = jnp.swapaxes(w_router, 1, 2)
    n_chunks = (t + tc) // S5_CHUNK
    n_levels = _s5_levels(n_chunks)

    for l in range(n_layers):
        with_ctx = l != n_layers - 1
        lam_init = 0.8 - 0.6 * math.exp(-0.3 * l)
        mod = mod_all[l]
        pa, pg = _project(xa, mod, wa[l], wg[l], dims)
        qh, kh, mq, mk = _prepare(pa, cos_t, sin_t, gq[l], gk[l], seg_ones, conv_w[l],
                                  mlstm_conv_b[l][None, :], dims)
        og = attn_out_gain[l][None, :]
        bound = 1.01 * DIFF_HEAD_DIM * jnp.max(jnp.abs(gq[l])) * jnp.max(jnp.abs(gk[l]))
        ya_l = _attention(qh, kh, pa, attn_lambda[l], og, bound, lam_init, dims, ctx_queries=False)
        if with_ctx:
            ya_c = _attention(qh, kh, pa, attn_lambda[l], og, bound, lam_init, dims, ctx_queries=True)
        else:
            ya_c = jnp.zeros((b * tc, BRANCH_W), BF16)
        ya = jnp.concatenate([ya_l, ya_c])
        mats = _s5_matrices(s5_lam_re[l], s5_lam_im[l], s5_log_step[l], s5_b_re[l], s5_b_im[l],
                            s5_c_re[l], s5_c_im[l], n_levels)
        y0, y1 = _s5_mixer(pa, mats, dims)
        hm = _mlstm(mq, mk, pa, pg, gate_bias[l], dims)
        x1, h2, aff, aff_t = _merge(ya, y0, y1, pa, hm, xa, mod, s5_d[l][None, :],
                                    s5_w_glu[l].astype(BF16), w_branch[l].astype(BF16),
                                    w_out[l].astype(BF16), wr_pad[l], wr_t[l], dims)
        wge, wue, wde = (w_exp_gate[l].astype(BF16), w_exp_up[l].astype(BF16), w_exp_down[l].astype(BF16))
        x2_l = _expert_choice(h2, aff, aff_t, x1, mod, wge, wue, wde, 0, b, t, lambda s: s)
        if with_ctx:
            x2_c = _expert_choice(h2, aff, aff_t, x1, mod, wge, wue, wde, b * t, b, tc, lambda s: b)
        else:
            x2_c = x1[b * t:]
        xa = jnp.concatenate([x2_l, x2_c])
    return xa[:b * t].reshape(b, t, d)
```

```python
import functools
import math

import jax
import jax.numpy as jnp
from jax import lax
from jax.experimental import pallas as pl
from jax.experimental.pallas import tpu as pltpu

F32 = jnp.float32
BF16 = jnp.bfloat16
HI = lax.Precision.HIGHEST

N_MOD = 6
NORM_EPS = 1e-6
GRID_W = 64
ROPE_BASE = 10000.0
N_HEADS = 4
DIFF_HEAD_DIM = 64
HEAD_W = 128
BRANCH_W = 512
S5_GROUPS = 32
S5_GROUP = 16
S5_STATE = 64
S5_CHUNK = 16
S5_SLAB = 8
S5_VMEM_LIMIT = 58 * 1024 * 1024
MLSTM_CHUNK = 128
CONV_K = 5
N_GATES = 16
N_EXPERTS = 16
CAPACITY_FACTOR = 2
LANES = 128
VMEM_LIMIT = 52 * 1024 * 1024
NEG_BIG = -1e30

SEG_Q, SEG_K, SEG_V, SEG_S5, SEG_MQ, SEG_MK, SEG_MV, SEG_MO, SEG_GATE = range(9)
PA_WIDTH = 8 * BRANCH_W + 3 * 1024


def _cparams(*sem):
    return pltpu.CompilerParams(dimension_semantics=sem, vmem_limit_bytes=VMEM_LIMIT)


def _mod_body(c_ref, w_ref, b_ref, o_ref):
    cv = c_ref[...]
    s = cv * jax.nn.sigmoid(cv)
    o_ref[0] = jnp.dot(s, w_ref[0], precision=HI, preferred_element_type=F32) + b_ref[0]


def _mod_vectors(cvec, w_mod, b_mod):
    n_layers, d, n = w_mod.shape
    tn = n // 4
    return pl.pallas_call(
        _mod_body,
        grid=(n_layers, n // tn),
        in_specs=[
            pl.BlockSpec((8, d), lambda l, j: (0, 0)),
            pl.BlockSpec((1, d, tn), lambda l, j: (l, 0, j)),
            pl.BlockSpec((1, 1, tn), lambda l, j: (l, 0, j)),
        ],
        out_specs=pl.BlockSpec((1, 8, tn), lambda l, j: (l, 0, j)),
        out_shape=jax.ShapeDtypeStruct((n_layers, 8, n), F32),
        compiler_params=_cparams("parallel", "parallel"),
        name="mod_vectors",
    )(cvec, w_mod, b_mod.reshape(n_layers, 1, n))


def _modulated_norm(x, mod, i_shift, i_scale):
    ms = jnp.mean(x * x, axis=-1, keepdims=True)
    xn = x * lax.rsqrt(ms + NORM_EPS)
    return xn * (1.0 + mod[i_scale:i_scale + 1, :]) + mod[i_shift:i_shift + 1, :]


def _proj_body(x_ref, m_ref, w_ref, wg_ref, pa_ref, pg_ref, *, tn):
    hb = _modulated_norm(x_ref[...], m_ref[0], 0, 1).astype(BF16)
    pg_ref[...] = jnp.dot(hb, wg_ref[...], preferred_element_type=F32)
    for c0 in range(0, pa_ref.shape[1], tn):
        pa_ref[:, c0:c0 + tn] = jnp.dot(hb, w_ref[:, c0:c0 + tn], preferred_element_type=F32).astype(BF16)


def _group_of_block(i, blocks_per_sample, n_samples):
    return jnp.minimum(i // blocks_per_sample, n_samples)


def _project(xa, mod, wa, wg, dims):
    b, t, tc, d = dims
    r = xa.shape[0]
    tm = 512
    npa = wa.shape[1]
    once = pl.Buffered(1)
    return pl.pallas_call(
        functools.partial(_proj_body, tn=1024),
        grid=(r // tm,),
        in_specs=[
            pl.BlockSpec((tm, d), lambda i: (i, 0)),
            pl.BlockSpec((1, N_MOD, d), lambda i: (_group_of_block(i, t // tm, b), 0, 0)),
            pl.BlockSpec((d, npa), lambda i: (0, 0), pipeline_mode=once),
            pl.BlockSpec((d, LANES), lambda i: (0, 0), pipeline_mode=once),
        ],
        out_specs=[
            pl.BlockSpec((tm, npa), lambda i: (i, 0)),
            pl.BlockSpec((tm, LANES), lambda i: (i, 0)),
        ],
        out_shape=[
            jax.ShapeDtypeStruct((r, npa), BF16),
            jax.ShapeDtypeStruct((r, LANES), F32),
        ],
        compiler_params=_cparams("parallel"),
        name="in_proj",
    )(xa, mod, wa, wg)


def _qk_norm_rope(x_bf, gain, cosf, sinf, seg_ones, first_half):
    x = x_bf.astype(F32)
    x2 = x * x
    hi = x2.astype(BF16)
    lo = (x2 - hi.astype(F32)).astype(BF16)
    ss = (jnp.dot(hi, seg_ones, preferred_element_type=F32)
          + jnp.dot(lo, seg_ones, preferred_element_type=F32))
    xn = x * lax.rsqrt(ss * (1.0 / DIFF_HEAD_DIM) + NORM_EPS) * gain
    half = DIFF_HEAD_DIM // 2
    width = x.shape[1]
    nxt = pltpu.roll(xn, width - half, 1)
    prv = pltpu.roll(xn, half, 1)
    partner = jnp.where(first_half, nxt, prv)
    return xn * cosf + partner * sinf


def _short_conv_silu(prev_ref, cur_ref, next_ref, w, bias, at_start, at_end, out_scale):
    tp = cur_ref.shape[0]
    prev = prev_ref[...].astype(F32)[8:16]
    nxt = next_ref[...].astype(F32)[0:8]
    prev = jnp.where(at_start, 0.0, prev)
    nxt = jnp.where(at_end, 0.0, nxt)
    ext = jnp.concatenate([prev, cur_ref[...].astype(F32), nxt], axis=0)
    acc = bias
    for kk in range(CONV_K):
        off = 8 + kk - CONV_K // 2
        acc = acc + w[kk:kk + 1, :] * ext[off:off + tp]
    y = acc * jax.nn.sigmoid(acc)
    return y * out_scale


def _prep_body(q_ref, k_ref, mqp_ref, mq_ref, mqn_ref, mkp_ref, mk_ref, mkn_ref,
               cos_ref, sin_ref, gq_ref, gk_ref, so_ref, cw_ref, cb_ref,
               qo_ref, ko_ref, mqo_ref, mko_ref, *, b, t, tc, tp):
    cos4 = jnp.concatenate([cos_ref[...]] * 4, axis=1)
    sin4 = jnp.concatenate([sin_ref[...]] * 4, axis=1)
    lane = lax.broadcasted_iota(jnp.int32, (tp, BRANCH_W), 1)
    first_half = (lane % DIFF_HEAD_DIM) < (DIFF_HEAD_DIM // 2)
    seg_ones = so_ref[...]
    qo_ref[...] = _qk_norm_rope(q_ref[...], gq_ref[...], cos4, sin4, seg_ones, first_half).astype(BF16)
    ko_ref[...] = _qk_norm_rope(k_ref[...], gk_ref[...], cos4, sin4, seg_ones, first_half).astype(BF16)

    row0 = pl.program_id(0) * tp
    in_lat = row0 < b * t
    local = jnp.where(in_lat, row0 % t, (row0 - b * t) % tc)
    seq_len = jnp.where(in_lat, t, tc)
    at_start = local == 0
    at_end = local + tp == seq_len
    cw = cw_ref[...]
    cb = cb_ref[...]
    mqo_ref[...] = _short_conv_silu(mqp_ref, mq_ref, mqn_ref, cw[:, :BRANCH_W], cb[:, :BRANCH_W],
                                    at_start, at_end, 1.0).astype(BF16)
    mko_ref[...] = _short_conv_silu(mkp_ref, mk_ref, mkn_ref, cw[:, BRANCH_W:], cb[:, BRANCH_W:],
                                    at_start, at_end, HEAD_W ** -0.5).astype(BF16)


def _prepare(pa, cos_tab, sin_tab, gq, gk, seg_ones, conv_w, conv_b, dims):
    b, t, tc, d = dims
    r = pa.shape[0]
    tp = 256
    halo = 16
    hb = tp // halo
    last_halo = r // halo - 1

    def cur(seg):
        return pl.BlockSpec((tp, BRANCH_W), lambda i: (i, seg))

    def prev(seg):
        return pl.BlockSpec((halo, BRANCH_W), lambda i: (jnp.maximum(i * hb - 1, 0), seg))

    def nxt(seg):
        return pl.BlockSpec((halo, BRANCH_W), lambda i: (jnp.minimum((i + 1) * hb, last_halo), seg))

    full = lambda shape: pl.BlockSpec(shape, lambda i: (0, 0))
    out = jax.ShapeDtypeStruct((r, BRANCH_W), BF16)
    return pl.pallas_call(
        functools.partial(_prep_body, b=b, t=t, tc=tc, tp=tp),
        grid=(r // tp,),
        in_specs=[
            cur(SEG_Q), cur(SEG_K),
            prev(SEG_MQ), cur(SEG_MQ), nxt(SEG_MQ),
            prev(SEG_MK), cur(SEG_MK), nxt(SEG_MK),
            pl.BlockSpec((tp, LANES), lambda i: (i, 0)),
            pl.BlockSpec((tp, LANES), lambda i: (i, 0)),
            full((1, BRANCH_W)), full((1, BRANCH_W)),
            full((BRANCH_W, BRANCH_W)),
            full((8, 2 * BRANCH_W)), full((1, 2 * BRANCH_W)),
        ],
        out_specs=[pl.BlockSpec((tp, BRANCH_W), lambda i: (i, 0))] * 4,
        out_shape=[out, out, out, out],
        compiler_params=_cparams("parallel"),
        name="row_prep",
    )(pa, pa, pa, pa, pa, pa, pa, pa, cos_tab, sin_tab, gq, gk, seg_ones, conv_w, conv_b)


def _attn_body(lam_ref, og_ref, sh_ref, q_ref, k_ref, v_ref, kc_ref, vc_ref, o_ref,
               q0_ref, q1_ref, m_ref, l_ref, acc_ref, *, lam_init, has_ctx, nk, fixed_shift):
    kj = pl.program_id(3)

    def process(kb, vb):
        for mi, qr in enumerate((q0_ref, q1_ref)):
            s = jnp.dot(kb, qr[...], preferred_element_type=F32)
            if fixed_shift:
                p = jnp.exp2(s - sh_ref[0:1, 0:1])
                l_ref[mi] += jnp.sum(p, axis=0, keepdims=True)
                acc_ref[mi] += lax.dot_general(vb, p.astype(BF16), (((0,), (0,)), ((), ())),
                                               preferred_element_type=F32)
            else:
                m_old = m_ref[mi]
                m_new = jnp.maximum(m_old, jnp.max(s, axis=0, keepdims=True))
                alpha = jnp.exp2(m_old - m_new)
                p = jnp.exp2(s - m_new)
                l_ref[mi] = alpha * l_ref[mi] + jnp.sum(p, axis=0, keepdims=True)
                pv = lax.dot_general(vb, p.astype(BF16), (((0,), (0,)), ((), ())),
                                     preferred_element_type=F32)
                acc_ref[mi] = alpha * acc_ref[mi] + pv
                m_ref[mi] = m_new

    @pl.when(kj == 0)
    def _():
        qt = q_ref[...].astype(F32).T.astype(BF16)
        row = lax.broadcasted_iota(jnp.int32, qt.shape, 0)
        zero = jnp.zeros_like(qt)
        q0_ref[...] = jnp.where(row < DIFF_HEAD_DIM, qt, zero)
        q1_ref[...] = jnp.where(row >= DIFF_HEAD_DIM, qt, zero)
        m_ref[...] = jnp.full(m_ref.shape, NEG_BIG, F32)
        l_ref[...] = jnp.zeros(l_ref.shape, F32)
        acc_ref[...] = jnp.zeros(acc_ref.shape, F32)
        if has_ctx:
            process(kc_ref[...], vc_ref[...])

    process(k_ref[...], v_ref[...])

    @pl.when(kj == nk - 1)
    def _():
        lv = lam_ref[...]
        lam = (jnp.exp(jnp.sum(lv[0:1] * lv[1:2], keepdims=True))
               - jnp.exp(jnp.sum(lv[2:3] * lv[3:4], keepdims=True)) + lam_init)
        o = acc_ref[0] / l_ref[0] - lam * (acc_ref[1] / l_ref[1])
        ms = jnp.mean(o * o, axis=0, keepdims=True)
        o = o * lax.rsqrt(ms + NORM_EPS)
        o_ref[...] = (o.T * (og_ref[...] * (1.0 - lam_init))).astype(BF16)


MAX_FIXED_SHIFT = 48.0


def _attention(qh, kh, pa, lam_vecs, out_gain, score_bound, lam_init, dims, *, ctx_queries):
    shift = jnp.full((1, LANES), score_bound, F32)
    run = lambda fixed: _attention_call(qh, kh, pa, lam_vecs, out_gain, shift, lam_init, dims,
                                        ctx_queries=ctx_queries, fixed_shift=fixed)
    return lax.cond(score_bound <= MAX_FIXED_SHIFT, lambda: run(True), lambda: run(False))


def _attention_call(qh, kh, pa, lam_vecs, out_gain, shift, lam_init, dims, *, ctx_queries, fixed_shift):
    b, t, tc, d = dims
    v_col = SEG_V * (BRANCH_W // HEAD_W)
    ctx_blk0 = (b * t) // tc
    if ctx_queries:
        tq = tk = tc
        nq, nk = 1, 1
        q_row = lambda bb, qi: ctx_blk0 + bb
        k_row = lambda bb, kj: ctx_blk0 + bb
        n_rows = b * tc
        o_row = lambda bb, qi: bb
    else:
        tq = min(2048, t)
        tk = min(1024, t)
        nq, nk = t // tq, t // tk
        q_row = lambda bb, qi: bb * nq + qi
        k_row = lambda bb, kj: bb * nk + kj
        n_rows = b * t
        o_row = q_row
    body = functools.partial(_attn_body, lam_init=lam_init, has_ctx=not ctx_queries, nk=nk,
                             fixed_shift=fixed_shift)
    return pl.pallas_call(
        body,
        grid=(b, N_HEADS, nq, nk),
        in_specs=[
            pl.BlockSpec((4, DIFF_HEAD_DIM), lambda bb, h, qi, kj: (0, 0)),
            pl.BlockSpec((1, HEAD_W), lambda bb, h, qi, kj: (0, 0)),
            pl.BlockSpec((1, LANES), lambda bb, h, qi, kj: (0, 0)),
            pl.BlockSpec((tq, HEAD_W), lambda bb, h, qi, kj: (q_row(bb, qi), h)),
            pl.BlockSpec((tk, HEAD_W), lambda bb, h, qi, kj: (k_row(bb, kj), h)),
            pl.BlockSpec((tk, HEAD_W), lambda bb, h, qi, kj: (k_row(bb, kj), v_col + h)),
            pl.BlockSpec((tc, HEAD_W), lambda bb, h, qi, kj: (ctx_blk0 + bb, h)),
            pl.BlockSpec((tc, HEAD_W), lambda bb, h, qi, kj: (ctx_blk0 + bb, v_col + h)),
        ],
        out_specs=pl.BlockSpec((tq, HEAD_W), lambda bb, h, qi, kj: (o_row(bb, qi), h)),
        out_shape=jax.ShapeDtypeStruct((n_rows, BRANCH_W), BF16),
        scratch_shapes=[
            pltpu.VMEM((HEAD_W, tq), BF16),
            pltpu.VMEM((HEAD_W, tq), BF16),
            pltpu.VMEM((2, 1, tq), F32),
            pltpu.VMEM((2, 1, tq), F32),
            pltpu.VMEM((2, HEAD_W, tq), F32),
        ],
        compiler_params=_cparams("parallel", "parallel", "parallel", "arbitrary"),
        name=("diff_attn_ctx" if ctx_queries else "diff_attn") + ("_fixed" if fixed_shift else ""),
    )(lam_vecs, out_gain, shift, qh, kh, pa, kh, pa)


def _s5_matrices(lam_re, lam_im, log_step, b_re, b_im, c_re, c_im, n_levels):
    ll, hg, pp, gg = S5_CHUNK, S5_GROUP, S5_STATE, S5_GROUPS
    dt = jnp.exp(log_step)[:, :, None]
    lr, li = lam_re * dt, lam_im * dt

    def a_pow(tau):
        tau = tau.astype(F32)[:, None, None, None]
        mag = jnp.exp(lr * tau)
        return mag * jnp.cos(li * tau), mag * jnp.sin(li * tau)

    ar1, ai1 = a_pow(jnp.ones((1,)))
    nr, ni = ar1[0] - 1.0, ai1[0]
    den = lam_re * lam_re + lam_im * lam_im
    f_re = (nr * lam_re + ni * lam_im) / den
    f_im = (ni * lam_re - nr * lam_im) / den
    bb_re = f_re[..., None] * b_re - f_im[..., None] * b_im
    bb_im = f_re[..., None] * b_im + f_im[..., None] * b_re

    ar, ai = a_pow(jnp.arange(ll + 1))
    ca_re = c_re[None] * ar[:, :, :, None, :] - c_im[None] * ai[:, :, :, None, :]
    ca_im = c_re[None] * ai[:, :, :, None, :] + c_im[None] * ar[:, :, :, None, :]
    kk = (jnp.einsum('tdgop,dgph->tdgoh', ca_re, bb_re, precision=HI)
          - jnp.einsum('tdgop,dgph->tdgoh', ca_im, bb_im, precision=HI))
    ns, sl = gg // S5_SLAB, S5_SLAB
    n_state = sl * pp
    width = ll * LANES
    jj = jnp.arange(ll)
    grp = jnp.arange(sl)
    same_go = (grp[:, None] == jnp.arange(LANES)[None, :] // hg).astype(F32)
    same_gp = (grp[:, None] == jnp.arange(n_state)[None, :] // pp).astype(F32)

    kt = kk[:ll].transpose(1, 0, 2, 4, 3).reshape(2, ll, ns, sl, hg, hg)
    kt = kt.transpose(0, 2, 1, 4, 3, 5).reshape(2, ns, ll, hg, LANES)
    kbd = (kt[:, :, :, None] * same_go[None, None, None, :, None, :]).reshape(2, ns, ll, LANES, LANES)

    pw = jnp.stack([ll - 1 - jj, jj])
    sel = lambda a: jnp.stack([a[pw[0], 0], a[pw[1], 1]]).reshape(2, ll, ns, n_state).transpose(0, 2, 1, 3)
    s_re, s_im = sel(ar), sel(ai)
    lane_bb = lambda m: m.reshape(2, ns, sl, pp, hg).transpose(0, 1, 4, 2, 3).reshape(2, ns, hg, n_state)
    t_re, t_im = lane_bb(bb_re), lane_bb(bb_im)

    def in_slab(a, bmat, c, dmat, sign):
        v = a[:, :, :, None, :] * bmat[:, :, None, :, :] + sign * c[:, :, :, None, :] * dmat[:, :, None, :, :]
        v = v[:, :, :, None] * same_gp[None, None, None, :, None, :]
        return v.reshape(2, ns, width, n_state)

    po = jnp.stack([jj + 1, ll - jj])

    def out_slab_t(a):
        m = jnp.stack([a[po[0], 0], a[po[1], 1]])
        m = m.reshape(2, ll, ns, sl, hg, pp).transpose(0, 2, 1, 4, 3, 5).reshape(2, ns, ll, hg, n_state)
        v = m[:, :, :, None] * same_gp[None, None, None, :, None, :]
        return v.reshape(2, ns, width, n_state)

    lev = (ll * (2 ** jnp.arange(n_levels))).astype(F32)
    alr, ali = a_pow(lev)
    pad_lev = (-n_levels) % 8

    def lev_slab(a):
        a = a.transpose(1, 0, 2, 3).reshape(2, n_levels, ns, n_state).transpose(0, 2, 1, 3)
        return jnp.pad(a, ((0, 0), (0, 0), (0, pad_lev), (0, 0)))

    return dict(kbd=kbd.astype(BF16),
                in_re=in_slab(s_re, t_re, s_im, t_im, -1.0).astype(BF16),
                in_im=in_slab(s_re, t_im, s_im, t_re, 1.0).astype(BF16),
                out_re_t=out_slab_t(ca_re).astype(BF16), out_im_t=out_slab_t(-ca_im).astype(BF16),
                al_re=lev_slab(alr), al_im=lev_slab(ali))


def _s5_body(ul_ref, uc_ref, kbd_ref, inr_ref, ini_ref, outr_ref, outi_ref, alr_ref, ali_ref,
             yl_ref, yc_ref, sr_ref, si_ref, w_ref, *, nlat, nctx, pad, n_levels, rev):
    ll = S5_CHUNK
    nc = nlat + nctx

    @pl.when(pl.program_id(1) == 0)
    def _():
        zero = jnp.zeros((LANES, LANES), BF16)
        for j in range(ll):
            for i in range(ll):
                lag = j - i if rev else i - j
                w_ref[j * LANES:(j + 1) * LANES, i * LANES:(i + 1) * LANES] = (
                    kbd_ref[0, 0, lag] if lag >= 0 else zero)

    cat = lambda ref: jnp.concatenate([ref[j] for j in range(ll)], axis=1)
    ulat, uctx = cat(ul_ref), cat(uc_ref)
    u = jnp.concatenate([ulat, uctx] if rev else [uctx, ulat], axis=0)
    lat0, ctx0 = (0, nlat) if rev else (nctx, 0)
    lo = 0 if rev else pad
    zero0 = nc if rev else 0
    zeros = jnp.zeros((pad, sr_ref.shape[1]), F32)
    sr_ref[zero0:zero0 + pad, :] = zeros
    si_ref[zero0:zero0 + pad, :] = zeros
    sr_ref[lo:lo + nc, :] = jnp.dot(u, inr_ref[0, 0], preferred_element_type=F32)
    si_ref[lo:lo + nc, :] = jnp.dot(u, ini_ref[0, 0], preferred_element_type=F32)
    for lev in range(n_levels):
        dd = 1 << lev
        src = lo + dd if rev else lo - dd
        a_r = alr_ref[0, 0, lev:lev + 1, :]
        a_i = ali_ref[0, 0, lev:lev + 1, :]
        cr, ci = sr_ref[lo:lo + nc, :], si_ref[lo:lo + nc, :]
        pr, pi = sr_ref[src:src + nc, :], si_ref[src:src + nc, :]
        sr_ref[lo:lo + nc, :] = cr + a_r * pr - a_i * pi
        si_ref[lo:lo + nc, :] = ci + a_r * pi + a_i * pr
    ent = lo + 1 if rev else lo - 1
    er = sr_ref[ent:ent + nc, :].astype(BF16)
    ei = si_ref[ent:ent + nc, :].astype(BF16)
    nt = (((1,), (1,)), ((), ()))
    for ib in range(ll // 2):
        cols = slice(ib * 2 * LANES, (ib + 1) * 2 * LANES)
        y = (jnp.dot(u, w_ref[:, cols], preferred_element_type=F32)
             + lax.dot_general(er, outr_ref[0, 0, cols, :], nt, preferred_element_type=F32)
             + lax.dot_general(ei, outi_ref[0, 0, cols, :], nt, preferred_element_type=F32)).astype(BF16)
        for k in range(2):
            yl_ref[2 * ib + k] = y[lat0:lat0 + nlat, k * LANES:(k + 1) * LANES]
            yc_ref[2 * ib + k] = y[ctx0:ctx0 + nctx, k * LANES:(k + 1) * LANES]


def _s5_levels(nc):
    return max(1, (nc - 1).bit_length())


def _s5_scan(u3, mats, dims, *, rev):
    b, t, tc, d = dims
    ll = S5_CHUNK
    nlat, nctx = t // ll, tc // ll
    n_levels = _s5_levels(nlat + nctx)
    pad = max(8, 1 << (n_levels - 1))
    ns = S5_GROUPS // S5_SLAB
    n_state = S5_SLAB * S5_STATE
    width = ll * LANES
    lev_rows = mats["al_re"].shape[2]
    once = pl.Buffered(1)
    dd = int(rev)
    per_slab = lambda shape: pl.BlockSpec((1, 1) + shape, lambda s, bb: (dd, s) + (0,) * len(shape),
                                          pipeline_mode=once)
    return pl.pallas_call(
        functools.partial(_s5_body, nlat=nlat, nctx=nctx, pad=pad, n_levels=n_levels, rev=rev),
        grid=(ns, b),
        in_specs=[
            pl.BlockSpec((ll, nlat, LANES), lambda s, bb: (0, bb, s), pipeline_mode=once),
            pl.BlockSpec((ll, nctx, LANES), lambda s, bb: (0, b * t // tc + bb, s)),
            per_slab((ll, LANES, LANES)),
            per_slab((width, n_state)), per_slab((width, n_state)),
            per_slab((width, n_state)), per_slab((width, n_state)),
            per_slab((lev_rows, n_state)), per_slab((lev_rows, n_state)),
        ],
        out_specs=[
            pl.BlockSpec((ll, nlat, LANES), lambda s, bb: (0, bb, s)),
            pl.BlockSpec((ll, nctx, LANES), lambda s, bb: (0, bb, s)),
        ],
        out_shape=[
            jax.ShapeDtypeStruct((ll, b * nlat, BRANCH_W), BF16),
            jax.ShapeDtypeStruct((ll, b * nctx, BRANCH_W), BF16),
        ],
        scratch_shapes=[pltpu.VMEM((pad + nlat + nctx, n_state), F32)] * 2
        + [pltpu.VMEM((width, width), BF16)],
        compiler_params=pltpu.CompilerParams(dimension_semantics=("parallel", "arbitrary"),
                                             vmem_limit_bytes=S5_VMEM_LIMIT),
        name="s5_scan_bwd" if rev else "s5_scan_fwd",
    )(u3, u3, mats["kbd"], mats["in_re"], mats["in_im"], mats["out_re_t"], mats["out_im_t"],
      mats["al_re"], mats["al_im"])


def _s5_mixer(pa, mats, dims):
    r = pa.shape[0]
    ll = S5_CHUNK
    u = pa[:, SEG_S5 * BRANCH_W:(SEG_S5 + 1) * BRANCH_W]
    u3 = u.reshape(r // ll, ll, BRANCH_W).transpose(1, 0, 2)
    ys = []
    for dd in range(2):
        yl, yc = _s5_scan(u3, mats, dims, rev=bool(dd))
        ys.append(jnp.concatenate([yl, yc], axis=1).transpose(1, 0, 2).reshape(r, BRANCH_W))
    return ys


def _log_sigmoid(x):
    return -(jnp.maximum(-x, 0.0) + jnp.log1p(jnp.exp(-jnp.abs(x))))


def _mlstm_body(q_ref, k_ref, v_ref, g_ref, gb_ref, h_ref, c_ref, n_ref, m_ref):
    dd = pl.program_id(1)
    fwd = dd == 0

    @pl.when(pl.program_id(2) == 0)
    def _():
        c_ref[...] = jnp.zeros(c_ref.shape, F32)
        n_ref[...] = jnp.zeros(n_ref.shape, F32)
        m_ref[...] = jnp.zeros(m_ref.shape, F32)

    ll = MLSTM_CHUNK
    row = lax.broadcasted_iota(jnp.int32, (ll, ll), 0)
    col = lax.broadcasted_iota(jnp.int32, (ll, ll), 1)
    sign = jnp.where(fwd, 1, -1)
    before = (col - row) * sign >= 0
    before_f = jnp.where(before, 1.0, 0.0)
    after_f = jnp.where((row - col) * sign >= 0, 1.0, 0.0)

    g = g_ref[...] + gb_ref[...]
    lf = _log_sigmoid(g)
    g_t = g.T
    lf_t = lf.T
    cum_cols = jnp.dot(after_f, lf, precision=HI, preferred_element_type=F32)
    cum_rows = jnp.dot(lf_t, before_f, precision=HI, preferred_element_type=F32)
    nt = (((1,), (1,)), ((), ()))
    outs = []
    for h in range(N_HEADS):
        def pick_col(a, base):
            return jnp.where(fwd, a[:, base + h:base + h + 1], a[:, 8 + base + h:8 + base + h + 1])

        def pick_row(a, base):
            return jnp.where(fwd, a[base + h:base + h + 1, :], a[8 + base + h:8 + base + h + 1, :])

        ig_col, cum_col = pick_col(g, 0), pick_col(cum_cols, 4)
        ig_row, lf_row, cum_row = pick_row(g_t, 0), pick_row(lf_t, 4), pick_row(cum_rows, 4)
        gtot = jnp.sum(lf_row, axis=1, keepdims=True)

        hs = slice(h * HEAD_W, (h + 1) * HEAD_W)
        q = q_ref[:, hs]
        k = k_ref[:, hs]
        v_t = v_ref[:, hs].astype(F32).T.astype(BF16)
        c0 = c_ref[h]
        n0 = n_ref[h]
        m0 = m_ref[h][:, 0:1]

        dmat = jnp.where(before, cum_row + (ig_col - cum_col), NEG_BIG)
        inter = cum_row + m0
        m_t = jnp.maximum(inter, jnp.max(dmat, axis=0, keepdims=True))
        pm = jnp.exp(dmat - m_t)
        ei = jnp.exp(inter - m_t)
        kq = lax.dot_general(k, q, nt, preferred_element_type=F32)
        wq = pm * kq
        cq = lax.dot_general(c0.astype(BF16), q, nt, preferred_element_type=F32)
        num = jnp.dot(v_t, wq.astype(BF16), preferred_element_type=F32) + ei * cq
        nq = lax.dot_general(jnp.broadcast_to(n0, (8, HEAD_W)).astype(BF16), q, nt,
                             preferred_element_type=F32)[0:1]
        den = jnp.sum(wq, axis=0, keepdims=True) + ei * nq
        outs.append((num / jnp.maximum(jnp.abs(den), jnp.exp(-m_t))).T)

        w_row = gtot - cum_row + ig_row
        mw = jnp.max(w_row, axis=1, keepdims=True)
        ew = jnp.exp(w_row - mw)
        vw = (v_t.astype(F32) * ew).astype(BF16)
        kv = jnp.dot(vw, k, preferred_element_type=F32)
        ks = jnp.dot(jnp.broadcast_to(ew, (8, ll)).astype(BF16), k, preferred_element_type=F32)[0:1]
        m_new = jnp.maximum(gtot + m0, mw)
        a = jnp.exp(gtot + m0 - m_new)
        e = jnp.exp(mw - m_new)
        c_ref[h] = a * c0 + e * kv
        n_ref[h] = a * n0 + e * ks
        m_ref[h] = jnp.broadcast_to(m_new, (1, LANES))
    h_ref[0] = jnp.concatenate(outs, axis=1).astype(BF16)


def _mlstm(mq, mk, pa, pg, gate_bias, dims):
    b, t, tc, d = dims
    r = pa.shape[0]
    ll = MLSTM_CHUNK
    nctx, nlat = tc // ll, t // ll
    ctx0 = (b * t) // ll

    def rb(bb, dd, c):
        is_ctx = c < nctx
        cc = jnp.where(dd == 0, c, nctx - 1 - c)
        cl = jnp.where(dd == 0, c - nctx, nlat - 1 - (c - nctx))
        return jnp.where(is_ctx, ctx0 + bb * nctx + cc, bb * nlat + cl)

    return pl.pallas_call(
        _mlstm_body,
        grid=(b, 2, nctx + nlat),
        in_specs=[
            pl.BlockSpec((ll, BRANCH_W), lambda bb, dd, c: (rb(bb, dd, c), 0)),
            pl.BlockSpec((ll, BRANCH_W), lambda bb, dd, c: (rb(bb, dd, c), 0)),
            pl.BlockSpec((ll, BRANCH_W), lambda bb, dd, c: (rb(bb, dd, c), SEG_MV)),
            pl.BlockSpec((ll, LANES), lambda bb, dd, c: (rb(bb, dd, c), 0)),
            pl.BlockSpec((1, LANES), lambda bb, dd, c: (0, 0)),
        ],
        out_specs=pl.BlockSpec((1, ll, BRANCH_W), lambda bb, dd, c: (dd, rb(bb, dd, c), 0)),
        out_shape=jax.ShapeDtypeStruct((2, r, BRANCH_W), BF16),
        scratch_shapes=[
            pltpu.VMEM((N_HEADS, HEAD_W, HEAD_W), F32),
            pltpu.VMEM((N_HEADS, 1, HEAD_W), F32),
            pltpu.VMEM((N_HEADS, 1, LANES), F32),
        ],
        compiler_params=_cparams("parallel", "parallel", "arbitrary"),
        name="mlstm",
    )(mq, mk, pa, pg, gate_bias)


def _merge_body(ya_ref, y0_ref, y1_ref, u_ref, h0_ref, h1_ref, mo_ref, ga_ref, gb_ref, gc_ref, x_ref, m_ref,
                d_ref, wglu_ref, wb_ref, wo_ref, wr_ref, wrt_ref,
                x1_ref, h2_ref, aff_ref, afft_ref):
    mod = m_ref[0]
    ys = d_ref[...] * u_ref[...].astype(F32) + y0_ref[...].astype(F32) + y1_ref[...].astype(F32)
    gl = jax.nn.gelu(ys)
    yb = gl * jax.nn.sigmoid(jnp.dot(gl.astype(BF16), wglu_ref[...], preferred_element_type=F32))
    yc = (h0_ref[0].astype(F32) + h1_ref[0].astype(F32)) * jax.nn.sigmoid(mo_ref[...].astype(F32))
    gate = lambda ref: jax.nn.sigmoid(ref[...].astype(F32))
    mixed = (gate(ga_ref) * jnp.dot(ya_ref[...], wb_ref[0], preferred_element_type=F32)
             + gate(gb_ref) * jnp.dot(yb.astype(BF16), wb_ref[1], preferred_element_type=F32)
             + gate(gc_ref) * jnp.dot(yc.astype(BF16), wb_ref[2], preferred_element_type=F32))
    out = jnp.dot(mixed.astype(BF16), wo_ref[...], preferred_element_type=F32)
    x1 = x_ref[...] + mod[2:3, :] * out
    x1_ref[...] = x1
    h2 = _modulated_norm(x1, mod, 3, 4)
    h2_ref[...] = h2.astype(BF16)
    logits = jnp.dot(h2, wr_ref[...], precision=HI, preferred_element_type=F32)
    lane = lax.broadcasted_iota(jnp.int32, logits.shape, 1)
    logits = jnp.where(lane < N_EXPERTS, logits, NEG_BIG)
    ex = jnp.exp(logits - jnp.max(logits, axis=1, keepdims=True))
    aff_ref[...] = ex / jnp.sum(ex, axis=1, keepdims=True)
    lt = lax.dot_general(wrt_ref[...], h2, (((1,), (1,)), ((), ())), precision=HI,
                         preferred_element_type=F32)
    et = jnp.exp(lt - jnp.max(lt, axis=0, keepdims=True))
    afft_ref[...] = et / jnp.sum(et, axis=0, keepdims=True)


def _merge(ya, y0, y1, pa, hm, xa, mod, s5_d, w_glu, w_branch, w_out, wr_pad, wr_t, dims):
    b, t, tc, d = dims
    r = xa.shape[0]
    tm = 256
    gseg = SEG_GATE * BRANCH_W // d
    rowblk = lambda width, col=0: pl.BlockSpec((tm, width), lambda i: (i, col))
    full2 = lambda shape: pl.BlockSpec(shape, lambda i: (0, 0))
    return pl.pallas_call(
        _merge_body,
        grid=(r // tm,),
        in_specs=[
            rowblk(BRANCH_W), rowblk(BRANCH_W), rowblk(BRANCH_W),
            rowblk(BRANCH_W, SEG_S5),
            pl.BlockSpec((1, tm, BRANCH_W), lambda i: (0, i, 0)),
            pl.BlockSpec((1, tm, BRANCH_W), lambda i: (1, i, 0)),
            rowblk(BRANCH_W, SEG_MO),
            rowblk(d, gseg), rowblk(d, gseg + 1), rowblk(d, gseg + 2),
            rowblk(d),
            pl.BlockSpec((1, N_MOD, d), lambda i: (_group_of_block(i, t // tm, b), 0, 0)),
            full2((1, BRANCH_W)),
            full2((BRANCH_W, BRANCH_W)),
            pl.BlockSpec((3, BRANCH_W, d), lambda i: (0, 0, 0)),
            full2((d, d)),
            full2((d, LANES)),
            full2((N_EXPERTS, d)),
        ],
        out_specs=[
            rowblk(d), rowblk(d), rowblk(LANES),
            pl.BlockSpec((N_EXPERTS, tm), lambda i: (0, i)),
        ],
        out_shape=[
            jax.ShapeDtypeStruct((r, d), F32),
            jax.ShapeDtypeStruct((r, d), BF16),
            jax.ShapeDtypeStruct((r, LANES), F32),
            jax.ShapeDtypeStruct((N_EXPERTS, r), F32),
        ],
        compiler_params=_cparams("parallel"),
        name="merge",
    )(ya, y0, y1, pa, hm, hm, pa, pa, pa, pa, xa, mod, s5_d, w_glu, w_branch, w_out, wr_pad, wr_t)


def _route_body(a_ref, tri_ref, low_ref, pos_ref, offs_ref, *, cap):
    a = a_ref[0]
    e, nb, _ = a.shape
    bits = pltpu.bitcast(a, jnp.int32)

    def count(mask):
        c = jnp.sum(jnp.where(mask, 1.0, 0.0), axis=2, keepdims=True)
        return jnp.sum(c, axis=1, keepdims=True)

    def step(i, thr):
        cand = thr | jnp.left_shift(jnp.int32(1), 30 - i)
        return jnp.where(count(bits >= cand) >= cap, cand, thr)

    thr = lax.fori_loop(0, 31, step, jnp.zeros((e, 1, 1), jnp.int32))
    gt = bits > thr
    eq = bits == thr
    need = cap - count(gt)

    tri = tri_ref[...]
    low = low_ref[...]

    def exclusive_prefix(x):
        x2 = x.reshape(e * nb, LANES).astype(BF16)
        incl = jnp.dot(x2, tri, preferred_element_type=F32)
        before = jnp.sum(jnp.dot(low, x2, preferred_element_type=F32), axis=1, keepdims=True)
        return (incl - x2.astype(F32) + before).reshape(e, nb, LANES), before.reshape(e, nb, 1)

    eq_rank, _ = exclusive_prefix(jnp.where(eq, 1.0, 0.0))
    sel = gt | (eq & (eq_rank < need))
    pos, before = exclusive_prefix(jnp.where(sel, 1.0, 0.0))
    pos_ref[0] = jnp.where(sel, pos, -1.0)
    offs_ref[0] = before.astype(jnp.int32)


def _route(aff3, cap):
    ns, e, nb, _ = aff3.shape
    i = jnp.arange(LANES)
    tri = (i[:, None] <= i[None, :]).astype(BF16)
    r = jnp.arange(e * nb)
    low = ((r[:, None] // nb == r[None, :] // nb) & (r[None, :] < r[:, None])).astype(BF16)
    return pl.pallas_call(
        functools.partial(_route_body, cap=cap),
        grid=(ns,),
        in_specs=[
            pl.BlockSpec((1, e, nb, LANES), lambda s: (s, 0, 0, 0)),
            pl.BlockSpec((LANES, LANES), lambda s: (0, 0)),
            pl.BlockSpec((e * nb, e * nb), lambda s: (0, 0)),
        ],
        out_specs=[
            pl.BlockSpec((1, e, nb, LANES), lambda s: (s, 0, 0, 0)),
            pl.BlockSpec((1, e, nb, 1), lambda s: (s, 0, 0, 0)),
        ],
        out_shape=[
            jax.ShapeDtypeStruct((ns, e, nb, LANES), F32),
            jax.ShapeDtypeStruct((ns, e, nb, 1), jnp.int32),
        ],
        compiler_params=_cparams("parallel"),
        name="route",
    )(aff3, tri, low)


SLOT_ALIGN = 16


def _dispatch_body(st_ref, h_ref, a_ref, p_ref, o_ref, oa_ref, *, tb, win, nj, sub):
    ns, e, j = pl.program_id(0), pl.program_id(1), pl.program_id(2)

    @pl.when(j == 0)
    def _():
        o_ref[...] = jnp.zeros(o_ref.shape, o_ref.dtype)
        oa_ref[...] = jnp.zeros(oa_ref.shape, oa_ref.dtype)

    mine = lax.broadcasted_iota(jnp.int32, (tb, LANES), 1) == e
    for k in range(sub):
        jb = j * sub + k
        start = st_ref[(ns * N_EXPERTS + e) * (nj + 1) + jb]
        end = st_ref[(ns * N_EXPERTS + e) * (nj + 1) + jb + 1]

        @pl.when(end > start)
        def _():
            base = pl.multiple_of((start // SLOT_ALIGN) * SLOT_ALIGN, SLOT_ALIGN)
            slot = (base + lax.broadcasted_iota(jnp.int32, (win, tb), 0)).astype(F32)
            onehot = jnp.where(slot == p_ref[0, :, k * tb:(k + 1) * tb], 1.0, 0.0).astype(BF16)
            rows = jnp.dot(onehot, h_ref[k * tb:(k + 1) * tb, :], preferred_element_type=F32)
            cur = o_ref[0, 0, pl.ds(base, win), :]
            o_ref[0, 0, pl.ds(base, win), :] = cur + rows.astype(o_ref.dtype)
            aff = jnp.where(mine, a_ref[k * tb:(k + 1) * tb, :], 0.0)
            hi = aff.astype(BF16)
            rest = aff - hi.astype(F32)
            mid = rest.astype(BF16)
            low = (rest - mid.astype(F32)).astype(BF16)
            arows = (jnp.dot(onehot, hi, preferred_element_type=F32)
                     + jnp.dot(onehot, mid, preferred_element_type=F32)
                     + jnp.dot(onehot, low, preferred_element_type=F32))
            oa_ref[0, 0, pl.ds(base, win), :] = oa_ref[0, 0, pl.ds(base, win), :] + arows


def _dispatch(starts, h2, aff, pos, row0, ns, n, cap, tb):
    d = h2.shape[1]
    nj = n // tb
    sub = min(4, nj)
    njs = nj // sub
    win = tb + SLOT_ALIGN
    capp = cap + win
    blk0 = row0 // (sub * tb)
    pos_rows = pos.reshape(ns * N_EXPERTS * njs, 1, sub * tb)
    grid_spec = pltpu.PrefetchScalarGridSpec(
        num_scalar_prefetch=1,
        grid=(ns, N_EXPERTS, njs),
        in_specs=[
            pl.BlockSpec((sub * tb, d), lambda s, e, j, st: (blk0 + s * njs + j, 0)),
            pl.BlockSpec((sub * tb, LANES), lambda s, e, j, st: (blk0 + s * njs + j, 0)),
            pl.BlockSpec((1, 1, sub * tb), lambda s, e, j, st: ((s * N_EXPERTS + e) * njs + j, 0, 0)),
        ],
        out_specs=[
            pl.BlockSpec((1, 1, capp, d), lambda s, e, j, st: (s, e, 0, 0)),
            pl.BlockSpec((1, 1, capp, LANES), lambda s, e, j, st: (s, e, 0, 0)),
        ],
    )
    return pl.pallas_call(
        functools.partial(_dispatch_body, tb=tb, win=win, nj=nj, sub=sub),
        grid_spec=grid_spec,
        out_shape=[
            jax.ShapeDtypeStruct((ns, N_EXPERTS, capp, d), BF16),
            jax.ShapeDtypeStruct((ns, N_EXPERTS, capp, LANES), F32),
        ],
        compiler_params=_cparams("parallel", "parallel", "arbitrary"),
        name="dispatch",
    )(starts, h2, aff, pos_rows)


def _expert_body(x_ref, a_ref, wg_ref, wu_ref, wd_ref, y_ref, *, fc):
    x = x_ref[0, 0]
    f = wg_ref.shape[2]
    acc = jnp.zeros((x.shape[0], wd_ref.shape[2]), F32)
    for f0 in range(0, f, fc):
        g = jnp.dot(x, wg_ref[0, :, f0:f0 + fc], preferred_element_type=F32)
        u = jnp.dot(x, wu_ref[0, :, f0:f0 + fc], preferred_element_type=F32)
        hid = (g * jax.nn.sigmoid(g) * u).astype(BF16)
        acc = acc + jnp.dot(hid, wd_ref[0, f0:f0 + fc, :], preferred_element_type=F32)
    aff = jnp.sum(a_ref[0, 0], axis=1, keepdims=True)
    y_ref[0, 0] = (acc * aff).astype(BF16)


def _experts(xs, xa, w_gate, w_up, w_down, cap):
    ns, e, _, d = xs.shape
    f = w_gate.shape[2]
    ts = min(512, cap)
    return pl.pallas_call(
        functools.partial(_expert_body, fc=min(512, f)),
        grid=(e, ns, cap // ts),
        in_specs=[
            pl.BlockSpec((1, 1, ts, d), lambda ee, s, i: (s, ee, i, 0)),
            pl.BlockSpec((1, 1, ts, LANES), lambda ee, s, i: (s, ee, i, 0)),
            pl.BlockSpec((1, d, f), lambda ee, s, i: (ee, 0, 0)),
            pl.BlockSpec((1, d, f), lambda ee, s, i: (ee, 0, 0)),
            pl.BlockSpec((1, f, d), lambda ee, s, i: (ee, 0, 0)),
        ],
        out_specs=pl.BlockSpec((1, 1, ts, d), lambda ee, s, i: (s, ee, i, 0)),
        out_shape=jax.ShapeDtypeStruct((ns, e, cap, d), BF16),
        compiler_params=_cparams("parallel", "parallel", "parallel"),
        name="experts",
    )(xs, xa, w_gate, w_up, w_down)


COMBINE_EXPERTS = 8


def _combine_body(st_ref, *refs, sb, nj, nwin):
    y_refs = refs[:COMBINE_EXPERTS * nwin]
    p_ref, x_ref, m_ref, o_ref = refs[COMBINE_EXPERTS * nwin:]
    ns, j, eg = pl.program_id(0), pl.program_id(1), pl.program_id(2)

    @pl.when(eg == 0)
    def _():
        o_ref[...] = jnp.zeros(o_ref.shape, F32)

    tb = o_ref.shape[0]
    lane = lax.broadcasted_iota(jnp.int32, (tb, LANES), 1)
    slot_iota = lax.broadcasted_iota(jnp.int32, (tb, sb), 1)
    acc = None
    for k in range(COMBINE_EXPERTS):
        e = eg * COMBINE_EXPERTS + k
        start = st_ref[(ns * N_EXPERTS + e) * (nj + 1) + j]
        end = st_ref[(ns * N_EXPERTS + e) * (nj + 1) + j + 1]
        a = start // sb
        pos = jnp.sum(jnp.where(lane == e, p_ref[...], 0.0), axis=1, keepdims=True)
        onehot = lambda first: jnp.where(pos == (first + slot_iota).astype(F32), 1.0, 0.0).astype(BF16)
        got = jnp.dot(onehot(a * sb), y_refs[k * nwin][0, 0], preferred_element_type=F32)
        acc = got if acc is None else acc + got
        for w in range(1, nwin):
            first = (a + w) * sb

            @pl.when(end > first)
            def _():
                o_ref[...] += jnp.dot(onehot(first), y_refs[k * nwin + w][0, 0], preferred_element_type=F32)
    o_ref[...] += acc

    @pl.when(eg == N_EXPERTS // COMBINE_EXPERTS - 1)
    def _():
        o_ref[...] = x_ref[...] + m_ref[0, 5:6, :] * o_ref[...]


def _combine(starts, ys, pos_t, x1, mod, row0, ns, n, cap, tb, mod_group):
    d = x1.shape[1]
    nj = n // tb
    span = min(tb, cap)
    sb = min(LANES, cap)
    nwin = span // sb + 1
    nsb = cap // sb
    blk0 = row0 // tb

    def window(k, w):
        def index(s, j, eg, st):
            e = eg * COMBINE_EXPERTS + k
            first = st[(s * N_EXPERTS + e) * (nj + 1) + j] // sb
            return (s, e, jnp.minimum(first + w, nsb - 1), 0)
        return pl.BlockSpec((1, 1, sb, d), index)

    row = lambda s, j, eg, st: (blk0 + s * nj + j, 0)
    windows = [window(k, w) for k in range(COMBINE_EXPERTS) for w in range(nwin)]
    grid_spec = pltpu.PrefetchScalarGridSpec(
        num_scalar_prefetch=1,
        grid=(ns, nj, N_EXPERTS // COMBINE_EXPERTS),
        in_specs=windows + [
            pl.BlockSpec((tb, LANES), row),
            pl.BlockSpec((tb, d), row),
            pl.BlockSpec((1, N_MOD, d), lambda s, j, eg, st: (mod_group(s), 0, 0)),
        ],
        out_specs=pl.BlockSpec((tb, d), lambda s, j, eg, st: (s * nj + j, 0)),
    )
    return pl.pallas_call(
        functools.partial(_combine_body, sb=sb, nj=nj, nwin=nwin),
        grid_spec=grid_spec,
        out_shape=jax.ShapeDtypeStruct((ns * n, d), F32),
        compiler_params=_cparams("parallel", "parallel", "arbitrary"),
        name="combine",
    )(starts, *([ys] * len(windows)), pos_t, x1, mod)


def _expert_choice(h2, aff, aff_t, x1, mod, w_gate, w_up, w_down, row0, ns, n, mod_group):
    r, d = h2.shape
    e = N_EXPERTS
    cap = CAPACITY_FACTOR * n // e
    tb = min(256, n)
    n_pad = max(n, 8 * LANES)
    a = aff_t[:, row0:row0 + ns * n].reshape(e, ns, n).transpose(1, 0, 2)
    if n_pad > n:
        a = jnp.concatenate([a, jnp.full((ns, e, n_pad - n), -1.0, F32)], axis=2)
    pos, offs = _route(a.reshape(ns, e, n_pad // LANES, LANES), cap)
    pos = pos.reshape(ns, e, n_pad)[:, :, :n]
    starts = offs.reshape(ns, e, n_pad // LANES)[:, :, :n // LANES:tb // LANES]
    starts = jnp.concatenate([starts, jnp.full((ns, e, 1), cap, jnp.int32)], axis=2).reshape(-1)
    xs, xa = _dispatch(starts, h2, aff, pos, row0, ns, n, cap, tb)
    ys = _experts(xs, xa, w_gate, w_up, w_down, cap)
    pos_t = jnp.pad(pos.transpose(0, 2, 1).reshape(ns * n, e), ((0, 0), (0, LANES - e)), constant_values=-1.0)
    pos_t = jnp.pad(pos_t, ((row0, r - row0 - ns * n), (0, 0)))
    return _combine(starts, ys, pos_t, x1, mod, row0, ns, n, cap, tb, mod_group)


def _rope_tables(b, t, tc):
    n_freq = DIFF_HEAD_DIM // 4
    inv_freq = ROPE_BASE ** (-jnp.arange(n_freq, dtype=F32) / n_freq)
    pos = jnp.arange(t)
    row = (pos // GRID_W).astype(F32)
    col = (pos % GRID_W).astype(F32)
    ang = jnp.concatenate([row[:, None] * inv_freq, col[:, None] * inv_freq], axis=-1)
    cos, sin = jnp.cos(ang), jnp.sin(ang)
    cos_seg = jnp.concatenate([cos, cos], axis=-1)
    sin_seg = jnp.concatenate([-sin, sin], axis=-1)
    cos_t = jnp.tile(cos_seg, (b, LANES // DIFF_HEAD_DIM))
    sin_t = jnp.tile(sin_seg, (b, LANES // DIFF_HEAD_DIM))
    cos_t = jnp.concatenate([cos_t, jnp.ones((b * tc, LANES), F32)])
    sin_t = jnp.concatenate([sin_t, jnp.zeros((b * tc, LANES), F32)])
    return cos_t, sin_t


def kernel(x, c, ctx, c_ctx, w_mod, b_mod, w_in, attn_q_gain, attn_k_gain, attn_lambda, attn_out_gain,
           s5_lam_re, s5_lam_im, s5_log_step, s5_b_re, s5_b_im, s5_c_re, s5_c_im, s5_d, s5_w_glu,
           mlstm_conv_w, mlstm_conv_b, mlstm_i_bias, mlstm_f_bias,
           w_branch, w_out, w_router, w_exp_gate, w_exp_up, w_exp_down):
    b, t, d = x.shape
    tc = ctx.shape[1]
    n_layers = w_mod.shape[0]
    dims = (b, t, tc, d)
    assert b + 1 <= 8 and t % 512 == 0 and tc % 256 == 0 and (b * tc) % 512 == 0

    xa = jnp.concatenate([x.reshape(b * t, d), ctx.reshape(b * tc, d)])
    cvec = jnp.zeros((8, d), F32).at[:b].set(c).at[b].set(c_ctx)
    mod_all = _mod_vectors(cvec, w_mod, b_mod).reshape(n_layers, 8, N_MOD, d)

    n_main = 8 * BRANCH_W
    wa = jnp.concatenate([w_in[:, :, :n_main], w_in[:, :, n_main + N_GATES:]], axis=2).astype(BF16)
    wg = jnp.pad(w_in[:, :, n_main:n_main + N_GATES], ((0, 0), (0, 0), (0, LANES - N_GATES))).astype(BF16)
    cos_t, sin_t = _rope_tables(b, t, tc)
    seg = jnp.arange(BRANCH_W) // DIFF_HEAD_DIM
    seg_ones = (seg[:, None] == seg[None, :]).astype(BF16)
    n_seg = BRANCH_W // DIFF_HEAD_DIM
    gq = jnp.tile(attn_q_gain, (1, n_seg))[:, None, :] * (DIFF_HEAD_DIM ** -0.5 * math.log2(math.e))
    gk = jnp.tile(attn_k_gain, (1, n_seg))[:, None, :]
    conv_w = jnp.pad(mlstm_conv_w, ((0, 0), (0, 8 - CONV_K), (0, 0)))
    gate_bias = jnp.stack([mlstm_i_bias, mlstm_f_bias], axis=2).reshape(n_layers, 1, N_GATES)
    gate_bias = jnp.pad(gate_bias, ((0, 0), (0, 0), (0, LANES - N_GATES)))
    wr_pad = jnp.pad(w_router, ((0, 0), (0, 0), (0, LANES - N_EXPERTS)))
    wr_t = jnp.swapaxes(w_router, 1, 2)
    n_chunks = (t + tc) // S5_CHUNK
    n_levels = _s5_levels(n_chunks)

    for l in range(n_layers):
        with_ctx = l != n_layers - 1
        lam_init = 0.8 - 0.6 * math.exp(-0.3 * l)
        mod = mod_all[l]
        pa, pg = _project(xa, mod, wa[l], wg[l], dims)
        qh, kh, mq, mk = _prepare(pa, cos_t, sin_t, gq[l], gk[l], seg_ones, conv_w[l],
                                  mlstm_conv_b[l][None, :], dims)
        og = attn_out_gain[l][None, :]
        bound = 1.01 * DIFF_HEAD_DIM * jnp.max(jnp.abs(gq[l])) * jnp.max(jnp.abs(gk[l]))
        ya_l = _attention(qh, kh, pa, attn_lambda[l], og, bound, lam_init, dims, ctx_queries=False)
        if with_ctx:
            ya_c = _attention(qh, kh, pa, attn_lambda[l], og, bound, lam_init, dims, ctx_queries=True)
        else:
            ya_c = jnp.zeros((b * tc, BRANCH_W), BF16)
        ya = jnp.concatenate([ya_l, ya_c])
        mats = _s5_matrices(s5_lam_re[l], s5_lam_im[l], s5_log_step[l], s5_b_re[l], s5_b_im[l],
                            s5_c_re[l], s5_c_im[l], n_levels)
        y0, y1 = _s5_mixer(pa, mats, dims)
        hm = _mlstm(mq, mk, pa, pg, gate_bias[l], dims)
        x1, h2, aff, aff_t = _merge(ya, y0, y1, pa, hm, xa, mod, s5_d[l][None, :],
                                    s5_w_glu[l].astype(BF16), w_branch[l].astype(BF16),
                                    w_out[l].astype(BF16), wr_pad[l], wr_t[l], dims)
        wge, wue, wde = (w_exp_gate[l].astype(BF16), w_exp_up[l].astype(BF16), w_exp_down[l].astype(BF16))
        x2_l = _expert_choice(h2, aff, aff_t, x1, mod, wge, wue, wde, 0, b, t, lambda s: s)
        if with_ctx:
            x2_c = _expert_choice(h2, aff, aff_t, x1, mod, wge, wue, wde, b * t, b, tc, lambda s: b)
        else:
            x2_c = x1[b * t:]
        xa = jnp.concatenate([x2_l, x2_c])
    return xa[:b * t].reshape(b, t, d)
```

```python
import functools
import math

import jax
import jax.numpy as jnp
from jax import lax
from jax.experimental import pallas as pl
from jax.experimental.pallas import tpu as pltpu

F32 = jnp.float32
BF16 = jnp.bfloat16
HI = lax.Precision.HIGHEST

N_MOD = 6
NORM_EPS = 1e-6
GRID_W = 64
ROPE_BASE = 10000.0
N_HEADS = 4
DIFF_HEAD_DIM = 64
HEAD_W = 128
BRANCH_W = 512
S5_GROUPS = 32
S5_GROUP = 16
S5_STATE = 64
S5_CHUNK = 16
S5_SLAB = 8
S5_VMEM_LIMIT = 58 * 1024 * 1024
MLSTM_CHUNK = 128
CONV_K = 5
N_GATES = 16
N_EXPERTS = 16
CAPACITY_FACTOR = 2
LANES = 128
VMEM_LIMIT = 52 * 1024 * 1024
NEG_BIG = -1e30

SEG_Q, SEG_K, SEG_V, SEG_S5, SEG_MQ, SEG_MK, SEG_MV, SEG_MO, SEG_GATE = range(9)
PA_WIDTH = 8 * BRANCH_W + 3 * 1024


def _cparams(*sem):
    return pltpu.CompilerParams(dimension_semantics=sem, vmem_limit_bytes=VMEM_LIMIT)


def _mod_body(c_ref, w_ref, b_ref, o_ref):
    cv = c_ref[...]
    s = cv * jax.nn.sigmoid(cv)
    o_ref[0] = jnp.dot(s, w_ref[0], precision=HI, preferred_element_type=F32) + b_ref[0]


def _mod_vectors(cvec, w_mod, b_mod):
    n_layers, d, n = w_mod.shape
    tn = n // 4
    return pl.pallas_call(
        _mod_body,
        grid=(n_layers, n // tn),
        in_specs=[
            pl.BlockSpec((8, d), lambda l, j: (0, 0)),
            pl.BlockSpec((1, d, tn), lambda l, j: (l, 0, j)),
            pl.BlockSpec((1, 1, tn), lambda l, j: (l, 0, j)),
        ],
        out_specs=pl.BlockSpec((1, 8, tn), lambda l, j: (l, 0, j)),
        out_shape=jax.ShapeDtypeStruct((n_layers, 8, n), F32),
        compiler_params=_cparams("parallel", "parallel"),
        name="mod_vectors",
    )(cvec, w_mod, b_mod.reshape(n_layers, 1, n))


def _modulated_norm(x, mod, i_shift, i_scale):
    ms = jnp.mean(x * x, axis=-1, keepdims=True)
    xn = x * lax.rsqrt(ms + NORM_EPS)
    return xn * (1.0 + mod[i_scale:i_scale + 1, :]) + mod[i_shift:i_shift + 1, :]


def _proj_body(x_ref, m_ref, w_ref, wg_ref, pa_ref, pg_ref, *, tn):
    hb = _modulated_norm(x_ref[...], m_ref[0], 0, 1).astype(BF16)
    pg_ref[...] = jnp.dot(hb, wg_ref[...], preferred_element_type=F32)
    for c0 in range(0, pa_ref.shape[1], tn):
        pa_ref[:, c0:c0 + tn] = jnp.dot(hb, w_ref[:, c0:c0 + tn], preferred_element_type=F32).astype(BF16)


def _group_of_block(i, blocks_per_sample, n_samples):
    return jnp.minimum(i // blocks_per_sample, n_samples)


def _project(xa, mod, wa, wg, dims):
    b, t, tc, d = dims
    r = xa.shape[0]
    tm = 512
    npa = wa.shape[1]
    once = pl.Buffered(1)
    return pl.pallas_call(
        functools.partial(_proj_body, tn=1024),
        grid=(r // tm,),
        in_specs=[
            pl.BlockSpec((tm, d), lambda i: (i, 0)),
            pl.BlockSpec((1, N_MOD, d), lambda i: (_group_of_block(i, t // tm, b), 0, 0)),
            pl.BlockSpec((d, npa), lambda i: (0, 0), pipeline_mode=once),
            pl.BlockSpec((d, LANES), lambda i: (0, 0), pipeline_mode=once),
        ],
        out_specs=[
            pl.BlockSpec((tm, npa), lambda i: (i, 0)),
            pl.BlockSpec((tm, LANES), lambda i: (i, 0)),
        ],
        out_shape=[
            jax.ShapeDtypeStruct((r, npa), BF16),
            jax.ShapeDtypeStruct((r, LANES), F32),
        ],
        compiler_params=_cparams("parallel"),
        name="in_proj",
    )(xa, mod, wa, wg)


def _qk_norm_rope(x_bf, gain, cosf, sinf, seg_ones, first_half):
    x = x_bf.astype(F32)
    x2 = x * x
    hi = x2.astype(BF16)
    lo = (x2 - hi.astype(F32)).astype(BF16)
    ss = (jnp.dot(hi, seg_ones, preferred_element_type=F32)
          + jnp.dot(lo, seg_ones, preferred_element_type=F32))
    xn = x * lax.rsqrt(ss * (1.0 / DIFF_HEAD_DIM) + NORM_EPS) * gain
    half = DIFF_HEAD_DIM // 2
    width = x.shape[1]
    nxt = pltpu.roll(xn, width - half, 1)
    prv = pltpu.roll(xn, half, 1)
    partner = jnp.where(first_half, nxt, prv)
    return xn * cosf + partner * sinf


def _short_conv_silu(prev_ref, cur_ref, next_ref, w, bias, at_start, at_end, out_scale):
    tp = cur_ref.shape[0]
    prev = prev_ref[...].astype(F32)[8:16]
    nxt = next_ref[...].astype(F32)[0:8]
    prev = jnp.where(at_start, 0.0, prev)
    nxt = jnp.where(at_end, 0.0, nxt)
    ext = jnp.concatenate([prev, cur_ref[...].astype(F32), nxt], axis=0)
    acc = bias
    for kk in range(CONV_K):
        off = 8 + kk - CONV_K // 2
        acc = acc + w[kk:kk + 1, :] * ext[off:off + tp]
    y = acc * jax.nn.sigmoid(acc)
    return y * out_scale


def _prep_body(q_ref, k_ref, mqp_ref, mq_ref, mqn_ref, mkp_ref, mk_ref, mkn_ref,
               cos_ref, sin_ref, gq_ref, gk_ref, so_ref, cw_ref, cb_ref,
               qo_ref, ko_ref, mqo_ref, mko_ref, *, b, t, tc, tp):
    cos4 = jnp.concatenate([cos_ref[...]] * 4, axis=1)
    sin4 = jnp.concatenate([sin_ref[...]] * 4, axis=1)
    lane = lax.broadcasted_iota(jnp.int32, (tp, BRANCH_W), 1)
    first_half = (lane % DIFF_HEAD_DIM) < (DIFF_HEAD_DIM // 2)
    seg_ones = so_ref[...]
    qo_ref[...] = _qk_norm_rope(q_ref[...], gq_ref[...], cos4, sin4, seg_ones, first_half).astype(BF16)
    ko_ref[...] = _qk_norm_rope(k_ref[...], gk_ref[...], cos4, sin4, seg_ones, first_half).astype(BF16)

    row0 = pl.program_id(0) * tp
    in_lat = row0 < b * t
    local = jnp.where(in_lat, row0 % t, (row0 - b * t) % tc)
    seq_len = jnp.where(in_lat, t, tc)
    at_start = local == 0
    at_end = local + tp == seq_len
    cw = cw_ref[...]
    cb = cb_ref[...]
    mqo_ref[...] = _short_conv_silu(mqp_ref, mq_ref, mqn_ref, cw[:, :BRANCH_W], cb[:, :BRANCH_W],
                                    at_start, at_end, 1.0).astype(BF16)
    mko_ref[...] = _short_conv_silu(mkp_ref, mk_ref, mkn_ref, cw[:, BRANCH_W:], cb[:, BRANCH_W:],
                                    at_start, at_end, HEAD_W ** -0.5).astype(BF16)


def _prepare(pa, cos_tab, sin_tab, gq, gk, seg_ones, conv_w, conv_b, dims):
    b, t, tc, d = dims
    r = pa.shape[0]
    tp = 256
    halo = 16
    hb = tp // halo
    last_halo = r // halo - 1

    def cur(seg):
        return pl.BlockSpec((tp, BRANCH_W), lambda i: (i, seg))

    def prev(seg):
        return pl.BlockSpec((halo, BRANCH_W), lambda i: (jnp.maximum(i * hb - 1, 0), seg))

    def nxt(seg):
        return pl.BlockSpec((halo, BRANCH_W), lambda i: (jnp.minimum((i + 1) * hb, last_halo), seg))

    full = lambda shape: pl.BlockSpec(shape, lambda i: (0, 0))
    out = jax.ShapeDtypeStruct((r, BRANCH_W), BF16)
    return pl.pallas_call(
        functools.partial(_prep_body, b=b, t=t, tc=tc, tp=tp),
        grid=(r // tp,),
        in_specs=[
            cur(SEG_Q), cur(SEG_K),
            prev(SEG_MQ), cur(SEG_MQ), nxt(SEG_MQ),
            prev(SEG_MK), cur(SEG_MK), nxt(SEG_MK),
            pl.BlockSpec((tp, LANES), lambda i: (i, 0)),
            pl.BlockSpec((tp, LANES), lambda i: (i, 0)),
            full((1, BRANCH_W)), full((1, BRANCH_W)),
            full((BRANCH_W, BRANCH_W)),
            full((8, 2 * BRANCH_W)), full((1, 2 * BRANCH_W)),
        ],
        out_specs=[pl.BlockSpec((tp, BRANCH_W), lambda i: (i, 0))] * 4,
        out_shape=[out, out, out, out],
        compiler_params=_cparams("parallel"),
        name="row_prep",
    )(pa, pa, pa, pa, pa, pa, pa, pa, cos_tab, sin_tab, gq, gk, seg_ones, conv_w, conv_b)


ATTN_HEADS_PER_STEP = 2


def _attn_body(lam_ref, og_ref, sh_ref, q_ref, k_ref, v_ref, kc_ref, vc_ref, o_ref,
               qm_ref, m_ref, l_ref, acc_ref, *, lam_init, has_ctx, nk, fixed_shift):
    kj = pl.program_id(3)
    heads = range(ATTN_HEADS_PER_STEP)
    head_lanes = lambda hh: slice(hh * HEAD_W, (hh + 1) * HEAD_W)

    def process(k_all, v_all):
        for hh in heads:
            kb, vb = k_all[:, head_lanes(hh)], v_all[:, head_lanes(hh)]
            for mi in range(2):
                si = 2 * hh + mi
                s = jnp.dot(kb, qm_ref[si], preferred_element_type=F32)
                if fixed_shift:
                    p = jnp.exp2(s - sh_ref[0:1, 0:1])
                    l_ref[si] += jnp.sum(p, axis=0, keepdims=True)
                    acc_ref[si] += lax.dot_general(vb, p.astype(BF16), (((0,), (0,)), ((), ())),
                                                   preferred_element_type=F32)
                else:
                    m_old = m_ref[si]
                    m_new = jnp.maximum(m_old, jnp.max(s, axis=0, keepdims=True))
                    alpha = jnp.exp2(m_old - m_new)
                    p = jnp.exp2(s - m_new)
                    l_ref[si] = alpha * l_ref[si] + jnp.sum(p, axis=0, keepdims=True)
                    pv = lax.dot_general(vb, p.astype(BF16), (((0,), (0,)), ((), ())),
                                         preferred_element_type=F32)
                    acc_ref[si] = alpha * acc_ref[si] + pv
                    m_ref[si] = m_new

    @pl.when(kj == 0)
    def _():
        for hh in heads:
            qt = q_ref[:, head_lanes(hh)].astype(F32).T.astype(BF16)
            row = lax.broadcasted_iota(jnp.int32, qt.shape, 0)
            zero = jnp.zeros_like(qt)
            qm_ref[2 * hh] = jnp.where(row < DIFF_HEAD_DIM, qt, zero)
            qm_ref[2 * hh + 1] = jnp.where(row >= DIFF_HEAD_DIM, qt, zero)
        m_ref[...] = jnp.full(m_ref.shape, NEG_BIG, F32)
        l_ref[...] = jnp.zeros(l_ref.shape, F32)
        acc_ref[...] = jnp.zeros(acc_ref.shape, F32)
        if has_ctx:
            process(kc_ref[...], vc_ref[...])

    process(k_ref[...], v_ref[...])

    @pl.when(kj == nk - 1)
    def _():
        lv = lam_ref[...]
        lam = (jnp.exp(jnp.sum(lv[0:1] * lv[1:2], keepdims=True))
               - jnp.exp(jnp.sum(lv[2:3] * lv[3:4], keepdims=True)) + lam_init)
        for hh in heads:
            o = (acc_ref[2 * hh] / l_ref[2 * hh]
                 - lam * (acc_ref[2 * hh + 1] / l_ref[2 * hh + 1]))
            ms = jnp.mean(o * o, axis=0, keepdims=True)
            o = o * lax.rsqrt(ms + NORM_EPS)
            o_ref[:, head_lanes(hh)] = (o.T * (og_ref[...] * (1.0 - lam_init))).astype(BF16)


MAX_FIXED_SHIFT = 48.0


def _attention(qh, kh, pa, lam_vecs, out_gain, score_bound, lam_init, dims, *, ctx_queries):
    shift = jnp.full((1, LANES), score_bound, F32)
    run = lambda fixed: _attention_call(qh, kh, pa, lam_vecs, out_gain, shift, lam_init, dims,
                                        ctx_queries=ctx_queries, fixed_shift=fixed)
    return lax.cond(score_bound <= MAX_FIXED_SHIFT, lambda: run(True), lambda: run(False))


def _attention_call(qh, kh, pa, lam_vecs, out_gain, shift, lam_init, dims, *, ctx_queries, fixed_shift):
    b, t, tc, d = dims
    v_col = SEG_V * (BRANCH_W // HEAD_W)
    ctx_blk0 = (b * t) // tc
    if ctx_queries:
        tq = tk = tc
        nq, nk = 1, 1
        q_row = lambda bb, qi: ctx_blk0 + bb
        k_row = lambda bb, kj: ctx_blk0 + bb
        n_rows = b * tc
        o_row = lambda bb, qi: bb
    else:
        tq = min(2048, t)
        tk = min(1024, t)
        nq, nk = t // tq, t // tk
        q_row = lambda bb, qi: bb * nq + qi
        k_row = lambda bb, kj: bb * nk + kj
        n_rows = b * t
        o_row = q_row
    body = functools.partial(_attn_body, lam_init=lam_init, has_ctx=not ctx_queries, nk=nk,
                             fixed_shift=fixed_shift)
    hp = ATTN_HEADS_PER_STEP
    wb = hp * HEAD_W
    v_blk = SEG_V * BRANCH_W // wb
    return pl.pallas_call(
        body,
        grid=(b, N_HEADS // hp, nq, nk),
        in_specs=[
            pl.BlockSpec((4, DIFF_HEAD_DIM), lambda bb, h, qi, kj: (0, 0)),
            pl.BlockSpec((1, HEAD_W), lambda bb, h, qi, kj: (0, 0)),
            pl.BlockSpec((1, LANES), lambda bb, h, qi, kj: (0, 0)),
            pl.BlockSpec((tq, wb), lambda bb, h, qi, kj: (q_row(bb, qi), h)),
            pl.BlockSpec((tk, wb), lambda bb, h, qi, kj: (k_row(bb, kj), h)),
            pl.BlockSpec((tk, wb), lambda bb, h, qi, kj: (k_row(bb, kj), v_blk + h)),
            pl.BlockSpec((tc, wb), lambda bb, h, qi, kj: (ctx_blk0 + bb, h)),
            pl.BlockSpec((tc, wb), lambda bb, h, qi, kj: (ctx_blk0 + bb, v_blk + h)),
        ],
        out_specs=pl.BlockSpec((tq, wb), lambda bb, h, qi, kj: (o_row(bb, qi), h)),
        out_shape=jax.ShapeDtypeStruct((n_rows, BRANCH_W), BF16),
        scratch_shapes=[
            pltpu.VMEM((2 * hp, HEAD_W, tq), BF16),
            pltpu.VMEM((2 * hp, 1, tq), F32),
            pltpu.VMEM((2 * hp, 1, tq), F32),
            pltpu.VMEM((2 * hp, HEAD_W, tq), F32),
        ],
        compiler_params=_cparams("parallel", "parallel", "parallel", "arbitrary"),
        name=("diff_attn_ctx" if ctx_queries else "diff_attn") + ("_fixed" if fixed_shift else ""),
    )(lam_vecs, out_gain, shift, qh, kh, pa, kh, pa)


def _s5_matrices(lam_re, lam_im, log_step, b_re, b_im, c_re, c_im, n_levels):
    ll, hg, pp, gg = S5_CHUNK, S5_GROUP, S5_STATE, S5_GROUPS
    dt = jnp.exp(log_step)[:, :, None]
    lr, li = lam_re * dt, lam_im * dt

    def a_pow(tau):
        tau = tau.astype(F32)[:, None, None, None]
        mag = jnp.exp(lr * tau)
        return mag * jnp.cos(li * tau), mag * jnp.sin(li * tau)

    ar1, ai1 = a_pow(jnp.ones((1,)))
    nr, ni = ar1[0] - 1.0, ai1[0]
    den = lam_re * lam_re + lam_im * lam_im
    f_re = (nr * lam_re + ni * lam_im) / den
    f_im = (ni * lam_re - nr * lam_im) / den
    bb_re = f_re[..., None] * b_re - f_im[..., None] * b_im
    bb_im = f_re[..., None] * b_im + f_im[..., None] * b_re

    ar, ai = a_pow(jnp.arange(ll + 1))
    ca_re = c_re[None] * ar[:, :, :, None, :] - c_im[None] * ai[:, :, :, None, :]
    ca_im = c_re[None] * ai[:, :, :, None, :] + c_im[None] * ar[:, :, :, None, :]
    kk = (jnp.einsum('tdgop,dgph->tdgoh', ca_re, bb_re, precision=HI)
          - jnp.einsum('tdgop,dgph->tdgoh', ca_im, bb_im, precision=HI))
    ns, sl = gg // S5_SLAB, S5_SLAB
    n_state = sl * pp
    width = ll * LANES
    jj = jnp.arange(ll)
    grp = jnp.arange(sl)
    same_go = (grp[:, None] == jnp.arange(LANES)[None, :] // hg).astype(F32)
    same_gp = (grp[:, None] == jnp.arange(n_state)[None, :] // pp).astype(F32)

    kt = kk[:ll].transpose(1, 0, 2, 4, 3).reshape(2, ll, ns, sl, hg, hg)
    kt = kt.transpose(0, 2, 1, 4, 3, 5).reshape(2, ns, ll, hg, LANES)
    kbd = (kt[:, :, :, None] * same_go[None, None, None, :, None, :]).reshape(2, ns, ll, LANES, LANES)

    pw = jnp.stack([ll - 1 - jj, jj])
    sel = lambda a: jnp.stack([a[pw[0], 0], a[pw[1], 1]]).reshape(2, ll, ns, n_state).transpose(0, 2, 1, 3)
    s_re, s_im = sel(ar), sel(ai)
    lane_bb = lambda m: m.reshape(2, ns, sl, pp, hg).transpose(0, 1, 4, 2, 3).reshape(2, ns, hg, n_state)
    t_re, t_im = lane_bb(bb_re), lane_bb(bb_im)

    def in_slab(a, bmat, c, dmat, sign):
        v = a[:, :, :, None, :] * bmat[:, :, None, :, :] + sign * c[:, :, :, None, :] * dmat[:, :, None, :, :]
        v = v[:, :, :, None] * same_gp[None, None, None, :, None, :]
        return v.reshape(2, ns, width, n_state)

    po = jnp.stack([jj + 1, ll - jj])

    def out_slab_t(a):
        m = jnp.stack([a[po[0], 0], a[po[1], 1]])
        m = m.reshape(2, ll, ns, sl, hg, pp).transpose(0, 2, 1, 4, 3, 5).reshape(2, ns, ll, hg, n_state)
        v = m[:, :, :, None] * same_gp[None, None, None, :, None, :]
        return v.reshape(2, ns, width, n_state)

    lev = (ll * (2 ** jnp.arange(n_levels))).astype(F32)
    alr, ali = a_pow(lev)
    pad_lev = (-n_levels) % 8

    def lev_slab(a):
        a = a.transpose(1, 0, 2, 3).reshape(2, n_levels, ns, n_state).transpose(0, 2, 1, 3)
        return jnp.pad(a, ((0, 0), (0, 0), (0, pad_lev), (0, 0)))

    return dict(kbd=kbd.astype(BF16),
                in_re=in_slab(s_re, t_re, s_im, t_im, -1.0).astype(BF16),
                in_im=in_slab(s_re, t_im, s_im, t_re, 1.0).astype(BF16),
                out_re_t=out_slab_t(ca_re).astype(BF16), out_im_t=out_slab_t(-ca_im).astype(BF16),
                al_re=lev_slab(alr), al_im=lev_slab(ali))


def _s5_body(ul_ref, uc_ref, kbd_ref, inr_ref, ini_ref, outr_ref, outi_ref, alr_ref, ali_ref,
             yl_ref, yc_ref, sr_ref, si_ref, w_ref, *, nlat, nctx, pad, n_levels, rev):
    ll = S5_CHUNK
    nc = nlat + nctx

    @pl.when(pl.program_id(1) == 0)
    def _():
        zero = jnp.zeros((LANES, LANES), BF16)
        for j in range(ll):
            for i in range(ll):
                lag = j - i if rev else i - j
                w_ref[j * LANES:(j + 1) * LANES, i * LANES:(i + 1) * LANES] = (
                    kbd_ref[0, 0, lag] if lag >= 0 else zero)

    cat = lambda ref: jnp.concatenate([ref[j] for j in range(ll)], axis=1)
    ulat, uctx = cat(ul_ref), cat(uc_ref)
    u = jnp.concatenate([ulat, uctx] if rev else [uctx, ulat], axis=0)
    lat0, ctx0 = (0, nlat) if rev else (nctx, 0)
    lo = 0 if rev else pad
    zero0 = nc if rev else 0
    zeros = jnp.zeros((pad, sr_ref.shape[1]), F32)
    sr_ref[zero0:zero0 + pad, :] = zeros
    si_ref[zero0:zero0 + pad, :] = zeros
    sr_ref[lo:lo + nc, :] = jnp.dot(u, inr_ref[0, 0], preferred_element_type=F32)
    si_ref[lo:lo + nc, :] = jnp.dot(u, ini_ref[0, 0], preferred_element_type=F32)
    for lev in range(n_levels):
        dd = 1 << lev
        src = lo + dd if rev else lo - dd
        a_r = alr_ref[0, 0, lev:lev + 1, :]
        a_i = ali_ref[0, 0, lev:lev + 1, :]
        cr, ci = sr_ref[lo:lo + nc, :], si_ref[lo:lo + nc, :]
        pr, pi = sr_ref[src:src + nc, :], si_ref[src:src + nc, :]
        sr_ref[lo:lo + nc, :] = cr + a_r * pr - a_i * pi
        si_ref[lo:lo + nc, :] = ci + a_r * pi + a_i * pr
    ent = lo + 1 if rev else lo - 1
    er = sr_ref[ent:ent + nc, :].astype(BF16)
    ei = si_ref[ent:ent + nc, :].astype(BF16)
    nt = (((1,), (1,)), ((), ()))
    for ib in range(ll // 2):
        cols = slice(ib * 2 * LANES, (ib + 1) * 2 * LANES)
        y = (jnp.dot(u, w_ref[:, cols], preferred_element_type=F32)
             + lax.dot_general(er, outr_ref[0, 0, cols, :], nt, preferred_element_type=F32)
             + lax.dot_general(ei, outi_ref[0, 0, cols, :], nt, preferred_element_type=F32)).astype(BF16)
        for k in range(2):
            yl_ref[2 * ib + k] = y[lat0:lat0 + nlat, k * LANES:(k + 1) * LANES]
            yc_ref[2 * ib + k] = y[ctx0:ctx0 + nctx, k * LANES:(k + 1) * LANES]


def _s5_levels(nc):
    return max(1, (nc - 1).bit_length())


def _s5_scan(u3, mats, dims, *, rev):
    b, t, tc, d = dims
    ll = S5_CHUNK
    nlat, nctx = t // ll, tc // ll
    n_levels = _s5_levels(nlat + nctx)
    pad = max(8, 1 << (n_levels - 1))
    ns = S5_GROUPS // S5_SLAB
    n_state = S5_SLAB * S5_STATE
    width = ll * LANES
    lev_rows = mats["al_re"].shape[2]
    once = pl.Buffered(1)
    dd = int(rev)
    per_slab = lambda shape: pl.BlockSpec((1, 1) + shape, lambda s, bb: (dd, s) + (0,) * len(shape),
                                          pipeline_mode=once)
    return pl.pallas_call(
        functools.partial(_s5_body, nlat=nlat, nctx=nctx, pad=pad, n_levels=n_levels, rev=rev),
        grid=(ns, b),
        in_specs=[
            pl.BlockSpec((ll, nlat, LANES), lambda s, bb: (0, bb, s), pipeline_mode=once),
            pl.BlockSpec((ll, nctx, LANES), lambda s, bb: (0, b * t // tc + bb, s)),
            per_slab((ll, LANES, LANES)),
            per_slab((width, n_state)), per_slab((width, n_state)),
            per_slab((width, n_state)), per_slab((width, n_state)),
            per_slab((lev_rows, n_state)), per_slab((lev_rows, n_state)),
        ],
        out_specs=[
            pl.BlockSpec((ll, nlat, LANES), lambda s, bb: (0, bb, s)),
            pl.BlockSpec((ll, nctx, LANES), lambda s, bb: (0, bb, s)),
        ],
        out_shape=[
            jax.ShapeDtypeStruct((ll, b * nlat, BRANCH_W), BF16),
            jax.ShapeDtypeStruct((ll, b * nctx, BRANCH_W), BF16),
        ],
        scratch_shapes=[pltpu.VMEM((pad + nlat + nctx, n_state), F32)] * 2
        + [pltpu.VMEM((width, width), BF16)],
        compiler_params=pltpu.CompilerParams(dimension_semantics=("parallel", "arbitrary"),
                                             vmem_limit_bytes=S5_VMEM_LIMIT),
        name="s5_scan_bwd" if rev else "s5_scan_fwd",
    )(u3, u3, mats["kbd"], mats["in_re"], mats["in_im"], mats["out_re_t"], mats["out_im_t"],
      mats["al_re"], mats["al_im"])


def _s5_mixer(pa, mats, dims):
    r = pa.shape[0]
    ll = S5_CHUNK
    u = pa[:, SEG_S5 * BRANCH_W:(SEG_S5 + 1) * BRANCH_W]
    u3 = u.reshape(r // ll, ll, BRANCH_W).transpose(1, 0, 2)
    ys = []
    for dd in range(2):
        yl, yc = _s5_scan(u3, mats, dims, rev=bool(dd))
        ys.append(jnp.concatenate([yl, yc], axis=1).transpose(1, 0, 2).reshape(r, BRANCH_W))
    return ys


def _log_sigmoid(x):
    return -(jnp.maximum(-x, 0.0) + jnp.log1p(jnp.exp(-jnp.abs(x))))


def _mlstm_body(q_ref, k_ref, v_ref, g_ref, gb_ref, h_ref, c_ref, n_ref, m_ref):
    dd = pl.program_id(1)
    fwd = dd == 0

    @pl.when(pl.program_id(2) == 0)
    def _():
        c_ref[...] = jnp.zeros(c_ref.shape, F32)
        n_ref[...] = jnp.zeros(n_ref.shape, F32)
        m_ref[...] = jnp.zeros(m_ref.shape, F32)

    ll = MLSTM_CHUNK
    row = lax.broadcasted_iota(jnp.int32, (ll, ll), 0)
    col = lax.broadcasted_iota(jnp.int32, (ll, ll), 1)
    sign = jnp.where(fwd, 1, -1)
    before = (col - row) * sign >= 0
    before_f = jnp.where(before, 1.0, 0.0)
    after_f = jnp.where((row - col) * sign >= 0, 1.0, 0.0)

    g = g_ref[...] + gb_ref[...]
    lf = _log_sigmoid(g)
    g_t = g.T
    lf_t = lf.T
    cum_cols = jnp.dot(after_f, lf, precision=HI, preferred_element_type=F32)
    cum_rows = jnp.dot(lf_t, before_f, precision=HI, preferred_element_type=F32)
    nt = (((1,), (1,)), ((), ()))
    outs = []
    for h in range(N_HEADS):
        def pick_col(a, base):
            return jnp.where(fwd, a[:, base + h:base + h + 1], a[:, 8 + base + h:8 + base + h + 1])

        def pick_row(a, base):
            return jnp.where(fwd, a[base + h:base + h + 1, :], a[8 + base + h:8 + base + h + 1, :])

        ig_col, cum_col = pick_col(g, 0), pick_col(cum_cols, 4)
        ig_row, lf_row, cum_row = pick_row(g_t, 0), pick_row(lf_t, 4), pick_row(cum_rows, 4)
        gtot = jnp.sum(lf_row, axis=1, keepdims=True)

        hs = slice(h * HEAD_W, (h + 1) * HEAD_W)
        q = q_ref[:, hs]
        k = k_ref[:, hs]
        v_t = v_ref[:, hs].astype(F32).T.astype(BF16)
        c0 = c_ref[h]
        n0 = n_ref[h]
        m0 = m_ref[h][:, 0:1]

        dmat = jnp.where(before, cum_row + (ig_col - cum_col), NEG_BIG)
        inter = cum_row + m0
        m_t = jnp.maximum(inter, jnp.max(dmat, axis=0, keepdims=True))
        pm = jnp.exp(dmat - m_t)
        ei = jnp.exp(inter - m_t)
        kq = lax.dot_general(k, q, nt, preferred_element_type=F32)
        wq = pm * kq
        cq = lax.dot_general(c0.astype(BF16), q, nt, preferred_element_type=F32)
        num = jnp.dot(v_t, wq.astype(BF16), preferred_element_type=F32) + ei * cq
        nq = lax.dot_general(jnp.broadcast_to(n0, (8, HEAD_W)).astype(BF16), q, nt,
                             preferred_element_type=F32)[0:1]
        den = jnp.sum(wq, axis=0, keepdims=True) + ei * nq
        outs.append((num / jnp.maximum(jnp.abs(den), jnp.exp(-m_t))).T)

        w_row = gtot - cum_row + ig_row
        mw = jnp.max(w_row, axis=1, keepdims=True)
        ew = jnp.exp(w_row - mw)
        vw = (v_t.astype(F32) * ew).astype(BF16)
        kv = jnp.dot(vw, k, preferred_element_type=F32)
        ks = jnp.dot(jnp.broadcast_to(ew, (8, ll)).astype(BF16), k, preferred_element_type=F32)[0:1]
        m_new = jnp.maximum(gtot + m0, mw)
        a = jnp.exp(gtot + m0 - m_new)
        e = jnp.exp(mw - m_new)
        c_ref[h] = a * c0 + e * kv
        n_ref[h] = a * n0 + e * ks
        m_ref[h] = jnp.broadcast_to(m_new, (1, LANES))
    h_ref[0] = jnp.concatenate(outs, axis=1).astype(BF16)


def _mlstm(mq, mk, pa, pg, gate_bias, dims):
    b, t, tc, d = dims
    r = pa.shape[0]
    ll = MLSTM_CHUNK
    nctx, nlat = tc // ll, t // ll
    ctx0 = (b * t) // ll

    def rb(bb, dd, c):
        is_ctx = c < nctx
        cc = jnp.where(dd == 0, c, nctx - 1 - c)
        cl = jnp.where(dd == 0, c - nctx, nlat - 1 - (c - nctx))
        return jnp.where(is_ctx, ctx0 + bb * nctx + cc, bb * nlat + cl)

    return pl.pallas_call(
        _mlstm_body,
        grid=(b, 2, nctx + nlat),
        in_specs=[
            pl.BlockSpec((ll, BRANCH_W), lambda bb, dd, c: (rb(bb, dd, c), 0)),
            pl.BlockSpec((ll, BRANCH_W), lambda bb, dd, c: (rb(bb, dd, c), 0)),
            pl.BlockSpec((ll, BRANCH_W), lambda bb, dd, c: (rb(bb, dd, c), SEG_MV)),
            pl.BlockSpec((ll, LANES), lambda bb, dd, c: (rb(bb, dd, c), 0)),
            pl.BlockSpec((1, LANES), lambda bb, dd, c: (0, 0)),
        ],
        out_specs=pl.BlockSpec((1, ll, BRANCH_W), lambda bb, dd, c: (dd, rb(bb, dd, c), 0)),
        out_shape=jax.ShapeDtypeStruct((2, r, BRANCH_W), BF16),
        scratch_shapes=[
            pltpu.VMEM((N_HEADS, HEAD_W, HEAD_W), F32),
            pltpu.VMEM((N_HEADS, 1, HEAD_W), F32),
            pltpu.VMEM((N_HEADS, 1, LANES), F32),
        ],
        compiler_params=_cparams("parallel", "parallel", "arbitrary"),
        name="mlstm",
    )(mq, mk, pa, pg, gate_bias)


def _merge_body(ya_ref, y0_ref, y1_ref, u_ref, h0_ref, h1_ref, mo_ref, ga_ref, gb_ref, gc_ref, x_ref, m_ref,
                d_ref, wglu_ref, wb_ref, wo_ref, wr_ref, wrt_ref,
                x1_ref, h2_ref, aff_ref, afft_ref):
    mod = m_ref[0]
    ys = d_ref[...] * u_ref[...].astype(F32) + y0_ref[...].astype(F32) + y1_ref[...].astype(F32)
    gl = jax.nn.gelu(ys)
    yb = gl * jax.nn.sigmoid(jnp.dot(gl.astype(BF16), wglu_ref[...], preferred_element_type=F32))
    yc = (h0_ref[0].astype(F32) + h1_ref[0].astype(F32)) * jax.nn.sigmoid(mo_ref[...].astype(F32))
    gate = lambda ref: jax.nn.sigmoid(ref[...].astype(F32))
    mixed = (gate(ga_ref) * jnp.dot(ya_ref[...], wb_ref[0], preferred_element_type=F32)
             + gate(gb_ref) * jnp.dot(yb.astype(BF16), wb_ref[1], preferred_element_type=F32)
             + gate(gc_ref) * jnp.dot(yc.astype(BF16), wb_ref[2], preferred_element_type=F32))
    out = jnp.dot(mixed.astype(BF16), wo_ref[...], preferred_element_type=F32)
    x1 = x_ref[...] + mod[2:3, :] * out
    x1_ref[...] = x1
    h2 = _modulated_norm(x1, mod, 3, 4)
    h2_ref[...] = h2.astype(BF16)
    logits = jnp.dot(h2, wr_ref[...], precision=HI, preferred_element_type=F32)
    lane = lax.broadcasted_iota(jnp.int32, logits.shape, 1)
    logits = jnp.where(lane < N_EXPERTS, logits, NEG_BIG)
    ex = jnp.exp(logits - jnp.max(logits, axis=1, keepdims=True))
    aff_ref[...] = ex / jnp.sum(ex, axis=1, keepdims=True)
    lt = lax.dot_general(wrt_ref[...], h2, (((1,), (1,)), ((), ())), precision=HI,
                         preferred_element_type=F32)
    et = jnp.exp(lt - jnp.max(lt, axis=0, keepdims=True))
    afft_ref[...] = et / jnp.sum(et, axis=0, keepdims=True)


def _merge(ya, y0, y1, pa, hm, xa, mod, s5_d, w_glu, w_branch, w_out, wr_pad, wr_t, dims):
    b, t, tc, d = dims
    r = xa.shape[0]
    tm = 256
    gseg = SEG_GATE * BRANCH_W // d
    rowblk = lambda width, col=0: pl.BlockSpec((tm, width), lambda i: (i, col))
    full2 = lambda shape: pl.BlockSpec(shape, lambda i: (0, 0))
    return pl.pallas_call(
        _merge_body,
        grid=(r // tm,),
        in_specs=[
            rowblk(BRANCH_W), rowblk(BRANCH_W), rowblk(BRANCH_W),
            rowblk(BRANCH_W, SEG_S5),
            pl.BlockSpec((1, tm, BRANCH_W), lambda i: (0, i, 0)),
            pl.BlockSpec((1, tm, BRANCH_W), lambda i: (1, i, 0)),
            rowblk(BRANCH_W, SEG_MO),
            rowblk(d, gseg), rowblk(d, gseg + 1), rowblk(d, gseg + 2),
            rowblk(d),
            pl.BlockSpec((1, N_MOD, d), lambda i: (_group_of_block(i, t // tm, b), 0, 0)),
            full2((1, BRANCH_W)),
            full2((BRANCH_W, BRANCH_W)),
            pl.BlockSpec((3, BRANCH_W, d), lambda i: (0, 0, 0)),
            full2((d, d)),
            full2((d, LANES)),
            full2((N_EXPERTS, d)),
        ],
        out_specs=[
            rowblk(d), rowblk(d), rowblk(LANES),
            pl.BlockSpec((N_EXPERTS, tm), lambda i: (0, i)),
        ],
        out_shape=[
            jax.ShapeDtypeStruct((r, d), F32),
            jax.ShapeDtypeStruct((r, d), BF16),
            jax.ShapeDtypeStruct((r, LANES), F32),
            jax.ShapeDtypeStruct((N_EXPERTS, r), F32),
        ],
        compiler_params=_cparams("parallel"),
        name="merge",
    )(ya, y0, y1, pa, hm, hm, pa, pa, pa, pa, xa, mod, s5_d, w_glu, w_branch, w_out, wr_pad, wr_t)


def _route_body(a_ref, tri_ref, low_ref, pos_ref, offs_ref, *, cap):
    a = a_ref[0]
    e, nb, _ = a.shape
    bits = pltpu.bitcast(a, jnp.int32)

    def count(mask):
        c = jnp.sum(jnp.where(mask, 1.0, 0.0), axis=2, keepdims=True)
        return jnp.sum(c, axis=1, keepdims=True)

    def step(i, thr):
        cand = thr | jnp.left_shift(jnp.int32(1), 30 - i)
        return jnp.where(count(bits >= cand) >= cap, cand, thr)

    thr = lax.fori_loop(0, 31, step, jnp.zeros((e, 1, 1), jnp.int32))
    gt = bits > thr
    eq = bits == thr
    need = cap - count(gt)

    tri = tri_ref[...]
    low = low_ref[...]

    def exclusive_prefix(x):
        x2 = x.reshape(e * nb, LANES).astype(BF16)
        incl = jnp.dot(x2, tri, preferred_element_type=F32)
        before = jnp.sum(jnp.dot(low, x2, preferred_element_type=F32), axis=1, keepdims=True)
        return (incl - x2.astype(F32) + before).reshape(e, nb, LANES), before.reshape(e, nb, 1)

    eq_rank, _ = exclusive_prefix(jnp.where(eq, 1.0, 0.0))
    sel = gt | (eq & (eq_rank < need))
    pos, before = exclusive_prefix(jnp.where(sel, 1.0, 0.0))
    pos_ref[0] = jnp.where(sel, pos, -1.0)
    offs_ref[0] = before.astype(jnp.int32)


def _route(aff3, cap):
    ns, e, nb, _ = aff3.shape
    i = jnp.arange(LANES)
    tri = (i[:, None] <= i[None, :]).astype(BF16)
    r = jnp.arange(e * nb)
    low = ((r[:, None] // nb == r[None, :] // nb) & (r[None, :] < r[:, None])).astype(BF16)
    return pl.pallas_call(
        functools.partial(_route_body, cap=cap),
        grid=(ns,),
        in_specs=[
            pl.BlockSpec((1, e, nb, LANES), lambda s: (s, 0, 0, 0)),
            pl.BlockSpec((LANES, LANES), lambda s: (0, 0)),
            pl.BlockSpec((e * nb, e * nb), lambda s: (0, 0)),
        ],
        out_specs=[
            pl.BlockSpec((1, e, nb, LANES), lambda s: (s, 0, 0, 0)),
            pl.BlockSpec((1, e, nb, 1), lambda s: (s, 0, 0, 0)),
        ],
        out_shape=[
            jax.ShapeDtypeStruct((ns, e, nb, LANES), F32),
            jax.ShapeDtypeStruct((ns, e, nb, 1), jnp.int32),
        ],
        compiler_params=_cparams("parallel"),
        name="route",
    )(aff3, tri, low)


SLOT_ALIGN = 16


DISPATCH_EXPERTS = 2


def _dispatch_body(st_ref, h_ref, a_ref, p_ref, o_ref, oa_ref, *, tb, win, nj, sub):
    ns, eg, j = pl.program_id(0), pl.program_id(1), pl.program_id(2)

    @pl.when(j == 0)
    def _():
        o_ref[...] = jnp.zeros(o_ref.shape, o_ref.dtype)
        oa_ref[...] = jnp.zeros(oa_ref.shape, oa_ref.dtype)

    lane = lax.broadcasted_iota(jnp.int32, (tb, LANES), 1)
    for q in range(DISPATCH_EXPERTS):
        e = eg * DISPATCH_EXPERTS + q
        for k in range(sub):
            jb = j * sub + k
            start = st_ref[(ns * N_EXPERTS + e) * (nj + 1) + jb]
            end = st_ref[(ns * N_EXPERTS + e) * (nj + 1) + jb + 1]

            @pl.when(end > start)
            def _():
                base = pl.multiple_of((start // SLOT_ALIGN) * SLOT_ALIGN, SLOT_ALIGN)
                slot = (base + lax.broadcasted_iota(jnp.int32, (win, tb), 0)).astype(F32)
                pos = p_ref[0, q, 0, :, k * tb:(k + 1) * tb]
                onehot = jnp.where(slot == pos, 1.0, 0.0).astype(BF16)
                rows = jnp.dot(onehot, h_ref[k * tb:(k + 1) * tb, :], preferred_element_type=F32)
                cur = o_ref[0, q, pl.ds(base, win), :]
                o_ref[0, q, pl.ds(base, win), :] = cur + rows.astype(o_ref.dtype)
                aff = jnp.where(lane == e, a_ref[k * tb:(k + 1) * tb, :], 0.0)
                hi = aff.astype(BF16)
                rest = aff - hi.astype(F32)
                mid = rest.astype(BF16)
                low = (rest - mid.astype(F32)).astype(BF16)
                arows = (jnp.dot(onehot, hi, preferred_element_type=F32)
                         + jnp.dot(onehot, mid, preferred_element_type=F32)
                         + jnp.dot(onehot, low, preferred_element_type=F32))
                oa_ref[0, q, pl.ds(base, win), :] = oa_ref[0, q, pl.ds(base, win), :] + arows


def _dispatch(starts, h2, aff, pos, row0, ns, n, cap, tb):
    d = h2.shape[1]
    nj = n // tb
    sub = min(4, nj)
    njs = nj // sub
    win = tb + SLOT_ALIGN
    capp = cap + win
    blk0 = row0 // (sub * tb)
    de = DISPATCH_EXPERTS
    pos_rows = pos.reshape(ns, N_EXPERTS, njs, 1, sub * tb)
    grid_spec = pltpu.PrefetchScalarGridSpec(
        num_scalar_prefetch=1,
        grid=(ns, N_EXPERTS // de, njs),
        in_specs=[
            pl.BlockSpec((sub * tb, d), lambda s, e, j, st: (blk0 + s * njs + j, 0)),
            pl.BlockSpec((sub * tb, LANES), lambda s, e, j, st: (blk0 + s * njs + j, 0)),
            pl.BlockSpec((1, de, 1, 1, sub * tb), lambda s, e, j, st: (s, e, j, 0, 0)),
        ],
        out_specs=[
            pl.BlockSpec((1, de, capp, d), lambda s, e, j, st: (s, e, 0, 0)),
            pl.BlockSpec((1, de, capp, LANES), lambda s, e, j, st: (s, e, 0, 0)),
        ],
    )
    return pl.pallas_call(
        functools.partial(_dispatch_body, tb=tb, win=win, nj=nj, sub=sub),
        grid_spec=grid_spec,
        out_shape=[
            jax.ShapeDtypeStruct((ns, N_EXPERTS, capp, d), BF16),
            jax.ShapeDtypeStruct((ns, N_EXPERTS, capp, LANES), F32),
        ],
        compiler_params=_cparams("parallel", "parallel", "arbitrary"),
        name="dispatch",
    )(starts, h2, aff, pos_rows)


def _expert_body(x_ref, a_ref, wg_ref, wu_ref, wd_ref, y_ref, *, fc):
    x = x_ref[0, 0]
    f = wg_ref.shape[2]
    acc = jnp.zeros((x.shape[0], wd_ref.shape[2]), F32)
    for f0 in range(0, f, fc):
        g = jnp.dot(x, wg_ref[0, :, f0:f0 + fc], preferred_element_type=F32)
        u = jnp.dot(x, wu_ref[0, :, f0:f0 + fc], preferred_element_type=F32)
        hid = (g * jax.nn.sigmoid(g) * u).astype(BF16)
        acc = acc + jnp.dot(hid, wd_ref[0, f0:f0 + fc, :], preferred_element_type=F32)
    aff = jnp.sum(a_ref[0, 0], axis=1, keepdims=True)
    y_ref[0, 0] = (acc * aff).astype(BF16)


def _experts(xs, xa, w_gate, w_up, w_down, cap):
    ns, e, _, d = xs.shape
    f = w_gate.shape[2]
    ts = min(512, cap)
    return pl.pallas_call(
        functools.partial(_expert_body, fc=min(512, f)),
        grid=(e, ns, cap // ts),
        in_specs=[
            pl.BlockSpec((1, 1, ts, d), lambda ee, s, i: (s, ee, i, 0)),
            pl.BlockSpec((1, 1, ts, LANES), lambda ee, s, i: (s, ee, i, 0)),
            pl.BlockSpec((1, d, f), lambda ee, s, i: (ee, 0, 0)),
            pl.BlockSpec((1, d, f), lambda ee, s, i: (ee, 0, 0)),
            pl.BlockSpec((1, f, d), lambda ee, s, i: (ee, 0, 0)),
        ],
        out_specs=pl.BlockSpec((1, 1, ts, d), lambda ee, s, i: (s, ee, i, 0)),
        out_shape=jax.ShapeDtypeStruct((ns, e, cap, d), BF16),
        compiler_params=_cparams("parallel", "parallel", "parallel"),
        name="experts",
    )(xs, xa, w_gate, w_up, w_down)


COMBINE_EXPERTS = 8


def _combine_body(st_ref, *refs, sb, nj, nwin):
    y_refs = refs[:COMBINE_EXPERTS * nwin]
    p_ref, x_ref, m_ref, o_ref = refs[COMBINE_EXPERTS * nwin:]
    ns, j, eg = pl.program_id(0), pl.program_id(1), pl.program_id(2)

    @pl.when(eg == 0)
    def _():
        o_ref[...] = jnp.zeros(o_ref.shape, F32)

    tb = o_ref.shape[0]
    lane = lax.broadcasted_iota(jnp.int32, (tb, LANES), 1)
    slot_iota = lax.broadcasted_iota(jnp.int32, (tb, sb), 1)
    acc = None
    for k in range(COMBINE_EXPERTS):
        e = eg * COMBINE_EXPERTS + k
        start = st_ref[(ns * N_EXPERTS + e) * (nj + 1) + j]
        end = st_ref[(ns * N_EXPERTS + e) * (nj + 1) + j + 1]
        a = start // sb
        pos = jnp.sum(jnp.where(lane == e, p_ref[...], 0.0), axis=1, keepdims=True)
        onehot = lambda first: jnp.where(pos == (first + slot_iota).astype(F32), 1.0, 0.0).astype(BF16)
        got = jnp.dot(onehot(a * sb), y_refs[k * nwin][0, 0], preferred_element_type=F32)
        acc = got if acc is None else acc + got
        for w in range(1, nwin):
            first = (a + w) * sb

            @pl.when(end > first)
            def _():
                o_ref[...] += jnp.dot(onehot(first), y_refs[k * nwin + w][0, 0], preferred_element_type=F32)
    o_ref[...] += acc

    @pl.when(eg == N_EXPERTS // COMBINE_EXPERTS - 1)
    def _():
        o_ref[...] = x_ref[...] + m_ref[0, 5:6, :] * o_ref[...]


def _combine(starts, ys, pos_t, x1, mod, row0, ns, n, cap, tb, mod_group):
    d = x1.shape[1]
    nj = n // tb
    span = min(tb, cap)
    sb = min(LANES, cap)
    nwin = span // sb + 1
    nsb = cap // sb
    blk0 = row0 // tb

    def window(k, w):
        def index(s, j, eg, st):
            e = eg * COMBINE_EXPERTS + k
            start = st[(s * N_EXPERTS + e) * (nj + 1) + j]
            end = st[(s * N_EXPERTS + e) * (nj + 1) + j + 1]
            blk = start // sb + w
            if w == 0:
                return (s, e, jnp.minimum(blk, nsb - 1), 0)
            used = end > blk * sb
            return (jnp.where(used, s, 0), jnp.where(used, e, 0), jnp.where(used, blk, 0), 0)
        return pl.BlockSpec((1, 1, sb, d), index)

    row = lambda s, j, eg, st: (blk0 + s * nj + j, 0)
    windows = [window(k, w) for k in range(COMBINE_EXPERTS) for w in range(nwin)]
    grid_spec = pltpu.PrefetchScalarGridSpec(
        num_scalar_prefetch=1,
        grid=(ns, nj, N_EXPERTS // COMBINE_EXPERTS),
        in_specs=windows + [
            pl.BlockSpec((tb, LANES), row),
            pl.BlockSpec((tb, d), row),
            pl.BlockSpec((1, N_MOD, d), lambda s, j, eg, st: (mod_group(s), 0, 0)),
        ],
        out_specs=pl.BlockSpec((tb, d), lambda s, j, eg, st: (s * nj + j, 0)),
    )
    return pl.pallas_call(
        functools.partial(_combine_body, sb=sb, nj=nj, nwin=nwin),
        grid_spec=grid_spec,
        out_shape=jax.ShapeDtypeStruct((ns * n, d), F32),
        compiler_params=_cparams("parallel", "parallel", "arbitrary"),
        name="combine",
    )(starts, *([ys] * len(windows)), pos_t, x1, mod)


def _expert_choice(h2, aff, aff_t, x1, mod, w_gate, w_up, w_down, row0, ns, n, mod_group):
    r, d = h2.shape
    e = N_EXPERTS
    cap = CAPACITY_FACTOR * n // e
    tb = min(256, n)
    n_pad = max(n, 8 * LANES)
    a = aff_t[:, row0:row0 + ns * n].reshape(e, ns, n).transpose(1, 0, 2)
    if n_pad > n:
        a = jnp.concatenate([a, jnp.full((ns, e, n_pad - n), -1.0, F32)], axis=2)
    pos, offs = _route(a.reshape(ns, e, n_pad // LANES, LANES), cap)
    pos = pos.reshape(ns, e, n_pad)[:, :, :n]
    starts = offs.reshape(ns, e, n_pad // LANES)[:, :, :n // LANES:tb // LANES]
    starts = jnp.concatenate([starts, jnp.full((ns, e, 1), cap, jnp.int32)], axis=2).reshape(-1)
    xs, xa = _dispatch(starts, h2, aff, pos, row0, ns, n, cap, tb)
    ys = _experts(xs, xa, w_gate, w_up, w_down, cap)
    pos_t = jnp.pad(pos.transpose(0, 2, 1).reshape(ns * n, e), ((0, 0), (0, LANES - e)), constant_values=-1.0)
    pos_t = jnp.pad(pos_t, ((row0, r - row0 - ns * n), (0, 0)))
    return _combine(starts, ys, pos_t, x1, mod, row0, ns, n, cap, tb, mod_group)


def _rope_tables(b, t, tc):
    n_freq = DIFF_HEAD_DIM // 4
    inv_freq = ROPE_BASE ** (-jnp.arange(n_freq, dtype=F32) / n_freq)
    pos = jnp.arange(t)
    row = (pos // GRID_W).astype(F32)
    col = (pos % GRID_W).astype(F32)
    ang = jnp.concatenate([row[:, None] * inv_freq, col[:, None] * inv_freq], axis=-1)
    cos, sin = jnp.cos(ang), jnp.sin(ang)
    cos_seg = jnp.concatenate([cos, cos], axis=-1)
    sin_seg = jnp.concatenate([-sin, sin], axis=-1)
    cos_t = jnp.tile(cos_seg, (b, LANES // DIFF_HEAD_DIM))
    sin_t = jnp.tile(sin_seg, (b, LANES // DIFF_HEAD_DIM))
    cos_t = jnp.concatenate([cos_t, jnp.ones((b * tc, LANES), F32)])
    sin_t = jnp.concatenate([sin_t, jnp.zeros((b * tc, LANES), F32)])
    return cos_t, sin_t


def kernel(x, c, ctx, c_ctx, w_mod, b_mod, w_in, attn_q_gain, attn_k_gain, attn_lambda, attn_out_gain,
           s5_lam_re, s5_lam_im, s5_log_step, s5_b_re, s5_b_im, s5_c_re, s5_c_im, s5_d, s5_w_glu,
           mlstm_conv_w, mlstm_conv_b, mlstm_i_bias, mlstm_f_bias,
           w_branch, w_out, w_router, w_exp_gate, w_exp_up, w_exp_down):
    b, t, d = x.shape
    tc = ctx.shape[1]
    n_layers = w_mod.shape[0]
    dims = (b, t, tc, d)
    assert b + 1 <= 8 and t % 512 == 0 and tc % 256 == 0 and (b * tc) % 512 == 0

    xa = jnp.concatenate([x.reshape(b * t, d), ctx.reshape(b * tc, d)])
    cvec = jnp.zeros((8, d), F32).at[:b].set(c).at[b].set(c_ctx)
    mod_all = _mod_vectors(cvec, w_mod, b_mod).reshape(n_layers, 8, N_MOD, d)

    n_main = 8 * BRANCH_W
    wa = jnp.concatenate([w_in[:, :, :n_main], w_in[:, :, n_main + N_GATES:]], axis=2).astype(BF16)
    wg = jnp.pad(w_in[:, :, n_main:n_main + N_GATES], ((0, 0), (0, 0), (0, LANES - N_GATES))).astype(BF16)
    cos_t, sin_t = _rope_tables(b, t, tc)
    seg = jnp.arange(BRANCH_W) // DIFF_HEAD_DIM
    seg_ones = (seg[:, None] == seg[None, :]).astype(BF16)
    n_seg = BRANCH_W // DIFF_HEAD_DIM
    gq = jnp.tile(attn_q_gain, (1, n_seg))[:, None, :] * (DIFF_HEAD_DIM ** -0.5 * math.log2(math.e))
    gk = jnp.tile(attn_k_gain, (1, n_seg))[:, None, :]
    conv_w = jnp.pad(mlstm_conv_w, ((0, 0), (0, 8 - CONV_K), (0, 0)))
    gate_bias = jnp.stack([mlstm_i_bias, mlstm_f_bias], axis=2).reshape(n_layers, 1, N_GATES)
    gate_bias = jnp.pad(gate_bias, ((0, 0), (0, 0), (0, LANES - N_GATES)))
    wr_pad = jnp.pad(w_router, ((0, 0), (0, 0), (0, LANES - N_EXPERTS)))
    wr_t = jnp.swapaxes(w_router, 1, 2)
    n_chunks = (t + tc) // S5_CHUNK
    n_levels = _s5_levels(n_chunks)

    for l in range(n_layers):
        with_ctx = l != n_layers - 1
        lam_init = 0.8 - 0.6 * math.exp(-0.3 * l)
        mod = mod_all[l]
        pa, pg = _project(xa, mod, wa[l], wg[l], dims)
        qh, kh, mq, mk = _prepare(pa, cos_t, sin_t, gq[l], gk[l], seg_ones, conv_w[l],
                                  mlstm_conv_b[l][None, :], dims)
        og = attn_out_gain[l][None, :]
        bound = 1.01 * DIFF_HEAD_DIM * jnp.max(jnp.abs(gq[l])) * jnp.max(jnp.abs(gk[l]))
        ya_l = _attention(qh, kh, pa, attn_lambda[l], og, bound, lam_init, dims, ctx_queries=False)
        if with_ctx:
            ya_c = _attention(qh, kh, pa, attn_lambda[l], og, bound, lam_init, dims, ctx_queries=True)
        else:
            ya_c = jnp.zeros((b * tc, BRANCH_W), BF16)
        ya = jnp.concatenate([ya_l, ya_c])
        mats = _s5_matrices(s5_lam_re[l], s5_lam_im[l], s5_log_step[l], s5_b_re[l], s5_b_im[l],
                            s5_c_re[l], s5_c_im[l], n_levels)
        y0, y1 = _s5_mixer(pa, mats, dims)
        hm = _mlstm(mq, mk, pa, pg, gate_bias[l], dims)
        x1, h2, aff, aff_t = _merge(ya, y0, y1, pa, hm, xa, mod, s5_d[l][None, :],
                                    s5_w_glu[l].astype(BF16), w_branch[l].astype(BF16),
                                    w_out[l].astype(BF16), wr_pad[l], wr_t[l], dims)
        wge, wue, wde = (w_exp_gate[l].astype(BF16), w_exp_up[l].astype(BF16), w_exp_down[l].astype(BF16))
        x2_l = _expert_choice(h2, aff, aff_t, x1, mod, wge, wue, wde, 0, b, t, lambda s: s)
        if with_ctx:
            x2_c = _expert_choice(h2, aff, aff_t, x1, mod, wge, wue, wde, b * t, b, tc, lambda s: b)
        else:
            x2_c = x1[b * t:]
        xa = jnp.concatenate([x2_l, x2_c])
    return xa[:b * t].reshape(b, t, d)
```

```python
import functools
import math

import jax
import jax.numpy as jnp
from jax import lax
from jax.experimental import pallas as pl
from jax.experimental.pallas import tpu as pltpu

F32 = jnp.float32
BF16 = jnp.bfloat16
HI = lax.Precision.HIGHEST

N_MOD = 6
NORM_EPS = 1e-6
GRID_W = 64
ROPE_BASE = 10000.0
N_HEADS = 4
DIFF_HEAD_DIM = 64
HEAD_W = 128
BRANCH_W = 512
S5_GROUPS = 32
S5_GROUP = 16
S5_STATE = 64
S5_CHUNK = 16
S5_SLAB = 8
S5_VMEM_LIMIT = 58 * 1024 * 1024
MLSTM_CHUNK = 128
CONV_K = 5
N_GATES = 16
N_EXPERTS = 16
CAPACITY_FACTOR = 2
LANES = 128
VMEM_LIMIT = 52 * 1024 * 1024
NEG_BIG = -1e30

SEG_Q, SEG_K, SEG_V, SEG_S5, SEG_MQ, SEG_MK, SEG_MV, SEG_MO, SEG_GATE = range(9)
PA_WIDTH = 8 * BRANCH_W + 3 * 1024


def _cparams(*sem):
    return pltpu.CompilerParams(dimension_semantics=sem, vmem_limit_bytes=VMEM_LIMIT)


def _mod_body(c_ref, w_ref, b_ref, o_ref):
    cv = c_ref[...]
    s = cv * jax.nn.sigmoid(cv)
    o_ref[0] = jnp.dot(s, w_ref[0], precision=HI, preferred_element_type=F32) + b_ref[0]


def _mod_vectors(cvec, w_mod, b_mod):
    n_layers, d, n = w_mod.shape
    tn = n // 4
    return pl.pallas_call(
        _mod_body,
        grid=(n_layers, n // tn),
        in_specs=[
            pl.BlockSpec((8, d), lambda l, j: (0, 0)),
            pl.BlockSpec((1, d, tn), lambda l, j: (l, 0, j)),
            pl.BlockSpec((1, 1, tn), lambda l, j: (l, 0, j)),
        ],
        out_specs=pl.BlockSpec((1, 8, tn), lambda l, j: (l, 0, j)),
        out_shape=jax.ShapeDtypeStruct((n_layers, 8, n), F32),
        compiler_params=_cparams("parallel", "parallel"),
        name="mod_vectors",
    )(cvec, w_mod, b_mod.reshape(n_layers, 1, n))


def _modulated_norm(x, mod, i_shift, i_scale):
    ms = jnp.mean(x * x, axis=-1, keepdims=True)
    xn = x * lax.rsqrt(ms + NORM_EPS)
    return xn * (1.0 + mod[i_scale:i_scale + 1, :]) + mod[i_shift:i_shift + 1, :]


def _proj_body(x_ref, m_ref, w_ref, wg_ref, pa_ref, pg_ref, *, tn):
    hb = _modulated_norm(x_ref[...], m_ref[0], 0, 1).astype(BF16)
    pg_ref[...] = jnp.dot(hb, wg_ref[...], preferred_element_type=F32)
    for c0 in range(0, pa_ref.shape[1], tn):
        pa_ref[:, c0:c0 + tn] = jnp.dot(hb, w_ref[:, c0:c0 + tn], preferred_element_type=F32).astype(BF16)


def _group_of_block(i, blocks_per_sample, n_samples):
    return jnp.minimum(i // blocks_per_sample, n_samples)


def _project(xa, mod, wa, wg, dims):
    b, t, tc, d = dims
    r = xa.shape[0]
    tm = 512
    npa = wa.shape[1]
    once = pl.Buffered(1)
    return pl.pallas_call(
        functools.partial(_proj_body, tn=1024),
        grid=(r // tm,),
        in_specs=[
            pl.BlockSpec((tm, d), lambda i: (i, 0)),
            pl.BlockSpec((1, N_MOD, d), lambda i: (_group_of_block(i, t // tm, b), 0, 0)),
            pl.BlockSpec((d, npa), lambda i: (0, 0), pipeline_mode=once),
            pl.BlockSpec((d, LANES), lambda i: (0, 0), pipeline_mode=once),
        ],
        out_specs=[
            pl.BlockSpec((tm, npa), lambda i: (i, 0)),
            pl.BlockSpec((tm, LANES), lambda i: (i, 0)),
        ],
        out_shape=[
            jax.ShapeDtypeStruct((r, npa), BF16),
            jax.ShapeDtypeStruct((r, LANES), F32),
        ],
        compiler_params=_cparams("parallel"),
        name="in_proj",
    )(xa, mod, wa, wg)


def _qk_norm_rope(x_bf, gain, cosf, sinf, seg_ones, first_half):
    x = x_bf.astype(F32)
    x2 = x * x
    hi = x2.astype(BF16)
    lo = (x2 - hi.astype(F32)).astype(BF16)
    ss = (jnp.dot(hi, seg_ones, preferred_element_type=F32)
          + jnp.dot(lo, seg_ones, preferred_element_type=F32))
    xn = x * lax.rsqrt(ss * (1.0 / DIFF_HEAD_DIM) + NORM_EPS) * gain
    half = DIFF_HEAD_DIM // 2
    width = x.shape[1]
    nxt = pltpu.roll(xn, width - half, 1)
    prv = pltpu.roll(xn, half, 1)
    partner = jnp.where(first_half, nxt, prv)
    return xn * cosf + partner * sinf


def _short_conv_silu(prev_ref, cur_ref, next_ref, w, bias, at_start, at_end, out_scale):
    tp = cur_ref.shape[0]
    prev = prev_ref[...].astype(F32)[8:16]
    nxt = next_ref[...].astype(F32)[0:8]
    prev = jnp.where(at_start, 0.0, prev)
    nxt = jnp.where(at_end, 0.0, nxt)
    ext = jnp.concatenate([prev, cur_ref[...].astype(F32), nxt], axis=0)
    acc = bias
    for kk in range(CONV_K):
        off = 8 + kk - CONV_K // 2
        acc = acc + w[kk:kk + 1, :] * ext[off:off + tp]
    y = acc * jax.nn.sigmoid(acc)
    return y * out_scale


def _prep_body(q_ref, k_ref, mqp_ref, mq_ref, mqn_ref, mkp_ref, mk_ref, mkn_ref,
               cos_ref, sin_ref, gq_ref, gk_ref, so_ref, cw_ref, cb_ref,
               qo_ref, ko_ref, mqo_ref, mko_ref, *, b, t, tc, tp):
    cos4 = jnp.concatenate([cos_ref[...]] * 4, axis=1)
    sin4 = jnp.concatenate([sin_ref[...]] * 4, axis=1)
    lane = lax.broadcasted_iota(jnp.int32, (tp, BRANCH_W), 1)
    first_half = (lane % DIFF_HEAD_DIM) < (DIFF_HEAD_DIM // 2)
    seg_ones = so_ref[...]
    qo_ref[...] = _qk_norm_rope(q_ref[...], gq_ref[...], cos4, sin4, seg_ones, first_half).astype(BF16)
    ko_ref[...] = _qk_norm_rope(k_ref[...], gk_ref[...], cos4, sin4, seg_ones, first_half).astype(BF16)

    row0 = pl.program_id(0) * tp
    in_lat = row0 < b * t
    local = jnp.where(in_lat, row0 % t, (row0 - b * t) % tc)
    seq_len = jnp.where(in_lat, t, tc)
    at_start = local == 0
    at_end = local + tp == seq_len
    cw = cw_ref[...]
    cb = cb_ref[...]
    mqo_ref[...] = _short_conv_silu(mqp_ref, mq_ref, mqn_ref, cw[:, :BRANCH_W], cb[:, :BRANCH_W],
                                    at_start, at_end, 1.0).astype(BF16)
    mko_ref[...] = _short_conv_silu(mkp_ref, mk_ref, mkn_ref, cw[:, BRANCH_W:], cb[:, BRANCH_W:],
                                    at_start, at_end, HEAD_W ** -0.5).astype(BF16)


def _prepare(pa, cos_tab, sin_tab, gq, gk, seg_ones, conv_w, conv_b, dims):
    b, t, tc, d = dims
    r = pa.shape[0]
    tp = 256
    halo = 16
    hb = tp // halo
    last_halo = r // halo - 1

    def cur(seg):
        return pl.BlockSpec((tp, BRANCH_W), lambda i: (i, seg))

    def prev(seg):
        return pl.BlockSpec((halo, BRANCH_W), lambda i: (jnp.maximum(i * hb - 1, 0), seg))

    def nxt(seg):
        return pl.BlockSpec((halo, BRANCH_W), lambda i: (jnp.minimum((i + 1) * hb, last_halo), seg))

    full = lambda shape: pl.BlockSpec(shape, lambda i: (0, 0))
    out = jax.ShapeDtypeStruct((r, BRANCH_W), BF16)
    return pl.pallas_call(
        functools.partial(_prep_body, b=b, t=t, tc=tc, tp=tp),
        grid=(r // tp,),
        in_specs=[
            cur(SEG_Q), cur(SEG_K),
            prev(SEG_MQ), cur(SEG_MQ), nxt(SEG_MQ),
            prev(SEG_MK), cur(SEG_MK), nxt(SEG_MK),
            pl.BlockSpec((tp, LANES), lambda i: (i, 0)),
            pl.BlockSpec((tp, LANES), lambda i: (i, 0)),
            full((1, BRANCH_W)), full((1, BRANCH_W)),
            full((BRANCH_W, BRANCH_W)),
            full((8, 2 * BRANCH_W)), full((1, 2 * BRANCH_W)),
        ],
        out_specs=[pl.BlockSpec((tp, BRANCH_W), lambda i: (i, 0))] * 4,
        out_shape=[out, out, out, out],
        compiler_params=_cparams("parallel"),
        name="row_prep",
    )(pa, pa, pa, pa, pa, pa, pa, pa, cos_tab, sin_tab, gq, gk, seg_ones, conv_w, conv_b)


ATTN_HEADS_PER_STEP = 2


def _attn_body(lam_ref, og_ref, sh_ref, q_ref, k_ref, v_ref, kc_ref, vc_ref, o_ref,
               qm_ref, m_ref, l_ref, acc_ref, *, lam_init, has_ctx, nk, fixed_shift):
    kj = pl.program_id(3)
    heads = range(ATTN_HEADS_PER_STEP)
    head_lanes = lambda hh: slice(hh * HEAD_W, (hh + 1) * HEAD_W)

    def process(k_all, v_all):
        for hh in heads:
            kb, vb = k_all[:, head_lanes(hh)], v_all[:, head_lanes(hh)]
            for mi in range(2):
                si = 2 * hh + mi
                s = jnp.dot(kb, qm_ref[si], preferred_element_type=F32)
                if fixed_shift:
                    p = jnp.exp2(s - sh_ref[0:1, 0:1])
                    l_ref[si] += jnp.sum(p, axis=0, keepdims=True)
                    acc_ref[si] += lax.dot_general(vb, p.astype(BF16), (((0,), (0,)), ((), ())),
                                                   preferred_element_type=F32)
                else:
                    m_old = m_ref[si]
                    m_new = jnp.maximum(m_old, jnp.max(s, axis=0, keepdims=True))
                    alpha = jnp.exp2(m_old - m_new)
                    p = jnp.exp2(s - m_new)
                    l_ref[si] = alpha * l_ref[si] + jnp.sum(p, axis=0, keepdims=True)
                    pv = lax.dot_general(vb, p.astype(BF16), (((0,), (0,)), ((), ())),
                                         preferred_element_type=F32)
                    acc_ref[si] = alpha * acc_ref[si] + pv
                    m_ref[si] = m_new

    @pl.when(kj == 0)
    def _():
        for hh in heads:
            qt = q_ref[:, head_lanes(hh)].astype(F32).T.astype(BF16)
            row = lax.broadcasted_iota(jnp.int32, qt.shape, 0)
            zero = jnp.zeros_like(qt)
            qm_ref[2 * hh] = jnp.where(row < DIFF_HEAD_DIM, qt, zero)
            qm_ref[2 * hh + 1] = jnp.where(row >= DIFF_HEAD_DIM, qt, zero)
        m_ref[...] = jnp.full(m_ref.shape, NEG_BIG, F32)
        l_ref[...] = jnp.zeros(l_ref.shape, F32)
        acc_ref[...] = jnp.zeros(acc_ref.shape, F32)
        if has_ctx:
            process(kc_ref[...], vc_ref[...])

    process(k_ref[...], v_ref[...])

    @pl.when(kj == nk - 1)
    def _():
        lv = lam_ref[...]
        lam = (jnp.exp(jnp.sum(lv[0:1] * lv[1:2], keepdims=True))
               - jnp.exp(jnp.sum(lv[2:3] * lv[3:4], keepdims=True)) + lam_init)
        for hh in heads:
            o = (acc_ref[2 * hh] / l_ref[2 * hh]
                 - lam * (acc_ref[2 * hh + 1] / l_ref[2 * hh + 1]))
            ms = jnp.mean(o * o, axis=0, keepdims=True)
            o = o * lax.rsqrt(ms + NORM_EPS)
            o_ref[:, head_lanes(hh)] = (o.T * (og_ref[...] * (1.0 - lam_init))).astype(BF16)


MAX_FIXED_SHIFT = 48.0


def _attention(qh, kh, pa, lam_vecs, out_gain, score_bound, lam_init, dims, *, ctx_queries):
    shift = jnp.full((1, LANES), score_bound, F32)
    run = lambda fixed: _attention_call(qh, kh, pa, lam_vecs, out_gain, shift, lam_init, dims,
                                        ctx_queries=ctx_queries, fixed_shift=fixed)
    return lax.cond(score_bound <= MAX_FIXED_SHIFT, lambda: run(True), lambda: run(False))


def _attention_call(qh, kh, pa, lam_vecs, out_gain, shift, lam_init, dims, *, ctx_queries, fixed_shift):
    b, t, tc, d = dims
    v_col = SEG_V * (BRANCH_W // HEAD_W)
    ctx_blk0 = (b * t) // tc
    if ctx_queries:
        tq = tk = tc
        nq, nk = 1, 1
        q_row = lambda bb, qi: ctx_blk0 + bb
        k_row = lambda bb, kj: ctx_blk0 + bb
        n_rows = b * tc
        o_row = lambda bb, qi: bb
    else:
        tq = min(2048, t)
        tk = min(1024, t)
        nq, nk = t // tq, t // tk
        q_row = lambda bb, qi: bb * nq + qi
        k_row = lambda bb, kj: bb * nk + kj
        n_rows = b * t
        o_row = q_row
    body = functools.partial(_attn_body, lam_init=lam_init, has_ctx=not ctx_queries, nk=nk,
                             fixed_shift=fixed_shift)
    hp = ATTN_HEADS_PER_STEP
    wb = hp * HEAD_W
    v_blk = SEG_V * BRANCH_W // wb
    return pl.pallas_call(
        body,
        grid=(b, N_HEADS // hp, nq, nk),
        in_specs=[
            pl.BlockSpec((4, DIFF_HEAD_DIM), lambda bb, h, qi, kj: (0, 0)),
            pl.BlockSpec((1, HEAD_W), lambda bb, h, qi, kj: (0, 0)),
            pl.BlockSpec((1, LANES), lambda bb, h, qi, kj: (0, 0)),
            pl.BlockSpec((tq, wb), lambda bb, h, qi, kj: (q_row(bb, qi), h)),
            pl.BlockSpec((tk, wb), lambda bb, h, qi, kj: (k_row(bb, kj), h)),
            pl.BlockSpec((tk, wb), lambda bb, h, qi, kj: (k_row(bb, kj), v_blk + h)),
            pl.BlockSpec((tc, wb), lambda bb, h, qi, kj: (ctx_blk0 + bb, h)),
            pl.BlockSpec((tc, wb), lambda bb, h, qi, kj: (ctx_blk0 + bb, v_blk + h)),
        ],
        out_specs=pl.BlockSpec((tq, wb), lambda bb, h, qi, kj: (o_row(bb, qi), h)),
        out_shape=jax.ShapeDtypeStruct((n_rows, BRANCH_W), BF16),
        scratch_shapes=[
            pltpu.VMEM((2 * hp, HEAD_W, tq), BF16),
            pltpu.VMEM((2 * hp, 1, tq), F32),
            pltpu.VMEM((2 * hp, 1, tq), F32),
            pltpu.VMEM((2 * hp, HEAD_W, tq), F32),
        ],
        compiler_params=_cparams("parallel", "parallel", "parallel", "arbitrary"),
        name=("diff_attn_ctx" if ctx_queries else "diff_attn") + ("_fixed" if fixed_shift else ""),
    )(lam_vecs, out_gain, shift, qh, kh, pa, kh, pa)


def _s5_matrices(lam_re, lam_im, log_step, b_re, b_im, c_re, c_im, n_levels):
    ll, hg, pp, gg = S5_CHUNK, S5_GROUP, S5_STATE, S5_GROUPS
    dt = jnp.exp(log_step)[:, :, None]
    lr, li = lam_re * dt, lam_im * dt

    def a_pow(tau):
        tau = tau.astype(F32)[:, None, None, None]
        mag = jnp.exp(lr * tau)
        return mag * jnp.cos(li * tau), mag * jnp.sin(li * tau)

    ar1, ai1 = a_pow(jnp.ones((1,)))
    nr, ni = ar1[0] - 1.0, ai1[0]
    den = lam_re * lam_re + lam_im * lam_im
    f_re = (nr * lam_re + ni * lam_im) / den
    f_im = (ni * lam_re - nr * lam_im) / den
    bb_re = f_re[..., None] * b_re - f_im[..., None] * b_im
    bb_im = f_re[..., None] * b_im + f_im[..., None] * b_re

    ar, ai = a_pow(jnp.arange(ll + 1))
    ca_re = c_re[None] * ar[:, :, :, None, :] - c_im[None] * ai[:, :, :, None, :]
    ca_im = c_re[None] * ai[:, :, :, None, :] + c_im[None] * ar[:, :, :, None, :]
    kk = (jnp.einsum('tdgop,dgph->tdgoh', ca_re, bb_re, precision=HI)
          - jnp.einsum('tdgop,dgph->tdgoh', ca_im, bb_im, precision=HI))
    ns, sl = gg // S5_SLAB, S5_SLAB
    n_state = sl * pp
    width = ll * LANES
    jj = jnp.arange(ll)
    grp = jnp.arange(sl)
    same_go = (grp[:, None] == jnp.arange(LANES)[None, :] // hg).astype(F32)
    same_gp = (grp[:, None] == jnp.arange(n_state)[None, :] // pp).astype(F32)

    kt = kk[:ll].transpose(1, 0, 2, 4, 3).reshape(2, ll, ns, sl, hg, hg)
    kt = kt.transpose(0, 2, 1, 4, 3, 5).reshape(2, ns, ll, hg, LANES)
    kbd = (kt[:, :, :, None] * same_go[None, None, None, :, None, :]).reshape(2, ns, ll, LANES, LANES)

    pw = jnp.stack([ll - 1 - jj, jj])
    sel = lambda a: jnp.stack([a[pw[0], 0], a[pw[1], 1]]).reshape(2, ll, ns, n_state).transpose(0, 2, 1, 3)
    s_re, s_im = sel(ar), sel(ai)
    lane_bb = lambda m: m.reshape(2, ns, sl, pp, hg).transpose(0, 1, 4, 2, 3).reshape(2, ns, hg, n_state)
    t_re, t_im = lane_bb(bb_re), lane_bb(bb_im)

    def in_slab(a, bmat, c, dmat, sign):
        v = a[:, :, :, None, :] * bmat[:, :, None, :, :] + sign * c[:, :, :, None, :] * dmat[:, :, None, :, :]
        v = v[:, :, :, None] * same_gp[None, None, None, :, None, :]
        return v.reshape(2, ns, width, n_state)

    po = jnp.stack([jj + 1, ll - jj])

    def out_slab_t(a):
        m = jnp.stack([a[po[0], 0], a[po[1], 1]])
        m = m.reshape(2, ll, ns, sl, hg, pp).transpose(0, 2, 1, 4, 3, 5).reshape(2, ns, ll, hg, n_state)
        v = m[:, :, :, None] * same_gp[None, None, None, :, None, :]
        return v.reshape(2, ns, width, n_state)

    lev = (ll * (2 ** jnp.arange(n_levels))).astype(F32)
    alr, ali = a_pow(lev)
    pad_lev = (-n_levels) % 8

    def lev_slab(a):
        a = a.transpose(1, 0, 2, 3).reshape(2, n_levels, ns, n_state).transpose(0, 2, 1, 3)
        return jnp.pad(a, ((0, 0), (0, 0), (0, pad_lev), (0, 0)))

    return dict(kbd=kbd.astype(BF16),
                in_re=in_slab(s_re, t_re, s_im, t_im, -1.0).astype(BF16),
                in_im=in_slab(s_re, t_im, s_im, t_re, 1.0).astype(BF16),
                out_re_t=out_slab_t(ca_re).astype(BF16), out_im_t=out_slab_t(-ca_im).astype(BF16),
                al_re=lev_slab(alr), al_im=lev_slab(ali))


def _s5_body(ul_ref, uc_ref, kbd_ref, inr_ref, ini_ref, outr_ref, outi_ref, alr_ref, ali_ref,
             yl_ref, yc_ref, sr_ref, si_ref, w_ref, *, nlat, nctx, pad, n_levels, rev):
    ll = S5_CHUNK
    nc = nlat + nctx

    @pl.when(pl.program_id(1) == 0)
    def _():
        zero = jnp.zeros((LANES, LANES), BF16)
        for j in range(ll):
            for i in range(ll):
                lag = j - i if rev else i - j
                w_ref[j * LANES:(j + 1) * LANES, i * LANES:(i + 1) * LANES] = (
                    kbd_ref[0, 0, lag] if lag >= 0 else zero)

    cat = lambda ref: jnp.concatenate([ref[j] for j in range(ll)], axis=1)
    ulat, uctx = cat(ul_ref), cat(uc_ref)
    u = jnp.concatenate([ulat, uctx] if rev else [uctx, ulat], axis=0)
    lat0, ctx0 = (0, nlat) if rev else (nctx, 0)
    lo = 0 if rev else pad
    zero0 = nc if rev else 0
    zeros = jnp.zeros((pad, sr_ref.shape[1]), F32)
    sr_ref[zero0:zero0 + pad, :] = zeros
    si_ref[zero0:zero0 + pad, :] = zeros
    sr_ref[lo:lo + nc, :] = jnp.dot(u, inr_ref[0, 0], preferred_element_type=F32)
    si_ref[lo:lo + nc, :] = jnp.dot(u, ini_ref[0, 0], preferred_element_type=F32)
    for lev in range(n_levels):
        dd = 1 << lev
        src = lo + dd if rev else lo - dd
        a_r = alr_ref[0, 0, lev:lev + 1, :]
        a_i = ali_ref[0, 0, lev:lev + 1, :]
        cr, ci = sr_ref[lo:lo + nc, :], si_ref[lo:lo + nc, :]
        pr, pi = sr_ref[src:src + nc, :], si_ref[src:src + nc, :]
        sr_ref[lo:lo + nc, :] = cr + a_r * pr - a_i * pi
        si_ref[lo:lo + nc, :] = ci + a_r * pi + a_i * pr
    ent = lo + 1 if rev else lo - 1
    er = sr_ref[ent:ent + nc, :].astype(BF16)
    ei = si_ref[ent:ent + nc, :].astype(BF16)
    nt = (((1,), (1,)), ((), ()))
    for ib in range(ll // 2):
        cols = slice(ib * 2 * LANES, (ib + 1) * 2 * LANES)
        y = (jnp.dot(u, w_ref[:, cols], preferred_element_type=F32)
             + lax.dot_general(er, outr_ref[0, 0, cols, :], nt, preferred_element_type=F32)
             + lax.dot_general(ei, outi_ref[0, 0, cols, :], nt, preferred_element_type=F32)).astype(BF16)
        for k in range(2):
            yl_ref[2 * ib + k] = y[lat0:lat0 + nlat, k * LANES:(k + 1) * LANES]
            yc_ref[2 * ib + k] = y[ctx0:ctx0 + nctx, k * LANES:(k + 1) * LANES]


def _s5_levels(nc):
    return max(1, (nc - 1).bit_length())


def _s5_scan(u3, mats, dims, *, rev):
    b, t, tc, d = dims
    ll = S5_CHUNK
    nlat, nctx = t // ll, tc // ll
    n_levels = _s5_levels(nlat + nctx)
    pad = max(8, 1 << (n_levels - 1))
    ns = S5_GROUPS // S5_SLAB
    n_state = S5_SLAB * S5_STATE
    width = ll * LANES
    lev_rows = mats["al_re"].shape[2]
    once = pl.Buffered(1)
    dd = int(rev)
    per_slab = lambda shape: pl.BlockSpec((1, 1) + shape, lambda s, bb: (dd, s) + (0,) * len(shape),
                                          pipeline_mode=once)
    return pl.pallas_call(
        functools.partial(_s5_body, nlat=nlat, nctx=nctx, pad=pad, n_levels=n_levels, rev=rev),
        grid=(ns, b),
        in_specs=[
            pl.BlockSpec((ll, nlat, LANES), lambda s, bb: (0, bb, s), pipeline_mode=once),
            pl.BlockSpec((ll, nctx, LANES), lambda s, bb: (0, b * t // tc + bb, s)),
            per_slab((ll, LANES, LANES)),
            per_slab((width, n_state)), per_slab((width, n_state)),
            per_slab((width, n_state)), per_slab((width, n_state)),
            per_slab((lev_rows, n_state)), per_slab((lev_rows, n_state)),
        ],
        out_specs=[
            pl.BlockSpec((ll, nlat, LANES), lambda s, bb: (0, bb, s)),
            pl.BlockSpec((ll, nctx, LANES), lambda s, bb: (0, bb, s)),
        ],
        out_shape=[
            jax.ShapeDtypeStruct((ll, b * nlat, BRANCH_W), BF16),
            jax.ShapeDtypeStruct((ll, b * nctx, BRANCH_W), BF16),
        ],
        scratch_shapes=[pltpu.VMEM((pad + nlat + nctx, n_state), F32)] * 2
        + [pltpu.VMEM((width, width), BF16)],
        compiler_params=pltpu.CompilerParams(dimension_semantics=("parallel", "arbitrary"),
                                             vmem_limit_bytes=S5_VMEM_LIMIT),
        name="s5_scan_bwd" if rev else "s5_scan_fwd",
    )(u3, u3, mats["kbd"], mats["in_re"], mats["in_im"], mats["out_re_t"], mats["out_im_t"],
      mats["al_re"], mats["al_im"])


def _s5_mixer(pa, mats, dims):
    r = pa.shape[0]
    ll = S5_CHUNK
    u = pa[:, SEG_S5 * BRANCH_W:(SEG_S5 + 1) * BRANCH_W]
    u3 = u.reshape(r // ll, ll, BRANCH_W).transpose(1, 0, 2)
    ys = []
    for dd in range(2):
        yl, yc = _s5_scan(u3, mats, dims, rev=bool(dd))
        ys.append(jnp.concatenate([yl, yc], axis=1).transpose(1, 0, 2).reshape(r, BRANCH_W))
    return ys


def _log_sigmoid(x):
    return -(jnp.maximum(-x, 0.0) + jnp.log1p(jnp.exp(-jnp.abs(x))))


def _mlstm_body(q_ref, k_ref, v_ref, g_ref, gb_ref, h_ref, c_ref, n_ref, m_ref):
    dd = pl.program_id(1)
    fwd = dd == 0

    @pl.when(pl.program_id(2) == 0)
    def _():
        c_ref[...] = jnp.zeros(c_ref.shape, F32)
        n_ref[...] = jnp.zeros(n_ref.shape, F32)
        m_ref[...] = jnp.zeros(m_ref.shape, F32)

    ll = MLSTM_CHUNK
    row = lax.broadcasted_iota(jnp.int32, (ll, ll), 0)
    col = lax.broadcasted_iota(jnp.int32, (ll, ll), 1)
    sign = jnp.where(fwd, 1, -1)
    before = (col - row) * sign >= 0
    before_f = jnp.where(before, 1.0, 0.0)
    after_f = jnp.where((row - col) * sign >= 0, 1.0, 0.0)

    g = g_ref[...] + gb_ref[...]
    lf = _log_sigmoid(g)
    g_t = g.T
    lf_t = lf.T
    cum_cols = jnp.dot(after_f, lf, precision=HI, preferred_element_type=F32)
    cum_rows = jnp.dot(lf_t, before_f, precision=HI, preferred_element_type=F32)
    nt = (((1,), (1,)), ((), ()))
    outs = []
    for h in range(N_HEADS):
        def pick_col(a, base):
            return jnp.where(fwd, a[:, base + h:base + h + 1], a[:, 8 + base + h:8 + base + h + 1])

        def pick_row(a, base):
            return jnp.where(fwd, a[base + h:base + h + 1, :], a[8 + base + h:8 + base + h + 1, :])

        ig_col, cum_col = pick_col(g, 0), pick_col(cum_cols, 4)
        ig_row, lf_row, cum_row = pick_row(g_t, 0), pick_row(lf_t, 4), pick_row(cum_rows, 4)
        gtot = jnp.sum(lf_row, axis=1, keepdims=True)

        hs = slice(h * HEAD_W, (h + 1) * HEAD_W)
        q = q_ref[:, hs]
        k = k_ref[:, hs]
        v_t = v_ref[:, hs].astype(F32).T.astype(BF16)
        c0 = c_ref[h]
        n0 = n_ref[h]
        m0 = m_ref[h][:, 0:1]

        dmat = jnp.where(before, cum_row + (ig_col - cum_col), NEG_BIG)
        inter = cum_row + m0
        m_t = jnp.maximum(inter, jnp.max(dmat, axis=0, keepdims=True))
        pm = jnp.exp(dmat - m_t)
        ei = jnp.exp(inter - m_t)
        kq = lax.dot_general(k, q, nt, preferred_element_type=F32)
        wq = pm * kq
        cq = lax.dot_general(c0.astype(BF16), q, nt, preferred_element_type=F32)
        num = jnp.dot(v_t, wq.astype(BF16), preferred_element_type=F32) + ei * cq
        nq = lax.dot_general(jnp.broadcast_to(n0, (8, HEAD_W)).astype(BF16), q, nt,
                             preferred_element_type=F32)[0:1]
        den = jnp.sum(wq, axis=0, keepdims=True) + ei * nq
        outs.append((num / jnp.maximum(jnp.abs(den), jnp.exp(-m_t))).T)

        w_row = gtot - cum_row + ig_row
        mw = jnp.max(w_row, axis=1, keepdims=True)
        ew = jnp.exp(w_row - mw)
        vw = (v_t.astype(F32) * ew).astype(BF16)
        kv = jnp.dot(vw, k, preferred_element_type=F32)
        ks = jnp.dot(jnp.broadcast_to(ew, (8, ll)).astype(BF16), k, preferred_element_type=F32)[0:1]
        m_new = jnp.maximum(gtot + m0, mw)
        a = jnp.exp(gtot + m0 - m_new)
        e = jnp.exp(mw - m_new)
        c_ref[h] = a * c0 + e * kv
        n_ref[h] = a * n0 + e * ks
        m_ref[h] = jnp.broadcast_to(m_new, (1, LANES))
    h_ref[0] = jnp.concatenate(outs, axis=1).astype(BF16)


def _mlstm(mq, mk, pa, pg, gate_bias, dims):
    b, t, tc, d = dims
    r = pa.shape[0]
    ll = MLSTM_CHUNK
    nctx, nlat = tc // ll, t // ll
    ctx0 = (b * t) // ll

    def rb(bb, dd, c):
        is_ctx = c < nctx
        cc = jnp.where(dd == 0, c, nctx - 1 - c)
        cl = jnp.where(dd == 0, c - nctx, nlat - 1 - (c - nctx))
        return jnp.where(is_ctx, ctx0 + bb * nctx + cc, bb * nlat + cl)

    return pl.pallas_call(
        _mlstm_body,
        grid=(b, 2, nctx + nlat),
        in_specs=[
            pl.BlockSpec((ll, BRANCH_W), lambda bb, dd, c: (rb(bb, dd, c), 0)),
            pl.BlockSpec((ll, BRANCH_W), lambda bb, dd, c: (rb(bb, dd, c), 0)),
            pl.BlockSpec((ll, BRANCH_W), lambda bb, dd, c: (rb(bb, dd, c), SEG_MV)),
            pl.BlockSpec((ll, LANES), lambda bb, dd, c: (rb(bb, dd, c), 0)),
            pl.BlockSpec((1, LANES), lambda bb, dd, c: (0, 0)),
        ],
        out_specs=pl.BlockSpec((1, ll, BRANCH_W), lambda bb, dd, c: (dd, rb(bb, dd, c), 0)),
        out_shape=jax.ShapeDtypeStruct((2, r, BRANCH_W), BF16),
        scratch_shapes=[
            pltpu.VMEM((N_HEADS, HEAD_W, HEAD_W), F32),
            pltpu.VMEM((N_HEADS, 1, HEAD_W), F32),
            pltpu.VMEM((N_HEADS, 1, LANES), F32),
        ],
        compiler_params=_cparams("parallel", "parallel", "arbitrary"),
        name="mlstm",
    )(mq, mk, pa, pg, gate_bias)


def _merge_body(ya_ref, y0_ref, y1_ref, u_ref, h0_ref, h1_ref, mo_ref, ga_ref, gb_ref, gc_ref, x_ref, m_ref,
                d_ref, wglu_ref, wb_ref, wo_ref, wr_ref, wrt_ref,
                x1_ref, h2_ref, aff_ref, afft_ref):
    mod = m_ref[0]
    ys = d_ref[...] * u_ref[...].astype(F32) + y0_ref[...].astype(F32) + y1_ref[...].astype(F32)
    gl = jax.nn.gelu(ys)
    yb = gl * jax.nn.sigmoid(jnp.dot(gl.astype(BF16), wglu_ref[...], preferred_element_type=F32))
    yc = (h0_ref[0].astype(F32) + h1_ref[0].astype(F32)) * jax.nn.sigmoid(mo_ref[...].astype(F32))
    gate = lambda ref: jax.nn.sigmoid(ref[...].astype(F32))
    mixed = (gate(ga_ref) * jnp.dot(ya_ref[...], wb_ref[0], preferred_element_type=F32)
             + gate(gb_ref) * jnp.dot(yb.astype(BF16), wb_ref[1], preferred_element_type=F32)
             + gate(gc_ref) * jnp.dot(yc.astype(BF16), wb_ref[2], preferred_element_type=F32))
    out = jnp.dot(mixed.astype(BF16), wo_ref[...], preferred_element_type=F32)
    x1 = x_ref[...] + mod[2:3, :] * out
    x1_ref[...] = x1
    h2 = _modulated_norm(x1, mod, 3, 4)
    h2_ref[...] = h2.astype(BF16)
    logits = jnp.dot(h2, wr_ref[...], precision=HI, preferred_element_type=F32)
    lane = lax.broadcasted_iota(jnp.int32, logits.shape, 1)
    logits = jnp.where(lane < N_EXPERTS, logits, NEG_BIG)
    ex = jnp.exp(logits - jnp.max(logits, axis=1, keepdims=True))
    aff_ref[...] = ex / jnp.sum(ex, axis=1, keepdims=True)
    lt = lax.dot_general(wrt_ref[...], h2, (((1,), (1,)), ((), ())), precision=HI,
                         preferred_element_type=F32)
    et = jnp.exp(lt - jnp.max(lt, axis=0, keepdims=True))
    afft_ref[...] = et / jnp.sum(et, axis=0, keepdims=True)


def _merge(ya, y0, y1, pa, hm, xa, mod, s5_d, w_glu, w_branch, w_out, wr_pad, wr_t, dims):
    b, t, tc, d = dims
    r = xa.shape[0]
    tm = 256
    gseg = SEG_GATE * BRANCH_W // d
    rowblk = lambda width, col=0: pl.BlockSpec((tm, width), lambda i: (i, col))
    full2 = lambda shape: pl.BlockSpec(shape, lambda i: (0, 0))
    return pl.pallas_call(
        _merge_body,
        grid=(r // tm,),
        in_specs=[
            rowblk(BRANCH_W), rowblk(BRANCH_W), rowblk(BRANCH_W),
            rowblk(BRANCH_W, SEG_S5),
            pl.BlockSpec((1, tm, BRANCH_W), lambda i: (0, i, 0)),
            pl.BlockSpec((1, tm, BRANCH_W), lambda i: (1, i, 0)),
            rowblk(BRANCH_W, SEG_MO),
            rowblk(d, gseg), rowblk(d, gseg + 1), rowblk(d, gseg + 2),
            rowblk(d),
            pl.BlockSpec((1, N_MOD, d), lambda i: (_group_of_block(i, t // tm, b), 0, 0)),
            full2((1, BRANCH_W)),
            full2((BRANCH_W, BRANCH_W)),
            pl.BlockSpec((3, BRANCH_W, d), lambda i: (0, 0, 0)),
            full2((d, d)),
            full2((d, LANES)),
            full2((N_EXPERTS, d)),
        ],
        out_specs=[
            rowblk(d), rowblk(d), rowblk(LANES),
            pl.BlockSpec((N_EXPERTS, tm), lambda i: (0, i)),
        ],
        out_shape=[
            jax.ShapeDtypeStruct((r, d), F32),
            jax.ShapeDtypeStruct((r, d), BF16),
            jax.ShapeDtypeStruct((r, LANES), F32),
            jax.ShapeDtypeStruct((N_EXPERTS, r), F32),
        ],
        compiler_params=_cparams("parallel"),
        name="merge",
    )(ya, y0, y1, pa, hm, hm, pa, pa, pa, pa, xa, mod, s5_d, w_glu, w_branch, w_out, wr_pad, wr_t)


def _route_body(a_ref, tri_ref, low_ref, pos_ref, offs_ref, *, cap):
    a = a_ref[0]
    e, nb, _ = a.shape
    bits = pltpu.bitcast(a, jnp.int32)

    def count(mask):
        c = jnp.sum(jnp.where(mask, 1.0, 0.0), axis=2, keepdims=True)
        return jnp.sum(c, axis=1, keepdims=True)

    def step(i, thr):
        cand = thr | jnp.left_shift(jnp.int32(1), 30 - i)
        return jnp.where(count(bits >= cand) >= cap, cand, thr)

    thr = lax.fori_loop(0, 31, step, jnp.zeros((e, 1, 1), jnp.int32))
    gt = bits > thr
    eq = bits == thr
    need = cap - count(gt)

    tri = tri_ref[...]
    low = low_ref[...]

    def exclusive_prefix(x):
        x2 = x.reshape(e * nb, LANES).astype(BF16)
        incl = jnp.dot(x2, tri, preferred_element_type=F32)
        before = jnp.sum(jnp.dot(low, x2, preferred_element_type=F32), axis=1, keepdims=True)
        return (incl - x2.astype(F32) + before).reshape(e, nb, LANES), before.reshape(e, nb, 1)

    eq_rank, _ = exclusive_prefix(jnp.where(eq, 1.0, 0.0))
    sel = gt | (eq & (eq_rank < need))
    pos, before = exclusive_prefix(jnp.where(sel, 1.0, 0.0))
    pos_ref[0] = jnp.where(sel, pos, -1.0)
    offs_ref[0] = before.astype(jnp.int32)


def _route(aff3, cap):
    ns, e, nb, _ = aff3.shape
    i = jnp.arange(LANES)
    tri = (i[:, None] <= i[None, :]).astype(BF16)
    r = jnp.arange(e * nb)
    low = ((r[:, None] // nb == r[None, :] // nb) & (r[None, :] < r[:, None])).astype(BF16)
    return pl.pallas_call(
        functools.partial(_route_body, cap=cap),
        grid=(ns,),
        in_specs=[
            pl.BlockSpec((1, e, nb, LANES), lambda s: (s, 0, 0, 0)),
            pl.BlockSpec((LANES, LANES), lambda s: (0, 0)),
            pl.BlockSpec((e * nb, e * nb), lambda s: (0, 0)),
        ],
        out_specs=[
            pl.BlockSpec((1, e, nb, LANES), lambda s: (s, 0, 0, 0)),
            pl.BlockSpec((1, e, nb, 1), lambda s: (s, 0, 0, 0)),
        ],
        out_shape=[
            jax.ShapeDtypeStruct((ns, e, nb, LANES), F32),
            jax.ShapeDtypeStruct((ns, e, nb, 1), jnp.int32),
        ],
        compiler_params=_cparams("parallel"),
        name="route",
    )(aff3, tri, low)


SLOT_ALIGN = 16


DISPATCH_EXPERTS = 2


def _dispatch_body(st_ref, h_ref, a_ref, p_ref, o_ref, oa_ref, *, tb, win, nj, sub):
    ns, eg, j = pl.program_id(0), pl.program_id(1), pl.program_id(2)

    @pl.when(j == 0)
    def _():
        o_ref[...] = jnp.zeros(o_ref.shape, o_ref.dtype)
        oa_ref[...] = jnp.zeros(oa_ref.shape, oa_ref.dtype)

    lane = lax.broadcasted_iota(jnp.int32, (tb, LANES), 1)
    for q in range(DISPATCH_EXPERTS):
        e = eg * DISPATCH_EXPERTS + q
        for k in range(sub):
            start = st_ref[(ns * N_EXPERTS + e) * (nj + 1) + j * sub + k]
            base = pl.multiple_of((start // SLOT_ALIGN) * SLOT_ALIGN, SLOT_ALIGN)
            slot = (base + lax.broadcasted_iota(jnp.int32, (win, tb), 0)).astype(F32)
            pos = p_ref[0, q, 0, :, k * tb:(k + 1) * tb]
            onehot = jnp.where(slot == pos, 1.0, 0.0).astype(BF16)
            rows = jnp.dot(onehot, h_ref[k * tb:(k + 1) * tb, :], preferred_element_type=F32)
            cur = o_ref[0, q, pl.ds(base, win), :]
            o_ref[0, q, pl.ds(base, win), :] = cur + rows.astype(o_ref.dtype)
            aff = jnp.where(lane == e, a_ref[k * tb:(k + 1) * tb, :], 0.0)
            hi = aff.astype(BF16)
            rest = aff - hi.astype(F32)
            mid = rest.astype(BF16)
            low = (rest - mid.astype(F32)).astype(BF16)
            arows = (jnp.dot(onehot, hi, preferred_element_type=F32)
                     + jnp.dot(onehot, mid, preferred_element_type=F32)
                     + jnp.dot(onehot, low, preferred_element_type=F32))
            oa_ref[0, q, pl.ds(base, win), :] = oa_ref[0, q, pl.ds(base, win), :] + arows


def _dispatch(starts, h2, aff, pos, row0, ns, n, cap, tb):
    d = h2.shape[1]
    nj = n // tb
    sub = min(4, nj)
    njs = nj // sub
    win = tb + SLOT_ALIGN
    capp = cap + win
    blk0 = row0 // (sub * tb)
    de = DISPATCH_EXPERTS
    pos_rows = pos.reshape(ns, N_EXPERTS, njs, 1, sub * tb)
    grid_spec = pltpu.PrefetchScalarGridSpec(
        num_scalar_prefetch=1,
        grid=(ns, N_EXPERTS // de, njs),
        in_specs=[
            pl.BlockSpec((sub * tb, d), lambda s, e, j, st: (blk0 + s * njs + j, 0)),
            pl.BlockSpec((sub * tb, LANES), lambda s, e, j, st: (blk0 + s * njs + j, 0)),
            pl.BlockSpec((1, de, 1, 1, sub * tb), lambda s, e, j, st: (s, e, j, 0, 0)),
        ],
        out_specs=[
            pl.BlockSpec((1, de, capp, d), lambda s, e, j, st: (s, e, 0, 0)),
            pl.BlockSpec((1, de, capp, LANES), lambda s, e, j, st: (s, e, 0, 0)),
        ],
    )
    return pl.pallas_call(
        functools.partial(_dispatch_body, tb=tb, win=win, nj=nj, sub=sub),
        grid_spec=grid_spec,
        out_shape=[
            jax.ShapeDtypeStruct((ns, N_EXPERTS, capp, d), BF16),
            jax.ShapeDtypeStruct((ns, N_EXPERTS, capp, LANES), F32),
        ],
        compiler_params=_cparams("parallel", "parallel", "arbitrary"),
        name="dispatch",
    )(starts, h2, aff, pos_rows)


def _expert_body(x_ref, a_ref, wg_ref, wu_ref, wd_ref, y_ref, *, fc):
    x = x_ref[0, 0]
    f = wg_ref.shape[2]
    acc = jnp.zeros((x.shape[0], wd_ref.shape[2]), F32)
    for f0 in range(0, f, fc):
        g = jnp.dot(x, wg_ref[0, :, f0:f0 + fc], preferred_element_type=F32)
        u = jnp.dot(x, wu_ref[0, :, f0:f0 + fc], preferred_element_type=F32)
        hid = (g * jax.nn.sigmoid(g) * u).astype(BF16)
        acc = acc + jnp.dot(hid, wd_ref[0, f0:f0 + fc, :], preferred_element_type=F32)
    aff = jnp.sum(a_ref[0, 0], axis=1, keepdims=True)
    y_ref[0, 0] = (acc * aff).astype(BF16)


def _experts(xs, xa, w_gate, w_up, w_down, cap):
    ns, e, _, d = xs.shape
    f = w_gate.shape[2]
    ts = min(512, cap)
    return pl.pallas_call(
        functools.partial(_expert_body, fc=min(512, f)),
        grid=(e, ns, cap // ts),
        in_specs=[
            pl.BlockSpec((1, 1, ts, d), lambda ee, s, i: (s, ee, i, 0)),
            pl.BlockSpec((1, 1, ts, LANES), lambda ee, s, i: (s, ee, i, 0)),
            pl.BlockSpec((1, d, f), lambda ee, s, i: (ee, 0, 0)),
            pl.BlockSpec((1, d, f), lambda ee, s, i: (ee, 0, 0)),
            pl.BlockSpec((1, f, d), lambda ee, s, i: (ee, 0, 0)),
        ],
        out_specs=pl.BlockSpec((1, 1, ts, d), lambda ee, s, i: (s, ee, i, 0)),
        out_shape=jax.ShapeDtypeStruct((ns, e, cap, d), BF16),
        compiler_params=_cparams("parallel", "parallel", "parallel"),
        name="experts",
    )(xs, xa, w_gate, w_up, w_down)


COMBINE_EXPERTS = 8


def _combine_body(st_ref, *refs, sb, nj, nwin):
    y_refs = refs[:COMBINE_EXPERTS * nwin]
    p_ref, x_ref, m_ref, o_ref = refs[COMBINE_EXPERTS * nwin:]
    ns, j, eg = pl.program_id(0), pl.program_id(1), pl.program_id(2)

    @pl.when(eg == 0)
    def _():
        o_ref[...] = jnp.zeros(o_ref.shape, F32)

    tb = o_ref.shape[0]
    lane = lax.broadcasted_iota(jnp.int32, (tb, LANES), 1)
    slot_iota = lax.broadcasted_iota(jnp.int32, (tb, sb), 1)

    def gathered(k, w):
        e = eg * COMBINE_EXPERTS + k
        start = st_ref[(ns * N_EXPERTS + e) * (nj + 1) + j]
        end = st_ref[(ns * N_EXPERTS + e) * (nj + 1) + j + 1]
        first = (start // sb + w) * sb

        def rows():
            pos = jnp.sum(jnp.where(lane == e, p_ref[...], 0.0), axis=1, keepdims=True)
            onehot = jnp.where(pos == (first + slot_iota).astype(F32), 1.0, 0.0).astype(BF16)
            return jnp.dot(onehot, y_refs[k * nwin + w][0, 0], preferred_element_type=F32)
        return rows, end > first

    acc = gathered(0, 0)[0]()
    for k in range(1, COMBINE_EXPERTS):
        acc = acc + gathered(k, 0)[0]()
    o_ref[...] += acc
    for k in range(COMBINE_EXPERTS):
        for w in range(1, nwin):
            rows, used = gathered(k, w)

            @pl.when(used)
            def _():
                o_ref[...] += rows()

    @pl.when(eg == N_EXPERTS // COMBINE_EXPERTS - 1)
    def _():
        o_ref[...] = x_ref[...] + m_ref[0, 5:6, :] * o_ref[...]


def _combine(starts, ys, pos_t, x1, mod, row0, ns, n, cap, tb, mod_group):
    d = x1.shape[1]
    nj = n // tb
    span = min(tb, cap)
    sb = min(LANES, cap)
    nwin = span // sb + 1
    nsb = cap // sb
    blk0 = row0 // tb

    def window(k, w):
        def index(s, j, eg, st):
            e = eg * COMBINE_EXPERTS + k
            start = st[(s * N_EXPERTS + e) * (nj + 1) + j]
            end = st[(s * N_EXPERTS + e) * (nj + 1) + j + 1]
            blk = start // sb + w
            if w == 0:
                return (s, e, jnp.minimum(blk, nsb - 1), 0)
            used = end > blk * sb
            return (jnp.where(used, s, 0), jnp.where(used, e, 0), jnp.where(used, blk, 0), 0)
        return pl.BlockSpec((1, 1, sb, d), index)

    row = lambda s, j, eg, st: (blk0 + s * nj + j, 0)
    windows = [window(k, w) for k in range(COMBINE_EXPERTS) for w in range(nwin)]
    grid_spec = pltpu.PrefetchScalarGridSpec(
        num_scalar_prefetch=1,
        grid=(ns, nj, N_EXPERTS // COMBINE_EXPERTS),
        in_specs=windows + [
            pl.BlockSpec((tb, LANES), row),
            pl.BlockSpec((tb, d), row),
            pl.BlockSpec((1, N_MOD, d), lambda s, j, eg, st: (mod_group(s), 0, 0)),
        ],
        out_specs=pl.BlockSpec((tb, d), lambda s, j, eg, st: (s * nj + j, 0)),
    )
    return pl.pallas_call(
        functools.partial(_combine_body, sb=sb, nj=nj, nwin=nwin),
        grid_spec=grid_spec,
        out_shape=jax.ShapeDtypeStruct((ns * n, d), F32),
        compiler_params=_cparams("parallel", "parallel", "arbitrary"),
        name="combine",
    )(starts, *([ys] * len(windows)), pos_t, x1, mod)


def _expert_choice(h2, aff, aff_t, x1, mod, w_gate, w_up, w_down, row0, ns, n, mod_group):
    r, d = h2.shape
    e = N_EXPERTS
    cap = CAPACITY_FACTOR * n // e
    tb = min(256, n)
    n_pad = max(n, 8 * LANES)
    a = aff_t[:, row0:row0 + ns * n].reshape(e, ns, n).transpose(1, 0, 2)
    if n_pad > n:
        a = jnp.concatenate([a, jnp.full((ns, e, n_pad - n), -1.0, F32)], axis=2)
    pos, offs = _route(a.reshape(ns, e, n_pad // LANES, LANES), cap)
    pos = pos.reshape(ns, e, n_pad)[:, :, :n]
    starts = offs.reshape(ns, e, n_pad // LANES)[:, :, :n // LANES:tb // LANES]
    starts = jnp.concatenate([starts, jnp.full((ns, e, 1), cap, jnp.int32)], axis=2).reshape(-1)
    xs, xa = _dispatch(starts, h2, aff, pos, row0, ns, n, cap, tb)
    ys = _experts(xs, xa, w_gate, w_up, w_down, cap)
    pos_t = jnp.pad(pos.transpose(0, 2, 1).reshape(ns * n, e), ((0, 0), (0, LANES - e)), constant_values=-1.0)
    pos_t = jnp.pad(pos_t, ((row0, r - row0 - ns * n), (0, 0)))
    return _combine(starts, ys, pos_t, x1, mod, row0, ns, n, cap, tb, mod_group)


def _rope_tables(b, t, tc):
    n_freq = DIFF_HEAD_DIM // 4
    inv_freq = ROPE_BASE ** (-jnp.arange(n_freq, dtype=F32) / n_freq)
    pos = jnp.arange(t)
    row = (pos // GRID_W).astype(F32)
    col = (pos % GRID_W).astype(F32)
    ang = jnp.concatenate([row[:, None] * inv_freq, col[:, None] * inv_freq], axis=-1)
    cos, sin = jnp.cos(ang), jnp.sin(ang)
    cos_seg = jnp.concatenate([cos, cos], axis=-1)
    sin_seg = jnp.concatenate([-sin, sin], axis=-1)
    cos_t = jnp.tile(cos_seg, (b, LANES // DIFF_HEAD_DIM))
    sin_t = jnp.tile(sin_seg, (b, LANES // DIFF_HEAD_DIM))
    cos_t = jnp.concatenate([cos_t, jnp.ones((b * tc, LANES), F32)])
    sin_t = jnp.concatenate([sin_t, jnp.zeros((b * tc, LANES), F32)])
    return cos_t, sin_t


def kernel(x, c, ctx, c_ctx, w_mod, b_mod, w_in, attn_q_gain, attn_k_gain, attn_lambda, attn_out_gain,
           s5_lam_re, s5_lam_im, s5_log_step, s5_b_re, s5_b_im, s5_c_re, s5_c_im, s5_d, s5_w_glu,
           mlstm_conv_w, mlstm_conv_b, mlstm_i_bias, mlstm_f_bias,
           w_branch, w_out, w_router, w_exp_gate, w_exp_up, w_exp_down):
    b, t, d = x.shape
    tc = ctx.shape[1]
    n_layers = w_mod.shape[0]
    dims = (b, t, tc, d)
    assert b + 1 <= 8 and t % 512 == 0 and tc % 256 == 0 and (b * tc) % 512 == 0

    xa = jnp.concatenate([x.reshape(b * t, d), ctx.reshape(b * tc, d)])
    cvec = jnp.zeros((8, d), F32).at[:b].set(c).at[b].set(c_ctx)
    mod_all = _mod_vectors(cvec, w_mod, b_mod).reshape(n_layers, 8, N_MOD, d)

    n_main = 8 * BRANCH_W
    wa = jnp.concatenate([w_in[:, :, :n_main], w_in[:, :, n_main + N_GATES:]], axis=2).astype(BF16)
    wg = jnp.pad(w_in[:, :, n_main:n_main + N_GATES], ((0, 0), (0, 0), (0, LANES - N_GATES))).astype(BF16)
    cos_t, sin_t = _rope_tables(b, t, tc)
    seg = jnp.arange(BRANCH_W) // DIFF_HEAD_DIM
    seg_ones = (seg[:, None] == seg[None, :]).astype(BF16)
    n_seg = BRANCH_W // DIFF_HEAD_DIM
    gq = jnp.tile(attn_q_gain, (1, n_seg))[:, None, :] * (DIFF_HEAD_DIM ** -0.5 * math.log2(math.e))
    gk = jnp.tile(attn_k_gain, (1, n_seg))[:, None, :]
    conv_w = jnp.pad(mlstm_conv_w, ((0, 0), (0, 8 - CONV_K), (0, 0)))
    gate_bias = jnp.stack([mlstm_i_bias, mlstm_f_bias], axis=2).reshape(n_layers, 1, N_GATES)
    gate_bias = jnp.pad(gate_bias, ((0, 0), (0, 0), (0, LANES - N_GATES)))
    wr_pad = jnp.pad(w_router, ((0, 0), (0, 0), (0, LANES - N_EXPERTS)))
    wr_t = jnp.swapaxes(w_router, 1, 2)
    n_chunks = (t + tc) // S5_CHUNK
    n_levels = _s5_levels(n_chunks)

    for l in range(n_layers):
        with_ctx = l != n_layers - 1
        lam_init = 0.8 - 0.6 * math.exp(-0.3 * l)
        mod = mod_all[l]
        pa, pg = _project(xa, mod, wa[l], wg[l], dims)
        qh, kh, mq, mk = _prepare(pa, cos_t, sin_t, gq[l], gk[l], seg_ones, conv_w[l],
                                  mlstm_conv_b[l][None, :], dims)
        og = attn_out_gain[l][None, :]
        bound = 1.01 * DIFF_HEAD_DIM * jnp.max(jnp.abs(gq[l])) * jnp.max(jnp.abs(gk[l]))
        ya_l = _attention(qh, kh, pa, attn_lambda[l], og, bound, lam_init, dims, ctx_queries=False)
        if with_ctx:
            ya_c = _attention(qh, kh, pa, attn_lambda[l], og, bound, lam_init, dims, ctx_queries=True)
        else:
            ya_c = jnp.zeros((b * tc, BRANCH_W), BF16)
        ya = jnp.concatenate([ya_l, ya_c])
        mats = _s5_matrices(s5_lam_re[l], s5_lam_im[l], s5_log_step[l], s5_b_re[l], s5_b_im[l],
                            s5_c_re[l], s5_c_im[l], n_levels)
        y0, y1 = _s5_mixer(pa, mats, dims)
        hm = _mlstm(mq, mk, pa, pg, gate_bias[l], dims)
        x1, h2, aff, aff_t = _merge(ya, y0, y1, pa, hm, xa, mod, s5_d[l][None, :],
                                    s5_w_glu[l].astype(BF16), w_branch[l].astype(BF16),
                                    w_out[l].astype(BF16), wr_pad[l], wr_t[l], dims)
        wge, wue, wde = (w_exp_gate[l].astype(BF16), w_exp_up[l].astype(BF16), w_exp_down[l].astype(BF16))
        x2_l = _expert_choice(h2, aff, aff_t, x1, mod, wge, wue, wde, 0, b, t, lambda s: s)
        if with_ctx:
            x2_c = _expert_choice(h2, aff, aff_t, x1, mod, wge, wue, wde, b * t, b, tc, lambda s: b)
        else:
            x2_c = x1[b * t:]
        xa = jnp.concatenate([x2_l, x2_c])
    return xa[:b * t].reshape(b, t, d)
```

```python
import functools
import math

import jax
import jax.numpy as jnp
from jax import lax
from jax.experimental import pallas as pl
from jax.experimental.pallas import tpu as pltpu

F32 = jnp.float32
BF16 = jnp.bfloat16
HI = lax.Precision.HIGHEST

N_MOD = 6
NORM_EPS = 1e-6
GRID_W = 64
ROPE_BASE = 10000.0
N_HEADS = 4
DIFF_HEAD_DIM = 64
HEAD_W = 128
BRANCH_W = 512
S5_GROUPS = 32
S5_GROUP = 16
S5_STATE = 64
S5_CHUNK = 16
S5_SLAB = 8
S5_VMEM_LIMIT = 58 * 1024 * 1024
MLSTM_CHUNK = 128
CONV_K = 5
N_GATES = 16
N_EXPERTS = 16
CAPACITY_FACTOR = 2
LANES = 128
VMEM_LIMIT = 52 * 1024 * 1024
NEG_BIG = -1e30

SEG_Q, SEG_K, SEG_V, SEG_S5, SEG_MQ, SEG_MK, SEG_MV, SEG_MO, SEG_GATE = range(9)
PA_WIDTH = 8 * BRANCH_W + 3 * 1024


def _cparams(*sem):
    return pltpu.CompilerParams(dimension_semantics=sem, vmem_limit_bytes=VMEM_LIMIT)


def _mod_body(c_ref, w_ref, b_ref, o_ref):
    cv = c_ref[...]
    s = cv * jax.nn.sigmoid(cv)
    o_ref[0] = jnp.dot(s, w_ref[0], precision=HI, preferred_element_type=F32) + b_ref[0]


def _mod_vectors(cvec, w_mod, b_mod):
    n_layers, d, n = w_mod.shape
    tn = n // 4
    return pl.pallas_call(
        _mod_body,
        grid=(n_layers, n // tn),
        in_specs=[
            pl.BlockSpec((8, d), lambda l, j: (0, 0)),
            pl.BlockSpec((1, d, tn), lambda l, j: (l, 0, j)),
            pl.BlockSpec((1, 1, tn), lambda l, j: (l, 0, j)),
        ],
        out_specs=pl.BlockSpec((1, 8, tn), lambda l, j: (l, 0, j)),
        out_shape=jax.ShapeDtypeStruct((n_layers, 8, n), F32),
        compiler_params=_cparams("parallel", "parallel"),
        name="mod_vectors",
    )(cvec, w_mod, b_mod.reshape(n_layers, 1, n))


def _modulated_norm(x, mod, i_shift, i_scale):
    ms = jnp.mean(x * x, axis=-1, keepdims=True)
    xn = x * lax.rsqrt(ms + NORM_EPS)
    return xn * (1.0 + mod[i_scale:i_scale + 1, :]) + mod[i_shift:i_shift + 1, :]


def _proj_body(x_ref, m_ref, w_ref, wg_ref, pa_ref, pg_ref, *, tn):
    hb = _modulated_norm(x_ref[...], m_ref[0], 0, 1).astype(BF16)
    pg_ref[...] = jnp.dot(hb, wg_ref[...], preferred_element_type=F32)
    for c0 in range(0, pa_ref.shape[1], tn):
        pa_ref[:, c0:c0 + tn] = jnp.dot(hb, w_ref[:, c0:c0 + tn], preferred_element_type=F32).astype(BF16)


def _group_of_block(i, blocks_per_sample, n_samples):
    return jnp.minimum(i // blocks_per_sample, n_samples)


def _project(xa, mod, wa, wg, dims):
    b, t, tc, d = dims
    r = xa.shape[0]
    tm = 512
    npa = wa.shape[1]
    once = pl.Buffered(1)
    return pl.pallas_call(
        functools.partial(_proj_body, tn=1024),
        grid=(r // tm,),
        in_specs=[
            pl.BlockSpec((tm, d), lambda i: (i, 0)),
            pl.BlockSpec((1, N_MOD, d), lambda i: (_group_of_block(i, t // tm, b), 0, 0)),
            pl.BlockSpec((d, npa), lambda i: (0, 0), pipeline_mode=once),
            pl.BlockSpec((d, LANES), lambda i: (0, 0), pipeline_mode=once),
        ],
        out_specs=[
            pl.BlockSpec((tm, npa), lambda i: (i, 0)),
            pl.BlockSpec((tm, LANES), lambda i: (i, 0)),
        ],
        out_shape=[
            jax.ShapeDtypeStruct((r, npa), BF16),
            jax.ShapeDtypeStruct((r, LANES), F32),
        ],
        compiler_params=_cparams("parallel"),
        name="in_proj",
    )(xa, mod, wa, wg)


def _qk_norm_rope(x_bf, gain, cosf, sinf, seg_ones, first_half):
    x = x_bf.astype(F32)
    x2 = x * x
    hi = x2.astype(BF16)
    lo = (x2 - hi.astype(F32)).astype(BF16)
    ss = (jnp.dot(hi, seg_ones, preferred_element_type=F32)
          + jnp.dot(lo, seg_ones, preferred_element_type=F32))
    xn = x * lax.rsqrt(ss * (1.0 / DIFF_HEAD_DIM) + NORM_EPS) * gain
    half = DIFF_HEAD_DIM // 2
    width = x.shape[1]
    nxt = pltpu.roll(xn, width - half, 1)
    prv = pltpu.roll(xn, half, 1)
    partner = jnp.where(first_half, nxt, prv)
    return xn * cosf + partner * sinf


def _short_conv_silu(prev_ref, cur_ref, next_ref, w, bias, at_start, at_end, out_scale):
    tp = cur_ref.shape[0]
    prev = prev_ref[...].astype(F32)[8:16]
    nxt = next_ref[...].astype(F32)[0:8]
    prev = jnp.where(at_start, 0.0, prev)
    nxt = jnp.where(at_end, 0.0, nxt)
    ext = jnp.concatenate([prev, cur_ref[...].astype(F32), nxt], axis=0)
    acc = bias
    for kk in range(CONV_K):
        off = 8 + kk - CONV_K // 2
        acc = acc + w[kk:kk + 1, :] * ext[off:off + tp]
    y = acc * jax.nn.sigmoid(acc)
    return y * out_scale


def _prep_body(q_ref, k_ref, mqp_ref, mq_ref, mqn_ref, mkp_ref, mk_ref, mkn_ref,
               cos_ref, sin_ref, gq_ref, gk_ref, so_ref, cw_ref, cb_ref,
               qo_ref, ko_ref, mqo_ref, mko_ref, *, b, t, tc, tp):
    cos4 = jnp.concatenate([cos_ref[...]] * 4, axis=1)
    sin4 = jnp.concatenate([sin_ref[...]] * 4, axis=1)
    lane = lax.broadcasted_iota(jnp.int32, (tp, BRANCH_W), 1)
    first_half = (lane % DIFF_HEAD_DIM) < (DIFF_HEAD_DIM // 2)
    seg_ones = so_ref[...]
    qo_ref[...] = _qk_norm_rope(q_ref[...], gq_ref[...], cos4, sin4, seg_ones, first_half).astype(BF16)
    ko_ref[...] = _qk_norm_rope(k_ref[...], gk_ref[...], cos4, sin4, seg_ones, first_half).astype(BF16)

    row0 = pl.program_id(0) * tp
    in_lat = row0 < b * t
    local = jnp.where(in_lat, row0 % t, (row0 - b * t) % tc)
    seq_len = jnp.where(in_lat, t, tc)
    at_start = local == 0
    at_end = local + tp == seq_len
    cw = cw_ref[...]
    cb = cb_ref[...]
    mqo_ref[...] = _short_conv_silu(mqp_ref, mq_ref, mqn_ref, cw[:, :BRANCH_W], cb[:, :BRANCH_W],
                                    at_start, at_end, 1.0).astype(BF16)
    mko_ref[...] = _short_conv_silu(mkp_ref, mk_ref, mkn_ref, cw[:, BRANCH_W:], cb[:, BRANCH_W:],
                                    at_start, at_end, HEAD_W ** -0.5).astype(BF16)


def _prepare(pa, cos_tab, sin_tab, gq, gk, seg_ones, conv_w, conv_b, dims):
    b, t, tc, d = dims
    r = pa.shape[0]
    tp = 256
    halo = 16
    hb = tp // halo
    last_halo = r // halo - 1

    def cur(seg):
        return pl.BlockSpec((tp, BRANCH_W), lambda i: (i, seg))

    def prev(seg):
        return pl.BlockSpec((halo, BRANCH_W), lambda i: (jnp.maximum(i * hb - 1, 0), seg))

    def nxt(seg):
        return pl.BlockSpec((halo, BRANCH_W), lambda i: (jnp.minimum((i + 1) * hb, last_halo), seg))

    full = lambda shape: pl.BlockSpec(shape, lambda i: (0, 0))
    out = jax.ShapeDtypeStruct((r, BRANCH_W), BF16)
    return pl.pallas_call(
        functools.partial(_prep_body, b=b, t=t, tc=tc, tp=tp),
        grid=(r // tp,),
        in_specs=[
            cur(SEG_Q), cur(SEG_K),
            prev(SEG_MQ), cur(SEG_MQ), nxt(SEG_MQ),
            prev(SEG_MK), cur(SEG_MK), nxt(SEG_MK),
            pl.BlockSpec((tp, LANES), lambda i: (i, 0)),
            pl.BlockSpec((tp, LANES), lambda i: (i, 0)),
            full((1, BRANCH_W)), full((1, BRANCH_W)),
            full((BRANCH_W, BRANCH_W)),
            full((8, 2 * BRANCH_W)), full((1, 2 * BRANCH_W)),
        ],
        out_specs=[pl.BlockSpec((tp, BRANCH_W), lambda i: (i, 0))] * 4,
        out_shape=[out, out, out, out],
        compiler_params=_cparams("parallel"),
        name="row_prep",
    )(pa, pa, pa, pa, pa, pa, pa, pa, cos_tab, sin_tab, gq, gk, seg_ones, conv_w, conv_b)


ATTN_HEADS_PER_STEP = 4


def _attn_body(lam_ref, og_ref, sh_ref, q_ref, k_ref, v_ref, kc_ref, vc_ref, o_ref,
               qm_ref, m_ref, l_ref, acc_ref, *, lam_init, has_ctx, nk, fixed_shift):
    kj = pl.program_id(3)
    heads = range(ATTN_HEADS_PER_STEP)
    head_lanes = lambda hh: slice(hh * HEAD_W, (hh + 1) * HEAD_W)

    def process(k_all, v_all):
        for hh in heads:
            kb, vb = k_all[:, head_lanes(hh)], v_all[:, head_lanes(hh)]
            for mi in range(2):
                si = 2 * hh + mi
                s = jnp.dot(kb, qm_ref[si], preferred_element_type=F32)
                if fixed_shift:
                    p = jnp.exp2(s - sh_ref[0:1, 0:1])
                    l_ref[si] += jnp.sum(p, axis=0, keepdims=True)
                    acc_ref[si] += lax.dot_general(vb, p.astype(BF16), (((0,), (0,)), ((), ())),
                                                   preferred_element_type=F32)
                else:
                    m_old = m_ref[si]
                    m_new = jnp.maximum(m_old, jnp.max(s, axis=0, keepdims=True))
                    alpha = jnp.exp2(m_old - m_new)
                    p = jnp.exp2(s - m_new)
                    l_ref[si] = alpha * l_ref[si] + jnp.sum(p, axis=0, keepdims=True)
                    pv = lax.dot_general(vb, p.astype(BF16), (((0,), (0,)), ((), ())),
                                         preferred_element_type=F32)
                    acc_ref[si] = alpha * acc_ref[si] + pv
                    m_ref[si] = m_new

    @pl.when(kj == 0)
    def _():
        for hh in heads:
            qt = q_ref[:, head_lanes(hh)].astype(F32).T.astype(BF16)
            row = lax.broadcasted_iota(jnp.int32, qt.shape, 0)
            zero = jnp.zeros_like(qt)
            qm_ref[2 * hh] = jnp.where(row < DIFF_HEAD_DIM, qt, zero)
            qm_ref[2 * hh + 1] = jnp.where(row >= DIFF_HEAD_DIM, qt, zero)
        m_ref[...] = jnp.full(m_ref.shape, NEG_BIG, F32)
        l_ref[...] = jnp.zeros(l_ref.shape, F32)
        acc_ref[...] = jnp.zeros(acc_ref.shape, F32)
        if has_ctx:
            process(kc_ref[...], vc_ref[...])

    process(k_ref[...], v_ref[...])

    @pl.when(kj == nk - 1)
    def _():
        lv = lam_ref[...]
        lam = (jnp.exp(jnp.sum(lv[0:1] * lv[1:2], keepdims=True))
               - jnp.exp(jnp.sum(lv[2:3] * lv[3:4], keepdims=True)) + lam_init)
        for hh in heads:
            o = (acc_ref[2 * hh] / l_ref[2 * hh]
                 - lam * (acc_ref[2 * hh + 1] / l_ref[2 * hh + 1]))
            ms = jnp.mean(o * o, axis=0, keepdims=True)
            o = o * lax.rsqrt(ms + NORM_EPS)
            o_ref[:, head_lanes(hh)] = (o.T * (og_ref[...] * (1.0 - lam_init))).astype(BF16)


MAX_FIXED_SHIFT = 48.0


def _attention(qh, kh, pa, lam_vecs, out_gain, score_bound, lam_init, dims, *, ctx_queries):
    shift = jnp.full((1, LANES), score_bound, F32)
    run = lambda fixed: _attention_call(qh, kh, pa, lam_vecs, out_gain, shift, lam_init, dims,
                                        ctx_queries=ctx_queries, fixed_shift=fixed)
    return lax.cond(score_bound <= MAX_FIXED_SHIFT, lambda: run(True), lambda: run(False))


def _attention_call(qh, kh, pa, lam_vecs, out_gain, shift, lam_init, dims, *, ctx_queries, fixed_shift):
    b, t, tc, d = dims
    v_col = SEG_V * (BRANCH_W // HEAD_W)
    ctx_blk0 = (b * t) // tc
    if ctx_queries:
        tq = tk = tc
        nq, nk = 1, 1
        q_row = lambda bb, qi: ctx_blk0 + bb
        k_row = lambda bb, kj: ctx_blk0 + bb
        n_rows = b * tc
        o_row = lambda bb, qi: bb
    else:
        tq = min(2048, t)
        tk = min(1024, t)
        nq, nk = t // tq, t // tk
        q_row = lambda bb, qi: bb * nq + qi
        k_row = lambda bb, kj: bb * nk + kj
        n_rows = b * t
        o_row = q_row
    body = functools.partial(_attn_body, lam_init=lam_init, has_ctx=not ctx_queries, nk=nk,
                             fixed_shift=fixed_shift)
    hp = ATTN_HEADS_PER_STEP
    wb = hp * HEAD_W
    v_blk = SEG_V * BRANCH_W // wb
    return pl.pallas_call(
        body,
        grid=(b, N_HEADS // hp, nq, nk),
        in_specs=[
            pl.BlockSpec((4, DIFF_HEAD_DIM), lambda bb, h, qi, kj: (0, 0)),
            pl.BlockSpec((1, HEAD_W), lambda bb, h, qi, kj: (0, 0)),
            pl.BlockSpec((1, LANES), lambda bb, h, qi, kj: (0, 0)),
            pl.BlockSpec((tq, wb), lambda bb, h, qi, kj: (q_row(bb, qi), h)),
            pl.BlockSpec((tk, wb), lambda bb, h, qi, kj: (k_row(bb, kj), h)),
            pl.BlockSpec((tk, wb), lambda bb, h, qi, kj: (k_row(bb, kj), v_blk + h)),
            pl.BlockSpec((tc, wb), lambda bb, h, qi, kj: (ctx_blk0 + bb, h)),
            pl.BlockSpec((tc, wb), lambda bb, h, qi, kj: (ctx_blk0 + bb, v_blk + h)),
        ],
        out_specs=pl.BlockSpec((tq, wb), lambda bb, h, qi, kj: (o_row(bb, qi), h)),
        out_shape=jax.ShapeDtypeStruct((n_rows, BRANCH_W), BF16),
        scratch_shapes=[
            pltpu.VMEM((2 * hp, HEAD_W, tq), BF16),
            pltpu.VMEM((2 * hp, 1, tq), F32),
            pltpu.VMEM((2 * hp, 1, tq), F32),
            pltpu.VMEM((2 * hp, HEAD_W, tq), F32),
        ],
        compiler_params=_cparams("parallel", "parallel", "parallel", "arbitrary"),
        name=("diff_attn_ctx" if ctx_queries else "diff_attn") + ("_fixed" if fixed_shift else ""),
    )(lam_vecs, out_gain, shift, qh, kh, pa, kh, pa)


def _s5_matrices(lam_re, lam_im, log_step, b_re, b_im, c_re, c_im, n_levels):
    ll, hg, pp, gg = S5_CHUNK, S5_GROUP, S5_STATE, S5_GROUPS
    dt = jnp.exp(log_step)[:, :, None]
    lr, li = lam_re * dt, lam_im * dt

    def a_pow(tau):
        tau = tau.astype(F32)[:, None, None, None]
        mag = jnp.exp(lr * tau)
        return mag * jnp.cos(li * tau), mag * jnp.sin(li * tau)

    ar1, ai1 = a_pow(jnp.ones((1,)))
    nr, ni = ar1[0] - 1.0, ai1[0]
    den = lam_re * lam_re + lam_im * lam_im
    f_re = (nr * lam_re + ni * lam_im) / den
    f_im = (ni * lam_re - nr * lam_im) / den
    bb_re = f_re[..., None] * b_re - f_im[..., None] * b_im
    bb_im = f_re[..., None] * b_im + f_im[..., None] * b_re

    ar, ai = a_pow(jnp.arange(ll + 1))
    ca_re = c_re[None] * ar[:, :, :, None, :] - c_im[None] * ai[:, :, :, None, :]
    ca_im = c_re[None] * ai[:, :, :, None, :] + c_im[None] * ar[:, :, :, None, :]
    kk = (jnp.einsum('tdgop,dgph->tdgoh', ca_re, bb_re, precision=HI)
          - jnp.einsum('tdgop,dgph->tdgoh', ca_im, bb_im, precision=HI))
    ns, sl = gg // S5_SLAB, S5_SLAB
    n_state = sl * pp
    width = ll * LANES
    jj = jnp.arange(ll)
    grp = jnp.arange(sl)
    same_go = (grp[:, None] == jnp.arange(LANES)[None, :] // hg).astype(F32)
    same_gp = (grp[:, None] == jnp.arange(n_state)[None, :] // pp).astype(F32)

    kt = kk[:ll].transpose(1, 0, 2, 4, 3).reshape(2, ll, ns, sl, hg, hg)
    kt = kt.transpose(0, 2, 1, 4, 3, 5).reshape(2, ns, ll, hg, LANES)
    kbd = (kt[:, :, :, None] * same_go[None, None, None, :, None, :]).reshape(2, ns, ll, LANES, LANES)

    pw = jnp.stack([ll - 1 - jj, jj])
    sel = lambda a: jnp.stack([a[pw[0], 0], a[pw[1], 1]]).reshape(2, ll, ns, n_state).transpose(0, 2, 1, 3)
    s_re, s_im = sel(ar), sel(ai)
    lane_bb = lambda m: m.reshape(2, ns, sl, pp, hg).transpose(0, 1, 4, 2, 3).reshape(2, ns, hg, n_state)
    t_re, t_im = lane_bb(bb_re), lane_bb(bb_im)

    def in_slab(a, bmat, c, dmat, sign):
        v = a[:, :, :, None, :] * bmat[:, :, None, :, :] + sign * c[:, :, :, None, :] * dmat[:, :, None, :, :]
        v = v[:, :, :, None] * same_gp[None, None, None, :, None, :]
        return v.reshape(2, ns, width, n_state)

    po = jnp.stack([jj + 1, ll - jj])

    def out_slab_t(a):
        m = jnp.stack([a[po[0], 0], a[po[1], 1]])
        m = m.reshape(2, ll, ns, sl, hg, pp).transpose(0, 2, 1, 4, 3, 5).reshape(2, ns, ll, hg, n_state)
        v = m[:, :, :, None] * same_gp[None, None, None, :, None, :]
        return v.reshape(2, ns, width, n_state)

    lev = (ll * (2 ** jnp.arange(n_levels))).astype(F32)
    alr, ali = a_pow(lev)
    pad_lev = (-n_levels) % 8

    def lev_slab(a):
        a = a.transpose(1, 0, 2, 3).reshape(2, n_levels, ns, n_state).transpose(0, 2, 1, 3)
        return jnp.pad(a, ((0, 0), (0, 0), (0, pad_lev), (0, 0)))

    return dict(kbd=kbd.astype(BF16),
                in_re=in_slab(s_re, t_re, s_im, t_im, -1.0).astype(BF16),
                in_im=in_slab(s_re, t_im, s_im, t_re, 1.0).astype(BF16),
                out_re_t=out_slab_t(ca_re).astype(BF16), out_im_t=out_slab_t(-ca_im).astype(BF16),
                al_re=lev_slab(alr), al_im=lev_slab(ali))


def _s5_body(ul_ref, uc_ref, kbd_ref, inr_ref, ini_ref, outr_ref, outi_ref, alr_ref, ali_ref,
             yl_ref, yc_ref, sr_ref, si_ref, w_ref, *, nlat, nctx, pad, n_levels, rev):
    ll = S5_CHUNK
    nc = nlat + nctx

    @pl.when(pl.program_id(1) == 0)
    def _():
        zero = jnp.zeros((LANES, LANES), BF16)
        for j in range(ll):
            for i in range(ll):
                lag = j - i if rev else i - j
                w_ref[j * LANES:(j + 1) * LANES, i * LANES:(i + 1) * LANES] = (
                    kbd_ref[0, 0, lag] if lag >= 0 else zero)

    cat = lambda ref: jnp.concatenate([ref[j] for j in range(ll)], axis=1)
    ulat, uctx = cat(ul_ref), cat(uc_ref)
    u = jnp.concatenate([ulat, uctx] if rev else [uctx, ulat], axis=0)
    lat0, ctx0 = (0, nlat) if rev else (nctx, 0)
    lo = 0 if rev else pad
    zero0 = nc if rev else 0
    zeros = jnp.zeros((pad, sr_ref.shape[1]), F32)
    sr_ref[zero0:zero0 + pad, :] = zeros
    si_ref[zero0:zero0 + pad, :] = zeros
    sr_ref[lo:lo + nc, :] = jnp.dot(u, inr_ref[0, 0], preferred_element_type=F32)
    si_ref[lo:lo + nc, :] = jnp.dot(u, ini_ref[0, 0], preferred_element_type=F32)
    for lev in range(n_levels):
        dd = 1 << lev
        src = lo + dd if rev else lo - dd
        a_r = alr_ref[0, 0, lev:lev + 1, :]
        a_i = ali_ref[0, 0, lev:lev + 1, :]
        cr, ci = sr_ref[lo:lo + nc, :], si_ref[lo:lo + nc, :]
        pr, pi = sr_ref[src:src + nc, :], si_ref[src:src + nc, :]
        sr_ref[lo:lo + nc, :] = cr + a_r * pr - a_i * pi
        si_ref[lo:lo + nc, :] = ci + a_r * pi + a_i * pr
    ent = lo + 1 if rev else lo - 1
    er = sr_ref[ent:ent + nc, :].astype(BF16)
    ei = si_ref[ent:ent + nc, :].astype(BF16)
    nt = (((1,), (1,)), ((), ()))
    for ib in range(ll // 2):
        cols = slice(ib * 2 * LANES, (ib + 1) * 2 * LANES)
        y = (jnp.dot(u, w_ref[:, cols], preferred_element_type=F32)
             + lax.dot_general(er, outr_ref[0, 0, cols, :], nt, preferred_element_type=F32)
             + lax.dot_general(ei, outi_ref[0, 0, cols, :], nt, preferred_element_type=F32)).astype(BF16)
        for k in range(2):
            yl_ref[2 * ib + k] = y[lat0:lat0 + nlat, k * LANES:(k + 1) * LANES]
            yc_ref[2 * ib + k] = y[ctx0:ctx0 + nctx, k * LANES:(k + 1) * LANES]


def _s5_levels(nc):
    return max(1, (nc - 1).bit_length())


def _s5_scan(u3, mats, dims, *, rev):
    b, t, tc, d = dims
    ll = S5_CHUNK
    nlat, nctx = t // ll, tc // ll
    n_levels = _s5_levels(nlat + nctx)
    pad = max(8, 1 << (n_levels - 1))
    ns = S5_GROUPS // S5_SLAB
    n_state = S5_SLAB * S5_STATE
    width = ll * LANES
    lev_rows = mats["al_re"].shape[2]
    once = pl.Buffered(1)
    dd = int(rev)
    per_slab = lambda shape: pl.BlockSpec((1, 1) + shape, lambda s, bb: (dd, s) + (0,) * len(shape),
                                          pipeline_mode=once)
    return pl.pallas_call(
        functools.partial(_s5_body, nlat=nlat, nctx=nctx, pad=pad, n_levels=n_levels, rev=rev),
        grid=(ns, b),
        in_specs=[
            pl.BlockSpec((ll, nlat, LANES), lambda s, bb: (0, bb, s), pipeline_mode=once),
            pl.BlockSpec((ll, nctx, LANES), lambda s, bb: (0, b * t // tc + bb, s)),
            per_slab((ll, LANES, LANES)),
            per_slab((width, n_state)), per_slab((width, n_state)),
            per_slab((width, n_state)), per_slab((width, n_state)),
            per_slab((lev_rows, n_state)), per_slab((lev_rows, n_state)),
        ],
        out_specs=[
            pl.BlockSpec((ll, nlat, LANES), lambda s, bb: (0, bb, s)),
            pl.BlockSpec((ll, nctx, LANES), lambda s, bb: (0, bb, s)),
        ],
        out_shape=[
            jax.ShapeDtypeStruct((ll, b * nlat, BRANCH_W), BF16),
            jax.ShapeDtypeStruct((ll, b * nctx, BRANCH_W), BF16),
        ],
        scratch_shapes=[pltpu.VMEM((pad + nlat + nctx, n_state), F32)] * 2
        + [pltpu.VMEM((width, width), BF16)],
        compiler_params=pltpu.CompilerParams(dimension_semantics=("parallel", "arbitrary"),
                                             vmem_limit_bytes=S5_VMEM_LIMIT),
        name="s5_scan_bwd" if rev else "s5_scan_fwd",
    )(u3, u3, mats["kbd"], mats["in_re"], mats["in_im"], mats["out_re_t"], mats["out_im_t"],
      mats["al_re"], mats["al_im"])


def _s5_mixer(pa, mats, dims):
    r = pa.shape[0]
    ll = S5_CHUNK
    u = pa[:, SEG_S5 * BRANCH_W:(SEG_S5 + 1) * BRANCH_W]
    u3 = u.reshape(r // ll, ll, BRANCH_W).transpose(1, 0, 2)
    ys = []
    for dd in range(2):
        yl, yc = _s5_scan(u3, mats, dims, rev=bool(dd))
        ys.append(jnp.concatenate([yl, yc], axis=1).transpose(1, 0, 2).reshape(r, BRANCH_W))
    return ys


def _log_sigmoid(x):
    return -(jnp.maximum(-x, 0.0) + jnp.log1p(jnp.exp(-jnp.abs(x))))


def _mlstm_body(q_ref, k_ref, v_ref, g_ref, gb_ref, h_ref, c_ref, n_ref, m_ref):
    dd = pl.program_id(1)
    fwd = dd == 0

    @pl.when(pl.program_id(2) == 0)
    def _():
        c_ref[...] = jnp.zeros(c_ref.shape, F32)
        n_ref[...] = jnp.zeros(n_ref.shape, F32)
        m_ref[...] = jnp.zeros(m_ref.shape, F32)

    ll = MLSTM_CHUNK
    row = lax.broadcasted_iota(jnp.int32, (ll, ll), 0)
    col = lax.broadcasted_iota(jnp.int32, (ll, ll), 1)
    sign = jnp.where(fwd, 1, -1)
    before = (col - row) * sign >= 0
    before_f = jnp.where(before, 1.0, 0.0)
    after_f = jnp.where((row - col) * sign >= 0, 1.0, 0.0)

    g = g_ref[...] + gb_ref[...]
    lf = _log_sigmoid(g)
    g_t = g.T
    lf_t = lf.T
    cum_cols = jnp.dot(after_f, lf, precision=HI, preferred_element_type=F32)
    cum_rows = jnp.dot(lf_t, before_f, precision=HI, preferred_element_type=F32)
    nt = (((1,), (1,)), ((), ()))
    outs = []
    for h in range(N_HEADS):
        def pick_col(a, base):
            return jnp.where(fwd, a[:, base + h:base + h + 1], a[:, 8 + base + h:8 + base + h + 1])

        def pick_row(a, base):
            return jnp.where(fwd, a[base + h:base + h + 1, :], a[8 + base + h:8 + base + h + 1, :])

        ig_col, cum_col = pick_col(g, 0), pick_col(cum_cols, 4)
        ig_row, lf_row, cum_row = pick_row(g_t, 0), pick_row(lf_t, 4), pick_row(cum_rows, 4)
        gtot = jnp.sum(lf_row, axis=1, keepdims=True)

        hs = slice(h * HEAD_W, (h + 1) * HEAD_W)
        q = q_ref[:, hs]
        k = k_ref[:, hs]
        v_t = v_ref[:, hs].astype(F32).T.astype(BF16)
        c0 = c_ref[h]
        n0 = n_ref[h]
        m0 = m_ref[h][:, 0:1]

        dmat = jnp.where(before, cum_row + (ig_col - cum_col), NEG_BIG)
        inter = cum_row + m0
        m_t = jnp.maximum(inter, jnp.max(dmat, axis=0, keepdims=True))
        pm = jnp.exp(dmat - m_t)
        ei = jnp.exp(inter - m_t)
        kq = lax.dot_general(k, q, nt, preferred_element_type=F32)
        wq = pm * kq
        cq = lax.dot_general(c0.astype(BF16), q, nt, preferred_element_type=F32)
        num = jnp.dot(v_t, wq.astype(BF16), preferred_element_type=F32) + ei * cq
        nq = lax.dot_general(jnp.broadcast_to(n0, (8, HEAD_W)).astype(BF16), q, nt,
                             preferred_element_type=F32)[0:1]
        den = jnp.sum(wq, axis=0, keepdims=True) + ei * nq
        outs.append((num / jnp.maximum(jnp.abs(den), jnp.exp(-m_t))).T)

        w_row = gtot - cum_row + ig_row
        mw = jnp.max(w_row, axis=1, keepdims=True)
        ew = jnp.exp(w_row - mw)
        vw = (v_t.astype(F32) * ew).astype(BF16)
        kv = jnp.dot(vw, k, preferred_element_type=F32)
        ks = jnp.dot(jnp.broadcast_to(ew, (8, ll)).astype(BF16), k, preferred_element_type=F32)[0:1]
        m_new = jnp.maximum(gtot + m0, mw)
        a = jnp.exp(gtot + m0 - m_new)
        e = jnp.exp(mw - m_new)
        c_ref[h] = a * c0 + e * kv
        n_ref[h] = a * n0 + e * ks
        m_ref[h] = jnp.broadcast_to(m_new, (1, LANES))
    h_ref[0] = jnp.concatenate(outs, axis=1).astype(BF16)


def _mlstm(mq, mk, pa, pg, gate_bias, dims):
    b, t, tc, d = dims
    r = pa.shape[0]
    ll = MLSTM_CHUNK
    nctx, nlat = tc // ll, t // ll
    ctx0 = (b * t) // ll

    def rb(bb, dd, c):
        is_ctx = c < nctx
        cc = jnp.where(dd == 0, c, nctx - 1 - c)
        cl = jnp.where(dd == 0, c - nctx, nlat - 1 - (c - nctx))
        return jnp.where(is_ctx, ctx0 + bb * nctx + cc, bb * nlat + cl)

    return pl.pallas_call(
        _mlstm_body,
        grid=(b, 2, nctx + nlat),
        in_specs=[
            pl.BlockSpec((ll, BRANCH_W), lambda bb, dd, c: (rb(bb, dd, c), 0)),
            pl.BlockSpec((ll, BRANCH_W), lambda bb, dd, c: (rb(bb, dd, c), 0)),
            pl.BlockSpec((ll, BRANCH_W), lambda bb, dd, c: (rb(bb, dd, c), SEG_MV)),
            pl.BlockSpec((ll, LANES), lambda bb, dd, c: (rb(bb, dd, c), 0)),
            pl.BlockSpec((1, LANES), lambda bb, dd, c: (0, 0)),
        ],
        out_specs=pl.BlockSpec((1, ll, BRANCH_W), lambda bb, dd, c: (dd, rb(bb, dd, c), 0)),
        out_shape=jax.ShapeDtypeStruct((2, r, BRANCH_W), BF16),
        scratch_shapes=[
            pltpu.VMEM((N_HEADS, HEAD_W, HEAD_W), F32),
            pltpu.VMEM((N_HEADS, 1, HEAD_W), F32),
            pltpu.VMEM((N_HEADS, 1, LANES), F32),
        ],
        compiler_params=_cparams("parallel", "parallel", "arbitrary"),
        name="mlstm",
    )(mq, mk, pa, pg, gate_bias)


def _merge_body(ya_ref, y0_ref, y1_ref, u_ref, h0_ref, h1_ref, mo_ref, ga_ref, gb_ref, gc_ref, x_ref, m_ref,
                d_ref, wglu_ref, wb_ref, wo_ref, wr_ref, wrt_ref,
                x1_ref, h2_ref, aff_ref, afft_ref):
    mod = m_ref[0]
    ys = d_ref[...] * u_ref[...].astype(F32) + y0_ref[...].astype(F32) + y1_ref[...].astype(F32)
    gl = jax.nn.gelu(ys)
    yb = gl * jax.nn.sigmoid(jnp.dot(gl.astype(BF16), wglu_ref[...], preferred_element_type=F32))
    yc = (h0_ref[0].astype(F32) + h1_ref[0].astype(F32)) * jax.nn.sigmoid(mo_ref[...].astype(F32))
    gate = lambda ref: jax.nn.sigmoid(ref[...].astype(F32))
    mixed = (gate(ga_ref) * jnp.dot(ya_ref[...], wb_ref[0], preferred_element_type=F32)
             + gate(gb_ref) * jnp.dot(yb.astype(BF16), wb_ref[1], preferred_element_type=F32)
             + gate(gc_ref) * jnp.dot(yc.astype(BF16), wb_ref[2], preferred_element_type=F32))
    out = jnp.dot(mixed.astype(BF16), wo_ref[...], preferred_element_type=F32)
    x1 = x_ref[...] + mod[2:3, :] * out
    x1_ref[...] = x1
    h2 = _modulated_norm(x1, mod, 3, 4)
    h2_ref[...] = h2.astype(BF16)
    logits = jnp.dot(h2, wr_ref[...], precision=HI, preferred_element_type=F32)
    lane = lax.broadcasted_iota(jnp.int32, logits.shape, 1)
    logits = jnp.where(lane < N_EXPERTS, logits, NEG_BIG)
    ex = jnp.exp(logits - jnp.max(logits, axis=1, keepdims=True))
    aff_ref[...] = ex / jnp.sum(ex, axis=1, keepdims=True)
    lt = lax.dot_general(wrt_ref[...], h2, (((1,), (1,)), ((), ())), precision=HI,
                         preferred_element_type=F32)
    et = jnp.exp(lt - jnp.max(lt, axis=0, keepdims=True))
    afft_ref[...] = et / jnp.sum(et, axis=0, keepdims=True)


def _merge(ya, y0, y1, pa, hm, xa, mod, s5_d, w_glu, w_branch, w_out, wr_pad, wr_t, dims):
    b, t, tc, d = dims
    r = xa.shape[0]
    tm = 256
    gseg = SEG_GATE * BRANCH_W // d
    rowblk = lambda width, col=0: pl.BlockSpec((tm, width), lambda i: (i, col))
    full2 = lambda shape: pl.BlockSpec(shape, lambda i: (0, 0))
    return pl.pallas_call(
        _merge_body,
        grid=(r // tm,),
        in_specs=[
            rowblk(BRANCH_W), rowblk(BRANCH_W), rowblk(BRANCH_W),
            rowblk(BRANCH_W, SEG_S5),
            pl.BlockSpec((1, tm, BRANCH_W), lambda i: (0, i, 0)),
            pl.BlockSpec((1, tm, BRANCH_W), lambda i: (1, i, 0)),
            rowblk(BRANCH_W, SEG_MO),
            rowblk(d, gseg), rowblk(d, gseg + 1), rowblk(d, gseg + 2),
            rowblk(d),
            pl.BlockSpec((1, N_MOD, d), lambda i: (_group_of_block(i, t // tm, b), 0, 0)),
            full2((1, BRANCH_W)),
            full2((BRANCH_W, BRANCH_W)),
            pl.BlockSpec((3, BRANCH_W, d), lambda i: (0, 0, 0)),
            full2((d, d)),
            full2((d, LANES)),
            full2((N_EXPERTS, d)),
        ],
        out_specs=[
            rowblk(d), rowblk(d), rowblk(LANES),
            pl.BlockSpec((N_EXPERTS, tm), lambda i: (0, i)),
        ],
        out_shape=[
            jax.ShapeDtypeStruct((r, d), F32),
            jax.ShapeDtypeStruct((r, d), BF16),
            jax.ShapeDtypeStruct((r, LANES), F32),
            jax.ShapeDtypeStruct((N_EXPERTS, r), F32),
        ],
        compiler_params=_cparams("parallel"),
        name="merge",
    )(ya, y0, y1, pa, hm, hm, pa, pa, pa, pa, xa, mod, s5_d, w_glu, w_branch, w_out, wr_pad, wr_t)


def _route_body(a_ref, tri_ref, low_ref, pos_ref, offs_ref, *, cap):
    a = a_ref[0]
    e, nb, _ = a.shape
    bits = pltpu.bitcast(a, jnp.int32)

    def count(mask):
        c = jnp.sum(jnp.where(mask, 1.0, 0.0), axis=2, keepdims=True)
        return jnp.sum(c, axis=1, keepdims=True)

    def step(i, thr):
        cand = thr | jnp.left_shift(jnp.int32(1), 30 - i)
        return jnp.where(count(bits >= cand) >= cap, cand, thr)

    thr = lax.fori_loop(0, 31, step, jnp.zeros((e, 1, 1), jnp.int32))
    gt = bits > thr
    eq = bits == thr
    need = cap - count(gt)

    tri = tri_ref[...]
    low = low_ref[...]

    def exclusive_prefix(x):
        x2 = x.reshape(e * nb, LANES).astype(BF16)
        incl = jnp.dot(x2, tri, preferred_element_type=F32)
        before = jnp.sum(jnp.dot(low, x2, preferred_element_type=F32), axis=1, keepdims=True)
        return (incl - x2.astype(F32) + before).reshape(e, nb, LANES), before.reshape(e, nb, 1)

    eq_rank, _ = exclusive_prefix(jnp.where(eq, 1.0, 0.0))
    sel = gt | (eq & (eq_rank < need))
    pos, before = exclusive_prefix(jnp.where(sel, 1.0, 0.0))
    pos_ref[0] = jnp.where(sel, pos, -1.0)
    offs_ref[0] = before.astype(jnp.int32)


def _route(aff3, cap):
    ns, e, nb, _ = aff3.shape
    i = jnp.arange(LANES)
    tri = (i[:, None] <= i[None, :]).astype(BF16)
    r = jnp.arange(e * nb)
    low = ((r[:, None] // nb == r[None, :] // nb) & (r[None, :] < r[:, None])).astype(BF16)
    return pl.pallas_call(
        functools.partial(_route_body, cap=cap),
        grid=(ns,),
        in_specs=[
            pl.BlockSpec((1, e, nb, LANES), lambda s: (s, 0, 0, 0)),
            pl.BlockSpec((LANES, LANES), lambda s: (0, 0)),
            pl.BlockSpec((e * nb, e * nb), lambda s: (0, 0)),
        ],
        out_specs=[
            pl.BlockSpec((1, e, nb, LANES), lambda s: (s, 0, 0, 0)),
            pl.BlockSpec((1, e, nb, 1), lambda s: (s, 0, 0, 0)),
        ],
        out_shape=[
            jax.ShapeDtypeStruct((ns, e, nb, LANES), F32),
            jax.ShapeDtypeStruct((ns, e, nb, 1), jnp.int32),
        ],
        compiler_params=_cparams("parallel"),
        name="route",
    )(aff3, tri, low)


SLOT_ALIGN = 16


DISPATCH_EXPERTS = 2


def _dispatch_body(st_ref, h_ref, a_ref, p_ref, o_ref, oa_ref, *, tb, win, nj, sub):
    ns, eg, j = pl.program_id(0), pl.program_id(1), pl.program_id(2)

    @pl.when(j == 0)
    def _():
        o_ref[...] = jnp.zeros(o_ref.shape, o_ref.dtype)
        oa_ref[...] = jnp.zeros(oa_ref.shape, oa_ref.dtype)

    lane = lax.broadcasted_iota(jnp.int32, (tb, LANES), 1)
    for q in range(DISPATCH_EXPERTS):
        e = eg * DISPATCH_EXPERTS + q
        for k in range(sub):
            start = st_ref[(ns * N_EXPERTS + e) * (nj + 1) + j * sub + k]
            base = pl.multiple_of((start // SLOT_ALIGN) * SLOT_ALIGN, SLOT_ALIGN)
            slot = (base + lax.broadcasted_iota(jnp.int32, (win, tb), 0)).astype(F32)
            pos = p_ref[0, q, 0, :, k * tb:(k + 1) * tb]
            onehot = jnp.where(slot == pos, 1.0, 0.0).astype(BF16)
            rows = jnp.dot(onehot, h_ref[k * tb:(k + 1) * tb, :], preferred_element_type=F32)
            cur = o_ref[0, q, pl.ds(base, win), :]
            o_ref[0, q, pl.ds(base, win), :] = cur + rows.astype(o_ref.dtype)
            aff = jnp.where(lane == e, a_ref[k * tb:(k + 1) * tb, :], 0.0)
            hi = aff.astype(BF16)
            rest = aff - hi.astype(F32)
            mid = rest.astype(BF16)
            low = (rest - mid.astype(F32)).astype(BF16)
            arows = (jnp.dot(onehot, hi, preferred_element_type=F32)
                     + jnp.dot(onehot, mid, preferred_element_type=F32)
                     + jnp.dot(onehot, low, preferred_element_type=F32))
            oa_ref[0, q, pl.ds(base, win), :] = oa_ref[0, q, pl.ds(base, win), :] + arows


def _dispatch(starts, h2, aff, pos, row0, ns, n, cap, tb):
    d = h2.shape[1]
    nj = n // tb
    sub = min(4, nj)
    njs = nj // sub
    win = tb + SLOT_ALIGN
    capp = cap + win
    blk0 = row0 // (sub * tb)
    de = DISPATCH_EXPERTS
    pos_rows = pos.reshape(ns, N_EXPERTS, njs, 1, sub * tb)
    grid_spec = pltpu.PrefetchScalarGridSpec(
        num_scalar_prefetch=1,
        grid=(ns, N_EXPERTS // de, njs),
        in_specs=[
            pl.BlockSpec((sub * tb, d), lambda s, e, j, st: (blk0 + s * njs + j, 0)),
            pl.BlockSpec((sub * tb, LANES), lambda s, e, j, st: (blk0 + s * njs + j, 0)),
            pl.BlockSpec((1, de, 1, 1, sub * tb), lambda s, e, j, st: (s, e, j, 0, 0)),
        ],
        out_specs=[
            pl.BlockSpec((1, de, capp, d), lambda s, e, j, st: (s, e, 0, 0)),
            pl.BlockSpec((1, de, capp, LANES), lambda s, e, j, st: (s, e, 0, 0)),
        ],
    )
    return pl.pallas_call(
        functools.partial(_dispatch_body, tb=tb, win=win, nj=nj, sub=sub),
        grid_spec=grid_spec,
        out_shape=[
            jax.ShapeDtypeStruct((ns, N_EXPERTS, capp, d), BF16),
            jax.ShapeDtypeStruct((ns, N_EXPERTS, capp, LANES), F32),
        ],
        compiler_params=_cparams("parallel", "parallel", "arbitrary"),
        name="dispatch",
    )(starts, h2, aff, pos_rows)


def _expert_body(x_ref, a_ref, wg_ref, wu_ref, wd_ref, y_ref, *, fc):
    x = x_ref[0, 0]
    f = wg_ref.shape[2]
    acc = jnp.zeros((x.shape[0], wd_ref.shape[2]), F32)
    for f0 in range(0, f, fc):
        g = jnp.dot(x, wg_ref[0, :, f0:f0 + fc], preferred_element_type=F32)
        u = jnp.dot(x, wu_ref[0, :, f0:f0 + fc], preferred_element_type=F32)
        hid = (g * jax.nn.sigmoid(g) * u).astype(BF16)
        acc = acc + jnp.dot(hid, wd_ref[0, f0:f0 + fc, :], preferred_element_type=F32)
    aff = jnp.sum(a_ref[0, 0], axis=1, keepdims=True)
    y_ref[0, 0] = (acc * aff).astype(BF16)


def _experts(xs, xa, w_gate, w_up, w_down, cap):
    ns, e, _, d = xs.shape
    f = w_gate.shape[2]
    ts = min(512, cap)
    return pl.pallas_call(
        functools.partial(_expert_body, fc=min(512, f)),
        grid=(e, ns, cap // ts),
        in_specs=[
            pl.BlockSpec((1, 1, ts, d), lambda ee, s, i: (s, ee, i, 0)),
            pl.BlockSpec((1, 1, ts, LANES), lambda ee, s, i: (s, ee, i, 0)),
            pl.BlockSpec((1, d, f), lambda ee, s, i: (ee, 0, 0)),
            pl.BlockSpec((1, d, f), lambda ee, s, i: (ee, 0, 0)),
            pl.BlockSpec((1, f, d), lambda ee, s, i: (ee, 0, 0)),
        ],
        out_specs=pl.BlockSpec((1, 1, ts, d), lambda ee, s, i: (s, ee, i, 0)),
        out_shape=jax.ShapeDtypeStruct((ns, e, cap, d), BF16),
        compiler_params=_cparams("parallel", "parallel", "parallel"),
        name="experts",
    )(xs, xa, w_gate, w_up, w_down)


COMBINE_EXPERTS = 8


def _combine_body(st_ref, *refs, sb, nj, nwin):
    y_refs = refs[:COMBINE_EXPERTS * nwin]
    p_ref, x_ref, m_ref, o_ref = refs[COMBINE_EXPERTS * nwin:]
    ns, j, eg = pl.program_id(0), pl.program_id(1), pl.program_id(2)

    @pl.when(eg == 0)
    def _():
        o_ref[...] = jnp.zeros(o_ref.shape, F32)

    tb = o_ref.shape[0]
    lane = lax.broadcasted_iota(jnp.int32, (tb, LANES), 1)
    slot_iota = lax.broadcasted_iota(jnp.int32, (tb, sb), 1)

    def gathered(k, w):
        e = eg * COMBINE_EXPERTS + k
        start = st_ref[(ns * N_EXPERTS + e) * (nj + 1) + j]
        end = st_ref[(ns * N_EXPERTS + e) * (nj + 1) + j + 1]
        first = (start // sb + w) * sb

        def rows():
            pos = jnp.sum(jnp.where(lane == e, p_ref[...], 0.0), axis=1, keepdims=True)
            onehot = jnp.where(pos == (first + slot_iota).astype(F32), 1.0, 0.0).astype(BF16)
            return jnp.dot(onehot, y_refs[k * nwin + w][0, 0], preferred_element_type=F32)
        return rows, end > first

    acc = gathered(0, 0)[0]()
    for k in range(1, COMBINE_EXPERTS):
        acc = acc + gathered(k, 0)[0]()
    o_ref[...] += acc
    for k in range(COMBINE_EXPERTS):
        for w in range(1, nwin):
            rows, used = gathered(k, w)

            @pl.when(used)
            def _():
                o_ref[...] += rows()

    @pl.when(eg == N_EXPERTS // COMBINE_EXPERTS - 1)
    def _():
        o_ref[...] = x_ref[...] + m_ref[0, 5:6, :] * o_ref[...]


def _combine(starts, ys, pos_t, x1, mod, row0, ns, n, cap, tb, mod_group):
    d = x1.shape[1]
    nj = n // tb
    span = min(tb, cap)
    sb = min(LANES, cap)
    nwin = span // sb + 1
    nsb = cap // sb
    blk0 = row0 // tb

    def window(k, w):
        def index(s, j, eg, st):
            e = eg * COMBINE_EXPERTS + k
            start = st[(s * N_EXPERTS + e) * (nj + 1) + j]
            end = st[(s * N_EXPERTS + e) * (nj + 1) + j + 1]
            blk = start // sb + w
            if w == 0:
                return (s, e, jnp.minimum(blk, nsb - 1), 0)
            used = end > blk * sb
            return (jnp.where(used, s, 0), jnp.where(used, e, 0), jnp.where(used, blk, 0), 0)
        return pl.BlockSpec((1, 1, sb, d), index)

    row = lambda s, j, eg, st: (blk0 + s * nj + j, 0)
    windows = [window(k, w) for k in range(COMBINE_EXPERTS) for w in range(nwin)]
    grid_spec = pltpu.PrefetchScalarGridSpec(
        num_scalar_prefetch=1,
        grid=(ns, nj, N_EXPERTS // COMBINE_EXPERTS),
        in_specs=windows + [
            pl.BlockSpec((tb, LANES), row),
            pl.BlockSpec((tb, d), row),
            pl.BlockSpec((1, N_MOD, d), lambda s, j, eg, st: (mod_group(s), 0, 0)),
        ],
        out_specs=pl.BlockSpec((tb, d), lambda s, j, eg, st: (s * nj + j, 0)),
    )
    return pl.pallas_call(
        functools.partial(_combine_body, sb=sb, nj=nj, nwin=nwin),
        grid_spec=grid_spec,
        out_shape=jax.ShapeDtypeStruct((ns * n, d), F32),
        compiler_params=_cparams("parallel", "parallel", "arbitrary"),
        name="combine",
    )(starts, *([ys] * len(windows)), pos_t, x1, mod)


def _expert_choice(h2, aff, aff_t, x1, mod, w_gate, w_up, w_down, row0, ns, n, mod_group):
    r, d = h2.shape
    e = N_EXPERTS
    cap = CAPACITY_FACTOR * n // e
    tb = min(256, n)
    n_pad = max(n, 8 * LANES)
    a = aff_t[:, row0:row0 + ns * n].reshape(e, ns, n).transpose(1, 0, 2)
    if n_pad > n:
        a = jnp.concatenate([a, jnp.full((ns, e, n_pad - n), -1.0, F32)], axis=2)
    pos, offs = _route(a.reshape(ns, e, n_pad // LANES, LANES), cap)
    pos = pos.reshape(ns, e, n_pad)[:, :, :n]
    starts = offs.reshape(ns, e, n_pad // LANES)[:, :, :n // LANES:tb // LANES]
    starts = jnp.concatenate([starts, jnp.full((ns, e, 1), cap, jnp.int32)], axis=2).reshape(-1)
    xs, xa = _dispatch(starts, h2, aff, pos, row0, ns, n, cap, tb)
    ys = _experts(xs, xa, w_gate, w_up, w_down, cap)
    pos_t = jnp.pad(pos.transpose(0, 2, 1).reshape(ns * n, e), ((0, 0), (0, LANES - e)), constant_values=-1.0)
    pos_t = jnp.pad(pos_t, ((row0, r - row0 - ns * n), (0, 0)))
    return _combine(starts, ys, pos_t, x1, mod, row0, ns, n, cap, tb, mod_group)


def _rope_tables(b, t, tc):
    n_freq = DIFF_HEAD_DIM // 4
    inv_freq = ROPE_BASE ** (-jnp.arange(n_freq, dtype=F32) / n_freq)
    pos = jnp.arange(t)
    row = (pos // GRID_W).astype(F32)
    col = (pos % GRID_W).astype(F32)
    ang = jnp.concatenate([row[:, None] * inv_freq, col[:, None] * inv_freq], axis=-1)
    cos, sin = jnp.cos(ang), jnp.sin(ang)
    cos_seg = jnp.concatenate([cos, cos], axis=-1)
    sin_seg = jnp.concatenate([-sin, sin], axis=-1)
    cos_t = jnp.tile(cos_seg, (b, LANES // DIFF_HEAD_DIM))
    sin_t = jnp.tile(sin_seg, (b, LANES // DIFF_HEAD_DIM))
    cos_t = jnp.concatenate([cos_t, jnp.ones((b * tc, LANES), F32)])
    sin_t = jnp.concatenate([sin_t, jnp.zeros((b * tc, LANES), F32)])
    return cos_t, sin_t


def kernel(x, c, ctx, c_ctx, w_mod, b_mod, w_in, attn_q_gain, attn_k_gain, attn_lambda, attn_out_gain,
           s5_lam_re, s5_lam_im, s5_log_step, s5_b_re, s5_b_im, s5_c_re, s5_c_im, s5_d, s5_w_glu,
           mlstm_conv_w, mlstm_conv_b, mlstm_i_bias, mlstm_f_bias,
           w_branch, w_out, w_router, w_exp_gate, w_exp_up, w_exp_down):
    b, t, d = x.shape
    tc = ctx.shape[1]
    n_layers = w_mod.shape[0]
    dims = (b, t, tc, d)
    assert b + 1 <= 8 and t % 512 == 0 and tc % 256 == 0 and (b * tc) % 512 == 0

    xa = jnp.concatenate([x.reshape(b * t, d), ctx.reshape(b * tc, d)])
    cvec = jnp.zeros((8, d), F32).at[:b].set(c).at[b].set(c_ctx)
    mod_all = _mod_vectors(cvec, w_mod, b_mod).reshape(n_layers, 8, N_MOD, d)

    n_main = 8 * BRANCH_W
    wa = jnp.concatenate([w_in[:, :, :n_main], w_in[:, :, n_main + N_GATES:]], axis=2).astype(BF16)
    wg = jnp.pad(w_in[:, :, n_main:n_main + N_GATES], ((0, 0), (0, 0), (0, LANES - N_GATES))).astype(BF16)
    cos_t, sin_t = _rope_tables(b, t, tc)
    seg = jnp.arange(BRANCH_W) // DIFF_HEAD_DIM
    seg_ones = (seg[:, None] == seg[None, :]).astype(BF16)
    n_seg = BRANCH_W // DIFF_HEAD_DIM
    gq = jnp.tile(attn_q_gain, (1, n_seg))[:, None, :] * (DIFF_HEAD_DIM ** -0.5 * math.log2(math.e))
    gk = jnp.tile(attn_k_gain, (1, n_seg))[:, None, :]
    conv_w = jnp.pad(mlstm_conv_w, ((0, 0), (0, 8 - CONV_K), (0, 0)))
    gate_bias = jnp.stack([mlstm_i_bias, mlstm_f_bias], axis=2).reshape(n_layers, 1, N_GATES)
    gate_bias = jnp.pad(gate_bias, ((0, 0), (0, 0), (0, LANES - N_GATES)))
    wr_pad = jnp.pad(w_router, ((0, 0), (0, 0), (0, LANES - N_EXPERTS)))
    wr_t = jnp.swapaxes(w_router, 1, 2)
    n_chunks = (t + tc) // S5_CHUNK
    n_levels = _s5_levels(n_chunks)

    for l in range(n_layers):
        with_ctx = l != n_layers - 1
        lam_init = 0.8 - 0.6 * math.exp(-0.3 * l)
        mod = mod_all[l]
        pa, pg = _project(xa, mod, wa[l], wg[l], dims)
        qh, kh, mq, mk = _prepare(pa, cos_t, sin_t, gq[l], gk[l], seg_ones, conv_w[l],
                                  mlstm_conv_b[l][None, :], dims)
        og = attn_out_gain[l][None, :]
        bound = 1.01 * DIFF_HEAD_DIM * jnp.max(jnp.abs(gq[l])) * jnp.max(jnp.abs(gk[l]))
        ya_l = _attention(qh, kh, pa, attn_lambda[l], og, bound, lam_init, dims, ctx_queries=False)
        if with_ctx:
            ya_c = _attention(qh, kh, pa, attn_lambda[l], og, bound, lam_init, dims, ctx_queries=True)
        else:
            ya_c = jnp.zeros((b * tc, BRANCH_W), BF16)
        ya = jnp.concatenate([ya_l, ya_c])
        mats = _s5_matrices(s5_lam_re[l], s5_lam_im[l], s5_log_step[l], s5_b_re[l], s5_b_im[l],
                            s5_c_re[l], s5_c_im[l], n_levels)
        y0, y1 = _s5_mixer(pa, mats, dims)
        hm = _mlstm(mq, mk, pa, pg, gate_bias[l], dims)
        x1, h2, aff, aff_t = _merge(ya, y0, y1, pa, hm, xa, mod, s5_d[l][None, :],
                                    s5_w_glu[l].astype(BF16), w_branch[l].astype(BF16),
                                    w_out[l].astype(BF16), wr_pad[l], wr_t[l], dims)
        wge, wue, wde = (w_exp_gate[l].astype(BF16), w_exp_up[l].astype(BF16), w_exp_down[l].astype(BF16))
        x2_l = _expert_choice(h2, aff, aff_t, x1, mod, wge, wue, wde, 0, b, t, lambda s: s)
        if with_ctx:
            x2_c = _expert_choice(h2, aff, aff_t, x1, mod, wge, wue, wde, b * t, b, tc, lambda s: b)
        else:
            x2_c = x1[b * t:]
        xa = jnp.concatenate([x2_l, x2_c])
    return xa[:b * t].reshape(b, t, d)
```

```python
import functools
import math

import jax
import jax.numpy as jnp
from jax import lax
from jax.experimental import pallas as pl
from jax.experimental.pallas import tpu as pltpu

F32 = jnp.float32
BF16 = jnp.bfloat16
HI = lax.Precision.HIGHEST

N_MOD = 6
NORM_EPS = 1e-6
GRID_W = 64
ROPE_BASE = 10000.0
N_HEADS = 4
DIFF_HEAD_DIM = 64
HEAD_W = 128
BRANCH_W = 512
S5_GROUPS = 32
S5_GROUP = 16
S5_STATE = 64
S5_CHUNK = 16
S5_SLAB = 8
S5_VMEM_LIMIT = 58 * 1024 * 1024
MLSTM_CHUNK = 128
CONV_K = 5
N_GATES = 16
N_EXPERTS = 16
CAPACITY_FACTOR = 2
LANES = 128
VMEM_LIMIT = 52 * 1024 * 1024
NEG_BIG = -1e30

SEG_Q, SEG_K, SEG_V, SEG_S5, SEG_MQ, SEG_MK, SEG_MV, SEG_MO, SEG_GATE = range(9)
PA_WIDTH = 8 * BRANCH_W + 3 * 1024


def _cparams(*sem):
    return pltpu.CompilerParams(dimension_semantics=sem, vmem_limit_bytes=VMEM_LIMIT)


def _mod_body(c_ref, w_ref, b_ref, o_ref):
    cv = c_ref[...]
    s = cv * jax.nn.sigmoid(cv)
    o_ref[0] = jnp.dot(s, w_ref[0], precision=HI, preferred_element_type=F32) + b_ref[0]


def _mod_vectors(cvec, w_mod, b_mod):
    n_layers, d, n = w_mod.shape
    tn = n // 4
    return pl.pallas_call(
        _mod_body,
        grid=(n_layers, n // tn),
        in_specs=[
            pl.BlockSpec((8, d), lambda l, j: (0, 0)),
            pl.BlockSpec((1, d, tn), lambda l, j: (l, 0, j)),
            pl.BlockSpec((1, 1, tn), lambda l, j: (l, 0, j)),
        ],
        out_specs=pl.BlockSpec((1, 8, tn), lambda l, j: (l, 0, j)),
        out_shape=jax.ShapeDtypeStruct((n_layers, 8, n), F32),
        compiler_params=_cparams("parallel", "parallel"),
        name="mod_vectors",
    )(cvec, w_mod, b_mod.reshape(n_layers, 1, n))


def _modulated_norm(x, mod, i_shift, i_scale):
    ms = jnp.mean(x * x, axis=-1, keepdims=True)
    xn = x * lax.rsqrt(ms + NORM_EPS)
    return xn * (1.0 + mod[i_scale:i_scale + 1, :]) + mod[i_shift:i_shift + 1, :]


def _proj_body(x_ref, m_ref, w_ref, wg_ref, pa_ref, pg_ref, *, tn):
    hb = _modulated_norm(x_ref[...], m_ref[0], 0, 1).astype(BF16)
    pg_ref[...] = jnp.dot(hb, wg_ref[...], preferred_element_type=F32)
    for c0 in range(0, pa_ref.shape[1], tn):
        pa_ref[:, c0:c0 + tn] = jnp.dot(hb, w_ref[:, c0:c0 + tn], preferred_element_type=F32).astype(BF16)


def _group_of_block(i, blocks_per_sample, n_samples):
    return jnp.minimum(i // blocks_per_sample, n_samples)


def _project(xa, mod, wa, wg, dims):
    b, t, tc, d = dims
    r = xa.shape[0]
    tm = 512
    npa = wa.shape[1]
    once = pl.Buffered(1)
    return pl.pallas_call(
        functools.partial(_proj_body, tn=1024),
        grid=(r // tm,),
        in_specs=[
            pl.BlockSpec((tm, d), lambda i: (i, 0)),
            pl.BlockSpec((1, N_MOD, d), lambda i: (_group_of_block(i, t // tm, b), 0, 0)),
            pl.BlockSpec((d, npa), lambda i: (0, 0), pipeline_mode=once),
            pl.BlockSpec((d, LANES), lambda i: (0, 0), pipeline_mode=once),
        ],
        out_specs=[
            pl.BlockSpec((tm, npa), lambda i: (i, 0)),
            pl.BlockSpec((tm, LANES), lambda i: (i, 0)),
        ],
        out_shape=[
            jax.ShapeDtypeStruct((r, npa), BF16),
            jax.ShapeDtypeStruct((r, LANES), F32),
        ],
        compiler_params=_cparams("parallel"),
        name="in_proj",
    )(xa, mod, wa, wg)


def _qk_norm_rope(x_bf, gain, cosf, sinf, seg_ones, first_half):
    x = x_bf.astype(F32)
    x2 = x * x
    hi = x2.astype(BF16)
    lo = (x2 - hi.astype(F32)).astype(BF16)
    ss = (jnp.dot(hi, seg_ones, preferred_element_type=F32)
          + jnp.dot(lo, seg_ones, preferred_element_type=F32))
    xn = x * lax.rsqrt(ss * (1.0 / DIFF_HEAD_DIM) + NORM_EPS) * gain
    half = DIFF_HEAD_DIM // 2
    width = x.shape[1]
    nxt = pltpu.roll(xn, width - half, 1)
    prv = pltpu.roll(xn, half, 1)
    partner = jnp.where(first_half, nxt, prv)
    return xn * cosf + partner * sinf


def _short_conv_silu(prev_ref, cur_ref, next_ref, w, bias, at_start, at_end, out_scale):
    tp = cur_ref.shape[0]
    prev = prev_ref[...].astype(F32)[8:16]
    nxt = next_ref[...].astype(F32)[0:8]
    prev = jnp.where(at_start, 0.0, prev)
    nxt = jnp.where(at_end, 0.0, nxt)
    ext = jnp.concatenate([prev, cur_ref[...].astype(F32), nxt], axis=0)
    acc = bias
    for kk in range(CONV_K):
        off = 8 + kk - CONV_K // 2
        acc = acc + w[kk:kk + 1, :] * ext[off:off + tp]
    y = acc * jax.nn.sigmoid(acc)
    return y * out_scale


def _prep_body(q_ref, k_ref, mqp_ref, mq_ref, mqn_ref, mkp_ref, mk_ref, mkn_ref,
               cos_ref, sin_ref, gq_ref, gk_ref, so_ref, cw_ref, cb_ref,
               qo_ref, ko_ref, mqo_ref, mko_ref, *, b, t, tc, tp):
    cos4 = jnp.concatenate([cos_ref[...]] * 4, axis=1)
    sin4 = jnp.concatenate([sin_ref[...]] * 4, axis=1)
    lane = lax.broadcasted_iota(jnp.int32, (tp, BRANCH_W), 1)
    first_half = (lane % DIFF_HEAD_DIM) < (DIFF_HEAD_DIM // 2)
    seg_ones = so_ref[...]
    qo_ref[...] = _qk_norm_rope(q_ref[...], gq_ref[...], cos4, sin4, seg_ones, first_half).astype(BF16)
    ko_ref[...] = _qk_norm_rope(k_ref[...], gk_ref[...], cos4, sin4, seg_ones, first_half).astype(BF16)

    row0 = pl.program_id(0) * tp
    in_lat = row0 < b * t
    local = jnp.where(in_lat, row0 % t, (row0 - b * t) % tc)
    seq_len = jnp.where(in_lat, t, tc)
    at_start = local == 0
    at_end = local + tp == seq_len
    cw = cw_ref[...]
    cb = cb_ref[...]
    mqo_ref[...] = _short_conv_silu(mqp_ref, mq_ref, mqn_ref, cw[:, :BRANCH_W], cb[:, :BRANCH_W],
                                    at_start, at_end, 1.0).astype(BF16)
    mko_ref[...] = _short_conv_silu(mkp_ref, mk_ref, mkn_ref, cw[:, BRANCH_W:], cb[:, BRANCH_W:],
                                    at_start, at_end, HEAD_W ** -0.5).astype(BF16)


def _prepare(pa, cos_tab, sin_tab, gq, gk, seg_ones, conv_w, conv_b, dims):
    b, t, tc, d = dims
    r = pa.shape[0]
    tp = 256
    halo = 16
    hb = tp // halo
    last_halo = r // halo - 1

    def cur(seg):
        return pl.BlockSpec((tp, BRANCH_W), lambda i: (i, seg))

    def prev(seg):
        return pl.BlockSpec((halo, BRANCH_W), lambda i: (jnp.maximum(i * hb - 1, 0), seg))

    def nxt(seg):
        return pl.BlockSpec((halo, BRANCH_W), lambda i: (jnp.minimum((i + 1) * hb, last_halo), seg))

    full = lambda shape: pl.BlockSpec(shape, lambda i: (0, 0))
    out = jax.ShapeDtypeStruct((r, BRANCH_W), BF16)
    return pl.pallas_call(
        functools.partial(_prep_body, b=b, t=t, tc=tc, tp=tp),
        grid=(r // tp,),
        in_specs=[
            cur(SEG_Q), cur(SEG_K),
            prev(SEG_MQ), cur(SEG_MQ), nxt(SEG_MQ),
            prev(SEG_MK), cur(SEG_MK), nxt(SEG_MK),
            pl.BlockSpec((tp, LANES), lambda i: (i, 0)),
            pl.BlockSpec((tp, LANES), lambda i: (i, 0)),
            full((1, BRANCH_W)), full((1, BRANCH_W)),
            full((BRANCH_W, BRANCH_W)),
            full((8, 2 * BRANCH_W)), full((1, 2 * BRANCH_W)),
        ],
        out_specs=[pl.BlockSpec((tp, BRANCH_W), lambda i: (i, 0))] * 4,
        out_shape=[out, out, out, out],
        compiler_params=_cparams("parallel"),
        name="row_prep",
    )(pa, pa, pa, pa, pa, pa, pa, pa, cos_tab, sin_tab, gq, gk, seg_ones, conv_w, conv_b)


ATTN_HEADS_PER_STEP = 4


def _attn_body(lam_ref, og_ref, sh_ref, q_ref, k_ref, v_ref, kc_ref, vc_ref, o_ref,
               qm_ref, m_ref, l_ref, acc_ref, *, lam_init, has_ctx, nk, fixed_shift):
    kj = pl.program_id(3)
    heads = range(ATTN_HEADS_PER_STEP)
    head_lanes = lambda hh: slice(hh * HEAD_W, (hh + 1) * HEAD_W)

    def process(k_all, v_all):
        for hh in heads:
            kb, vb = k_all[:, head_lanes(hh)], v_all[:, head_lanes(hh)]
            for mi in range(2):
                si = 2 * hh + mi
                s = jnp.dot(kb, qm_ref[si], preferred_element_type=F32)
                if fixed_shift:
                    p = jnp.exp2(s - sh_ref[0:1, 0:1])
                    l_ref[si] += jnp.sum(p, axis=0, keepdims=True)
                    acc_ref[si] += lax.dot_general(vb, p.astype(BF16), (((0,), (0,)), ((), ())),
                                                   preferred_element_type=F32)
                else:
                    m_old = m_ref[si]
                    m_new = jnp.maximum(m_old, jnp.max(s, axis=0, keepdims=True))
                    alpha = jnp.exp2(m_old - m_new)
                    p = jnp.exp2(s - m_new)
                    l_ref[si] = alpha * l_ref[si] + jnp.sum(p, axis=0, keepdims=True)
                    pv = lax.dot_general(vb, p.astype(BF16), (((0,), (0,)), ((), ())),
                                         preferred_element_type=F32)
                    acc_ref[si] = alpha * acc_ref[si] + pv
                    m_ref[si] = m_new

    @pl.when(kj == 0)
    def _():
        for hh in heads:
            qt = q_ref[:, head_lanes(hh)].astype(F32).T.astype(BF16)
            row = lax.broadcasted_iota(jnp.int32, qt.shape, 0)
            zero = jnp.zeros_like(qt)
            qm_ref[2 * hh] = jnp.where(row < DIFF_HEAD_DIM, qt, zero)
            qm_ref[2 * hh + 1] = jnp.where(row >= DIFF_HEAD_DIM, qt, zero)
        m_ref[...] = jnp.full(m_ref.shape, NEG_BIG, F32)
        l_ref[...] = jnp.zeros(l_ref.shape, F32)
        acc_ref[...] = jnp.zeros(acc_ref.shape, F32)
        if has_ctx:
            process(kc_ref[...], vc_ref[...])

    process(k_ref[...], v_ref[...])

    @pl.when(kj == nk - 1)
    def _():
        lv = lam_ref[...]
        lam = (jnp.exp(jnp.sum(lv[0:1] * lv[1:2], keepdims=True))
               - jnp.exp(jnp.sum(lv[2:3] * lv[3:4], keepdims=True)) + lam_init)
        for hh in heads:
            o = (acc_ref[2 * hh] / l_ref[2 * hh]
                 - lam * (acc_ref[2 * hh + 1] / l_ref[2 * hh + 1]))
            ms = jnp.mean(o * o, axis=0, keepdims=True)
            o = o * lax.rsqrt(ms + NORM_EPS)
            o_ref[:, head_lanes(hh)] = (o.T * (og_ref[...] * (1.0 - lam_init))).astype(BF16)


MAX_FIXED_SHIFT = 48.0


def _attention(qh, kh, pa, lam_vecs, out_gain, score_bound, lam_init, dims, *, ctx_queries):
    shift = jnp.full((1, LANES), score_bound, F32)
    run = lambda fixed: _attention_call(qh, kh, pa, lam_vecs, out_gain, shift, lam_init, dims,
                                        ctx_queries=ctx_queries, fixed_shift=fixed)
    return lax.cond(score_bound <= MAX_FIXED_SHIFT, lambda: run(True), lambda: run(False))


def _attention_call(qh, kh, pa, lam_vecs, out_gain, shift, lam_init, dims, *, ctx_queries, fixed_shift):
    b, t, tc, d = dims
    v_col = SEG_V * (BRANCH_W // HEAD_W)
    ctx_blk0 = (b * t) // tc
    if ctx_queries:
        tq = tk = tc
        nq, nk = 1, 1
        q_row = lambda bb, qi: ctx_blk0 + bb
        k_row = lambda bb, kj: ctx_blk0 + bb
        n_rows = b * tc
        o_row = lambda bb, qi: bb
    else:
        tq = min(2048, t)
        tk = min(1024, t)
        nq, nk = t // tq, t // tk
        q_row = lambda bb, qi: bb * nq + qi
        k_row = lambda bb, kj: bb * nk + kj
        n_rows = b * t
        o_row = q_row
    body = functools.partial(_attn_body, lam_init=lam_init, has_ctx=not ctx_queries, nk=nk,
                             fixed_shift=fixed_shift)
    hp = ATTN_HEADS_PER_STEP
    wb = hp * HEAD_W
    v_blk = SEG_V * BRANCH_W // wb
    return pl.pallas_call(
        body,
        grid=(b, N_HEADS // hp, nq, nk),
        in_specs=[
            pl.BlockSpec((4, DIFF_HEAD_DIM), lambda bb, h, qi, kj: (0, 0)),
            pl.BlockSpec((1, HEAD_W), lambda bb, h, qi, kj: (0, 0)),
            pl.BlockSpec((1, LANES), lambda bb, h, qi, kj: (0, 0)),
            pl.BlockSpec((tq, wb), lambda bb, h, qi, kj: (q_row(bb, qi), h)),
            pl.BlockSpec((tk, wb), lambda bb, h, qi, kj: (k_row(bb, kj), h)),
            pl.BlockSpec((tk, wb), lambda bb, h, qi, kj: (k_row(bb, kj), v_blk + h)),
            pl.BlockSpec((tc, wb), lambda bb, h, qi, kj: (ctx_blk0 + bb, h)),
            pl.BlockSpec((tc, wb), lambda bb, h, qi, kj: (ctx_blk0 + bb, v_blk + h)),
        ],
        out_specs=pl.BlockSpec((tq, wb), lambda bb, h, qi, kj: (o_row(bb, qi), h)),
        out_shape=jax.ShapeDtypeStruct((n_rows, BRANCH_W), BF16),
        scratch_shapes=[
            pltpu.VMEM((2 * hp, HEAD_W, tq), BF16),
            pltpu.VMEM((2 * hp, 1, tq), F32),
            pltpu.VMEM((2 * hp, 1, tq), F32),
            pltpu.VMEM((2 * hp, HEAD_W, tq), F32),
        ],
        compiler_params=_cparams("parallel", "parallel", "parallel", "arbitrary"),
        name=("diff_attn_ctx" if ctx_queries else "diff_attn") + ("_fixed" if fixed_shift else ""),
    )(lam_vecs, out_gain, shift, qh, kh, pa, kh, pa)


def _s5_matrices(lam_re, lam_im, log_step, b_re, b_im, c_re, c_im, n_levels):
    ll, hg, pp, gg = S5_CHUNK, S5_GROUP, S5_STATE, S5_GROUPS
    dt = jnp.exp(log_step)[:, :, None]
    lr, li = lam_re * dt, lam_im * dt

    def a_pow(tau):
        tau = tau.astype(F32)[:, None, None, None]
        mag = jnp.exp(lr * tau)
        return mag * jnp.cos(li * tau), mag * jnp.sin(li * tau)

    ar1, ai1 = a_pow(jnp.ones((1,)))
    nr, ni = ar1[0] - 1.0, ai1[0]
    den = lam_re * lam_re + lam_im * lam_im
    f_re = (nr * lam_re + ni * lam_im) / den
    f_im = (ni * lam_re - nr * lam_im) / den
    bb_re = f_re[..., None] * b_re - f_im[..., None] * b_im
    bb_im = f_re[..., None] * b_im + f_im[..., None] * b_re

    ar, ai = a_pow(jnp.arange(ll + 1))
    ca_re = c_re[None] * ar[:, :, :, None, :] - c_im[None] * ai[:, :, :, None, :]
    ca_im = c_re[None] * ai[:, :, :, None, :] + c_im[None] * ar[:, :, :, None, :]
    kk = (jnp.einsum('tdgop,dgph->tdgoh', ca_re, bb_re, precision=HI)
          - jnp.einsum('tdgop,dgph->tdgoh', ca_im, bb_im, precision=HI))
    ns, sl = gg // S5_SLAB, S5_SLAB
    n_state = sl * pp
    width = ll * LANES
    jj = jnp.arange(ll)
    grp = jnp.arange(sl)
    same_go = (grp[:, None] == jnp.arange(LANES)[None, :] // hg).astype(F32)
    same_gp = (grp[:, None] == jnp.arange(n_state)[None, :] // pp).astype(F32)

    kt = kk[:ll].transpose(1, 0, 2, 4, 3).reshape(2, ll, ns, sl, hg, hg)
    kt = kt.transpose(0, 2, 1, 4, 3, 5).reshape(2, ns, ll, hg, LANES)
    kbd = (kt[:, :, :, None] * same_go[None, None, None, :, None, :]).reshape(2, ns, ll, LANES, LANES)

    pw = jnp.stack([ll - 1 - jj, jj])
    sel = lambda a: jnp.stack([a[pw[0], 0], a[pw[1], 1]]).reshape(2, ll, ns, n_state).transpose(0, 2, 1, 3)
    s_re, s_im = sel(ar), sel(ai)
    lane_bb = lambda m: m.reshape(2, ns, sl, pp, hg).transpose(0, 1, 4, 2, 3).reshape(2, ns, hg, n_state)
    t_re, t_im = lane_bb(bb_re), lane_bb(bb_im)

    def in_slab(a, bmat, c, dmat, sign):
        v = a[:, :, :, None, :] * bmat[:, :, None, :, :] + sign * c[:, :, :, None, :] * dmat[:, :, None, :, :]
        v = v[:, :, :, None] * same_gp[None, None, None, :, None, :]
        return v.reshape(2, ns, width, n_state)

    po = jnp.stack([jj + 1, ll - jj])

    def out_slab_t(a):
        m = jnp.stack([a[po[0], 0], a[po[1], 1]])
        m = m.reshape(2, ll, ns, sl, hg, pp).transpose(0, 2, 1, 4, 3, 5).reshape(2, ns, ll, hg, n_state)
        v = m[:, :, :, None] * same_gp[None, None, None, :, None, :]
        return v.reshape(2, ns, width, n_state)

    lev = (ll * (2 ** jnp.arange(n_levels))).astype(F32)
    alr, ali = a_pow(lev)
    pad_lev = (-n_levels) % 8

    def lev_slab(a):
        a = a.transpose(1, 0, 2, 3).reshape(2, n_levels, ns, n_state).transpose(0, 2, 1, 3)
        return jnp.pad(a, ((0, 0), (0, 0), (0, pad_lev), (0, 0)))

    return dict(kbd=kbd.astype(BF16),
                in_re=in_slab(s_re, t_re, s_im, t_im, -1.0).astype(BF16),
                in_im=in_slab(s_re, t_im, s_im, t_re, 1.0).astype(BF16),
                out_re_t=out_slab_t(ca_re).astype(BF16), out_im_t=out_slab_t(-ca_im).astype(BF16),
                al_re=lev_slab(alr), al_im=lev_slab(ali))


def _s5_body(ul_ref, uc_ref, kbd_ref, inr_ref, ini_ref, outr_ref, outi_ref, alr_ref, ali_ref,
             yl_ref, yc_ref, sr_ref, si_ref, w_ref, *, nlat, nctx, pad, n_levels, rev):
    ll = S5_CHUNK
    nc = nlat + nctx

    @pl.when(pl.program_id(1) == 0)
    def _():
        zero = jnp.zeros((LANES, LANES), BF16)
        for j in range(ll):
            for i in range(ll):
                lag = j - i if rev else i - j
                w_ref[j * LANES:(j + 1) * LANES, i * LANES:(i + 1) * LANES] = (
                    kbd_ref[0, 0, lag] if lag >= 0 else zero)

    cat = lambda ref: jnp.concatenate([ref[j] for j in range(ll)], axis=1)
    ulat, uctx = cat(ul_ref), cat(uc_ref)
    u = jnp.concatenate([ulat, uctx] if rev else [uctx, ulat], axis=0)
    lat0, ctx0 = (0, nlat) if rev else (nctx, 0)
    lo = 0 if rev else pad
    zero0 = nc if rev else 0
    zeros = jnp.zeros((pad, sr_ref.shape[1]), F32)
    sr_ref[zero0:zero0 + pad, :] = zeros
    si_ref[zero0:zero0 + pad, :] = zeros
    sr_ref[lo:lo + nc, :] = jnp.dot(u, inr_ref[0, 0], preferred_element_type=F32)
    si_ref[lo:lo + nc, :] = jnp.dot(u, ini_ref[0, 0], preferred_element_type=F32)
    for lev in range(n_levels):
        dd = 1 << lev
        src = lo + dd if rev else lo - dd
        a_r = alr_ref[0, 0, lev:lev + 1, :]
        a_i = ali_ref[0, 0, lev:lev + 1, :]
        cr, ci = sr_ref[lo:lo + nc, :], si_ref[lo:lo + nc, :]
        pr, pi = sr_ref[src:src + nc, :], si_ref[src:src + nc, :]
        sr_ref[lo:lo + nc, :] = cr + a_r * pr - a_i * pi
        si_ref[lo:lo + nc, :] = ci + a_r * pi + a_i * pr
    ent = lo + 1 if rev else lo - 1
    er = sr_ref[ent:ent + nc, :].astype(BF16)
    ei = si_ref[ent:ent + nc, :].astype(BF16)
    nt = (((1,), (1,)), ((), ()))
    for ib in range(ll // 2):
        cols = slice(ib * 2 * LANES, (ib + 1) * 2 * LANES)
        y = (jnp.dot(u, w_ref[:, cols], preferred_element_type=F32)
             + lax.dot_general(er, outr_ref[0, 0, cols, :], nt, preferred_element_type=F32)
             + lax.dot_general(ei, outi_ref[0, 0, cols, :], nt, preferred_element_type=F32)).astype(BF16)
        for k in range(2):
            yl_ref[2 * ib + k] = y[lat0:lat0 + nlat, k * LANES:(k + 1) * LANES]
            yc_ref[2 * ib + k] = y[ctx0:ctx0 + nctx, k * LANES:(k + 1) * LANES]


def _s5_levels(nc):
    return max(1, (nc - 1).bit_length())


def _s5_scan(u3, mats, dims, *, rev):
    b, t, tc, d = dims
    ll = S5_CHUNK
    nlat, nctx = t // ll, tc // ll
    n_levels = _s5_levels(nlat + nctx)
    pad = max(8, 1 << (n_levels - 1))
    ns = S5_GROUPS // S5_SLAB
    n_state = S5_SLAB * S5_STATE
    width = ll * LANES
    lev_rows = mats["al_re"].shape[2]
    once = pl.Buffered(1)
    dd = int(rev)
    per_slab = lambda shape: pl.BlockSpec((1, 1) + shape, lambda s, bb: (dd, s) + (0,) * len(shape),
                                          pipeline_mode=once)
    return pl.pallas_call(
        functools.partial(_s5_body, nlat=nlat, nctx=nctx, pad=pad, n_levels=n_levels, rev=rev),
        grid=(ns, b),
        in_specs=[
            pl.BlockSpec((ll, nlat, LANES), lambda s, bb: (0, bb, s), pipeline_mode=once),
            pl.BlockSpec((ll, nctx, LANES), lambda s, bb: (0, b * t // tc + bb, s)),
            per_slab((ll, LANES, LANES)),
            per_slab((width, n_state)), per_slab((width, n_state)),
            per_slab((width, n_state)), per_slab((width, n_state)),
            per_slab((lev_rows, n_state)), per_slab((lev_rows, n_state)),
        ],
        out_specs=[
            pl.BlockSpec((ll, nlat, LANES), lambda s, bb: (0, bb, s)),
            pl.BlockSpec((ll, nctx, LANES), lambda s, bb: (0, bb, s)),
        ],
        out_shape=[
            jax.ShapeDtypeStruct((ll, b * nlat, BRANCH_W), BF16),
            jax.ShapeDtypeStruct((ll, b * nctx, BRANCH_W), BF16),
        ],
        scratch_shapes=[pltpu.VMEM((pad + nlat + nctx, n_state), F32)] * 2
        + [pltpu.VMEM((width, width), BF16)],
        compiler_params=pltpu.CompilerParams(dimension_semantics=("parallel", "arbitrary"),
                                             vmem_limit_bytes=S5_VMEM_LIMIT),
        name="s5_scan_bwd" if rev else "s5_scan_fwd",
    )(u3, u3, mats["kbd"], mats["in_re"], mats["in_im"], mats["out_re_t"], mats["out_im_t"],
      mats["al_re"], mats["al_im"])


def _s5_mixer(pa, mats, dims):
    r = pa.shape[0]
    ll = S5_CHUNK
    u = pa[:, SEG_S5 * BRANCH_W:(SEG_S5 + 1) * BRANCH_W]
    u3 = u.reshape(r // ll, ll, BRANCH_W).transpose(1, 0, 2)
    ys = []
    for dd in range(2):
        yl, yc = _s5_scan(u3, mats, dims, rev=bool(dd))
        ys.append(jnp.concatenate([yl, yc], axis=1).transpose(1, 0, 2).reshape(r, BRANCH_W))
    return ys


def _log_sigmoid(x):
    return -(jnp.maximum(-x, 0.0) + jnp.log1p(jnp.exp(-jnp.abs(x))))


def _mlstm_body(q_ref, k_ref, v_ref, g_ref, gb_ref, h_ref, c_ref, n_ref, m_ref):
    dd = pl.program_id(1)
    fwd = dd == 0

    @pl.when(pl.program_id(2) == 0)
    def _():
        c_ref[...] = jnp.zeros(c_ref.shape, F32)
        n_ref[...] = jnp.zeros(n_ref.shape, F32)
        m_ref[...] = jnp.zeros(m_ref.shape, F32)

    ll = MLSTM_CHUNK
    row = lax.broadcasted_iota(jnp.int32, (ll, ll), 0)
    col = lax.broadcasted_iota(jnp.int32, (ll, ll), 1)
    sign = jnp.where(fwd, 1, -1)
    before = (col - row) * sign >= 0
    before_f = jnp.where(before, 1.0, 0.0)
    after_f = jnp.where((row - col) * sign >= 0, 1.0, 0.0)

    g = g_ref[...] + gb_ref[...]
    lf = _log_sigmoid(g)
    g_t = g.T
    lf_t = lf.T
    cum_cols = jnp.dot(after_f, lf, precision=HI, preferred_element_type=F32)
    cum_rows = jnp.dot(lf_t, before_f, precision=HI, preferred_element_type=F32)
    nt = (((1,), (1,)), ((), ()))
    outs = []
    for h in range(N_HEADS):
        def pick_col(a, base):
            return jnp.where(fwd, a[:, base + h:base + h + 1], a[:, 8 + base + h:8 + base + h + 1])

        def pick_row(a, base):
            return jnp.where(fwd, a[base + h:base + h + 1, :], a[8 + base + h:8 + base + h + 1, :])

        ig_col, cum_col = pick_col(g, 0), pick_col(cum_cols, 4)
        ig_row, lf_row, cum_row = pick_row(g_t, 0), pick_row(lf_t, 4), pick_row(cum_rows, 4)
        gtot = jnp.sum(lf_row, axis=1, keepdims=True)

        hs = slice(h * HEAD_W, (h + 1) * HEAD_W)
        q = q_ref[:, hs]
        k = k_ref[:, hs]
        v_t = v_ref[:, hs].astype(F32).T.astype(BF16)
        c0 = c_ref[h]
        n0 = n_ref[h]
        m0 = m_ref[h][:, 0:1]

        dmat = jnp.where(before, cum_row + (ig_col - cum_col), NEG_BIG)
        inter = cum_row + m0
        m_t = jnp.maximum(inter, jnp.max(dmat, axis=0, keepdims=True))
        pm = jnp.exp(dmat - m_t)
        ei = jnp.exp(inter - m_t)
        kq = lax.dot_general(k, q, nt, preferred_element_type=F32)
        wq = pm * kq
        cq = lax.dot_general(c0.astype(BF16), q, nt, preferred_element_type=F32)
        num = jnp.dot(v_t, wq.astype(BF16), preferred_element_type=F32) + ei * cq
        nq = lax.dot_general(jnp.broadcast_to(n0, (8, HEAD_W)).astype(BF16), q, nt,
                             preferred_element_type=F32)[0:1]
        den = jnp.sum(wq, axis=0, keepdims=True) + ei * nq
        outs.append((num / jnp.maximum(jnp.abs(den), jnp.exp(-m_t))).T)

        w_row = gtot - cum_row + ig_row
        mw = jnp.max(w_row, axis=1, keepdims=True)
        ew = jnp.exp(w_row - mw)
        vw = (v_t.astype(F32) * ew).astype(BF16)
        kv = jnp.dot(vw, k, preferred_element_type=F32)
        ks = jnp.dot(jnp.broadcast_to(ew, (8, ll)).astype(BF16), k, preferred_element_type=F32)[0:1]
        m_new = jnp.maximum(gtot + m0, mw)
        a = jnp.exp(gtot + m0 - m_new)
        e = jnp.exp(mw - m_new)
        c_ref[h] = a * c0 + e * kv
        n_ref[h] = a * n0 + e * ks
        m_ref[h] = jnp.broadcast_to(m_new, (1, LANES))
    h_ref[0] = jnp.concatenate(outs, axis=1).astype(BF16)


def _mlstm(mq, mk, pa, pg, gate_bias, dims):
    b, t, tc, d = dims
    r = pa.shape[0]
    ll = MLSTM_CHUNK
    nctx, nlat = tc // ll, t // ll
    ctx0 = (b * t) // ll

    def rb(bb, dd, c):
        is_ctx = c < nctx
        cc = jnp.where(dd == 0, c, nctx - 1 - c)
        cl = jnp.where(dd == 0, c - nctx, nlat - 1 - (c - nctx))
        return jnp.where(is_ctx, ctx0 + bb * nctx + cc, bb * nlat + cl)

    return pl.pallas_call(
        _mlstm_body,
        grid=(b, 2, nctx + nlat),
        in_specs=[
            pl.BlockSpec((ll, BRANCH_W), lambda bb, dd, c: (rb(bb, dd, c), 0)),
            pl.BlockSpec((ll, BRANCH_W), lambda bb, dd, c: (rb(bb, dd, c), 0)),
            pl.BlockSpec((ll, BRANCH_W), lambda bb, dd, c: (rb(bb, dd, c), SEG_MV)),
            pl.BlockSpec((ll, LANES), lambda bb, dd, c: (rb(bb, dd, c), 0)),
            pl.BlockSpec((1, LANES), lambda bb, dd, c: (0, 0)),
        ],
        out_specs=pl.BlockSpec((1, ll, BRANCH_W), lambda bb, dd, c: (dd, rb(bb, dd, c), 0)),
        out_shape=jax.ShapeDtypeStruct((2, r, BRANCH_W), BF16),
        scratch_shapes=[
            pltpu.VMEM((N_HEADS, HEAD_W, HEAD_W), F32),
            pltpu.VMEM((N_HEADS, 1, HEAD_W), F32),
            pltpu.VMEM((N_HEADS, 1, LANES), F32),
        ],
        compiler_params=_cparams("parallel", "parallel", "arbitrary"),
        name="mlstm",
    )(mq, mk, pa, pg, gate_bias)


def _merge_body(ya_ref, y0_ref, y1_ref, u_ref, h0_ref, h1_ref, mo_ref, ga_ref, gb_ref, gc_ref, x_ref, m_ref,
                d_ref, wglu_ref, wb_ref, wo_ref, wr_ref, wrt_ref,
                x1_ref, h2_ref, aff_ref, afft_ref):
    mod = m_ref[0]
    ys = d_ref[...] * u_ref[...].astype(F32) + y0_ref[...].astype(F32) + y1_ref[...].astype(F32)
    gl = jax.nn.gelu(ys)
    yb = gl * jax.nn.sigmoid(jnp.dot(gl.astype(BF16), wglu_ref[...], preferred_element_type=F32))
    yc = (h0_ref[0].astype(F32) + h1_ref[0].astype(F32)) * jax.nn.sigmoid(mo_ref[...].astype(F32))
    gate = lambda ref: jax.nn.sigmoid(ref[...].astype(F32))
    mixed = (gate(ga_ref) * jnp.dot(ya_ref[...], wb_ref[0], preferred_element_type=F32)
             + gate(gb_ref) * jnp.dot(yb.astype(BF16), wb_ref[1], preferred_element_type=F32)
             + gate(gc_ref) * jnp.dot(yc.astype(BF16), wb_ref[2], preferred_element_type=F32))
    out = jnp.dot(mixed.astype(BF16), wo_ref[...], preferred_element_type=F32)
    x1 = x_ref[...] + mod[2:3, :] * out
    x1_ref[...] = x1
    h2 = _modulated_norm(x1, mod, 3, 4)
    h2_ref[...] = h2.astype(BF16)
    logits = jnp.dot(h2, wr_ref[...], precision=HI, preferred_element_type=F32)
    lane = lax.broadcasted_iota(jnp.int32, logits.shape, 1)
    logits = jnp.where(lane < N_EXPERTS, logits, NEG_BIG)
    ex = jnp.exp(logits - jnp.max(logits, axis=1, keepdims=True))
    aff_ref[...] = ex / jnp.sum(ex, axis=1, keepdims=True)
    lt = lax.dot_general(wrt_ref[...], h2, (((1,), (1,)), ((), ())), precision=HI,
                         preferred_element_type=F32)
    et = jnp.exp(lt - jnp.max(lt, axis=0, keepdims=True))
    afft_ref[...] = et / jnp.sum(et, axis=0, keepdims=True)


def _merge(ya, y0, y1, pa, hm, xa, mod, s5_d, w_glu, w_branch, w_out, wr_pad, wr_t, dims):
    b, t, tc, d = dims
    r = xa.shape[0]
    tm = 256
    gseg = SEG_GATE * BRANCH_W // d
    rowblk = lambda width, col=0: pl.BlockSpec((tm, width), lambda i: (i, col))
    full2 = lambda shape: pl.BlockSpec(shape, lambda i: (0, 0))
    return pl.pallas_call(
        _merge_body,
        grid=(r // tm,),
        in_specs=[
            rowblk(BRANCH_W), rowblk(BRANCH_W), rowblk(BRANCH_W),
            rowblk(BRANCH_W, SEG_S5),
            pl.BlockSpec((1, tm, BRANCH_W), lambda i: (0, i, 0)),
            pl.BlockSpec((1, tm, BRANCH_W), lambda i: (1, i, 0)),
            rowblk(BRANCH_W, SEG_MO),
            rowblk(d, gseg), rowblk(d, gseg + 1), rowblk(d, gseg + 2),
            rowblk(d),
            pl.BlockSpec((1, N_MOD, d), lambda i: (_group_of_block(i, t // tm, b), 0, 0)),
            full2((1, BRANCH_W)),
            full2((BRANCH_W, BRANCH_W)),
            pl.BlockSpec((3, BRANCH_W, d), lambda i: (0, 0, 0)),
            full2((d, d)),
            full2((d, LANES)),
            full2((N_EXPERTS, d)),
        ],
        out_specs=[
            rowblk(d), rowblk(d), rowblk(LANES),
            pl.BlockSpec((N_EXPERTS, tm), lambda i: (0, i)),
        ],
        out_shape=[
            jax.ShapeDtypeStruct((r, d), F32),
            jax.ShapeDtypeStruct((r, d), BF16),
            jax.ShapeDtypeStruct((r, LANES), F32),
            jax.ShapeDtypeStruct((N_EXPERTS, r), F32),
        ],
        compiler_params=_cparams("parallel"),
        name="merge",
    )(ya, y0, y1, pa, hm, hm, pa, pa, pa, pa, xa, mod, s5_d, w_glu, w_branch, w_out, wr_pad, wr_t)


def _route_body(a_ref, tri_ref, low_ref, pos_ref, offs_ref, *, cap):
    a = a_ref[0]
    e, nb, _ = a.shape
    bits = pltpu.bitcast(a, jnp.int32)

    def count(mask):
        c = jnp.sum(jnp.where(mask, 1.0, 0.0), axis=2, keepdims=True)
        return jnp.sum(c, axis=1, keepdims=True)

    def step(i, thr):
        cand = thr | jnp.left_shift(jnp.int32(1), 30 - i)
        return jnp.where(count(bits >= cand) >= cap, cand, thr)

    thr = lax.fori_loop(0, 31, step, jnp.zeros((e, 1, 1), jnp.int32))
    gt = bits > thr
    eq = bits == thr
    need = cap - count(gt)

    tri = tri_ref[...]
    low = low_ref[...]

    def exclusive_prefix(x):
        x2 = x.reshape(e * nb, LANES).astype(BF16)
        incl = jnp.dot(x2, tri, preferred_element_type=F32)
        before = jnp.sum(jnp.dot(low, x2, preferred_element_type=F32), axis=1, keepdims=True)
        return (incl - x2.astype(F32) + before).reshape(e, nb, LANES), before.reshape(e, nb, 1)

    eq_rank, _ = exclusive_prefix(jnp.where(eq, 1.0, 0.0))
    sel = gt | (eq & (eq_rank < need))
    pos, before = exclusive_prefix(jnp.where(sel, 1.0, 0.0))
    pos_ref[0] = jnp.where(sel, pos, -1.0)
    offs_ref[0] = before.astype(jnp.int32)


def _route(aff3, cap):
    ns, e, nb, _ = aff3.shape
    i = jnp.arange(LANES)
    tri = (i[:, None] <= i[None, :]).astype(BF16)
    r = jnp.arange(e * nb)
    low = ((r[:, None] // nb == r[None, :] // nb) & (r[None, :] < r[:, None])).astype(BF16)
    return pl.pallas_call(
        functools.partial(_route_body, cap=cap),
        grid=(ns,),
        in_specs=[
            pl.BlockSpec((1, e, nb, LANES), lambda s: (s, 0, 0, 0)),
            pl.BlockSpec((LANES, LANES), lambda s: (0, 0)),
            pl.BlockSpec((e * nb, e * nb), lambda s: (0, 0)),
        ],
        out_specs=[
            pl.BlockSpec((1, e, nb, LANES), lambda s: (s, 0, 0, 0)),
            pl.BlockSpec((1, e, nb, 1), lambda s: (s, 0, 0, 0)),
        ],
        out_shape=[
            jax.ShapeDtypeStruct((ns, e, nb, LANES), F32),
            jax.ShapeDtypeStruct((ns, e, nb, 1), jnp.int32),
        ],
        compiler_params=_cparams("parallel"),
        name="route",
    )(aff3, tri, low)


SLOT_ALIGN = 16


DISPATCH_EXPERTS = 2


def _dispatch_body(st_ref, h_ref, a_ref, p_ref, o_ref, oa_ref, *, tb, win, nj, sub):
    ns, eg, j = pl.program_id(0), pl.program_id(1), pl.program_id(2)

    @pl.when(j == 0)
    def _():
        o_ref[...] = jnp.zeros(o_ref.shape, o_ref.dtype)
        oa_ref[...] = jnp.zeros(oa_ref.shape, oa_ref.dtype)

    lane = lax.broadcasted_iota(jnp.int32, (tb, LANES), 1)
    for q in range(DISPATCH_EXPERTS):
        e = eg * DISPATCH_EXPERTS + q
        for k in range(sub):
            start = st_ref[(ns * N_EXPERTS + e) * (nj + 1) + j * sub + k]
            base = pl.multiple_of((start // SLOT_ALIGN) * SLOT_ALIGN, SLOT_ALIGN)
            slot = (base + lax.broadcasted_iota(jnp.int32, (win, tb), 0)).astype(F32)
            pos = p_ref[0, q, 0, :, k * tb:(k + 1) * tb]
            onehot = jnp.where(slot == pos, 1.0, 0.0).astype(BF16)
            rows = jnp.dot(onehot, h_ref[k * tb:(k + 1) * tb, :], preferred_element_type=F32)
            cur = o_ref[0, q, pl.ds(base, win), :]
            o_ref[0, q, pl.ds(base, win), :] = cur + rows.astype(o_ref.dtype)
            aff = jnp.where(lane == e, a_ref[k * tb:(k + 1) * tb, :], 0.0)
            hi = aff.astype(BF16)
            rest = aff - hi.astype(F32)
            mid = rest.astype(BF16)
            low = (rest - mid.astype(F32)).astype(BF16)
            arows = (jnp.dot(onehot, hi, preferred_element_type=F32)
                     + jnp.dot(onehot, mid, preferred_element_type=F32)
                     + jnp.dot(onehot, low, preferred_element_type=F32))
            oa_ref[0, q, pl.ds(base, win), :] = oa_ref[0, q, pl.ds(base, win), :] + arows


def _dispatch(starts, h2, aff, pos, row0, ns, n, cap, tb):
    d = h2.shape[1]
    nj = n // tb
    sub = min(8, nj)
    njs = nj // sub
    win = tb + SLOT_ALIGN
    capp = cap + win
    blk0 = row0 // (sub * tb)
    de = DISPATCH_EXPERTS
    pos_rows = pos.reshape(ns, N_EXPERTS, njs, 1, sub * tb)
    grid_spec = pltpu.PrefetchScalarGridSpec(
        num_scalar_prefetch=1,
        grid=(ns, N_EXPERTS // de, njs),
        in_specs=[
            pl.BlockSpec((sub * tb, d), lambda s, e, j, st: (blk0 + s * njs + j, 0)),
            pl.BlockSpec((sub * tb, LANES), lambda s, e, j, st: (blk0 + s * njs + j, 0)),
            pl.BlockSpec((1, de, 1, 1, sub * tb), lambda s, e, j, st: (s, e, j, 0, 0)),
        ],
        out_specs=[
            pl.BlockSpec((1, de, capp, d), lambda s, e, j, st: (s, e, 0, 0)),
            pl.BlockSpec((1, de, capp, LANES), lambda s, e, j, st: (s, e, 0, 0)),
        ],
    )
    return pl.pallas_call(
        functools.partial(_dispatch_body, tb=tb, win=win, nj=nj, sub=sub),
        grid_spec=grid_spec,
        out_shape=[
            jax.ShapeDtypeStruct((ns, N_EXPERTS, capp, d), BF16),
            jax.ShapeDtypeStruct((ns, N_EXPERTS, capp, LANES), F32),
        ],
        compiler_params=_cparams("parallel", "parallel", "arbitrary"),
        name="dispatch",
    )(starts, h2, aff, pos_rows)


def _expert_body(x_ref, a_ref, wg_ref, wu_ref, wd_ref, y_ref, *, fc):
    x = x_ref[0, 0]
    f = wg_ref.shape[2]
    acc = jnp.zeros((x.shape[0], wd_ref.shape[2]), F32)
    for f0 in range(0, f, fc):
        g = jnp.dot(x, wg_ref[0, :, f0:f0 + fc], preferred_element_type=F32)
        u = jnp.dot(x, wu_ref[0, :, f0:f0 + fc], preferred_element_type=F32)
        hid = (g * jax.nn.sigmoid(g) * u).astype(BF16)
        acc = acc + jnp.dot(hid, wd_ref[0, f0:f0 + fc, :], preferred_element_type=F32)
    aff = jnp.sum(a_ref[0, 0], axis=1, keepdims=True)
    y_ref[0, 0] = (acc * aff).astype(BF16)


def _experts(xs, xa, w_gate, w_up, w_down, cap):
    ns, e, _, d = xs.shape
    f = w_gate.shape[2]
    ts = min(512, cap)
    return pl.pallas_call(
        functools.partial(_expert_body, fc=min(512, f)),
        grid=(e, ns, cap // ts),
        in_specs=[
            pl.BlockSpec((1, 1, ts, d), lambda ee, s, i: (s, ee, i, 0)),
            pl.BlockSpec((1, 1, ts, LANES), lambda ee, s, i: (s, ee, i, 0)),
            pl.BlockSpec((1, d, f), lambda ee, s, i: (ee, 0, 0)),
            pl.BlockSpec((1, d, f), lambda ee, s, i: (ee, 0, 0)),
            pl.BlockSpec((1, f, d), lambda ee, s, i: (ee, 0, 0)),
        ],
        out_specs=pl.BlockSpec((1, 1, ts, d), lambda ee, s, i: (s, ee, i, 0)),
        out_shape=jax.ShapeDtypeStruct((ns, e, cap, d), BF16),
        compiler_params=_cparams("parallel", "parallel", "parallel"),
        name="experts",
    )(xs, xa, w_gate, w_up, w_down)


COMBINE_EXPERTS = 16


def _combine_body(st_ref, *refs, sb, nj, nwin):
    y_refs = refs[:COMBINE_EXPERTS * nwin]
    p_ref, x_ref, m_ref, o_ref = refs[COMBINE_EXPERTS * nwin:]
    ns, j, eg = pl.program_id(0), pl.program_id(1), pl.program_id(2)

    @pl.when(eg == 0)
    def _():
        o_ref[...] = jnp.zeros(o_ref.shape, F32)

    tb = o_ref.shape[0]
    lane = lax.broadcasted_iota(jnp.int32, (tb, LANES), 1)
    slot_iota = lax.broadcasted_iota(jnp.int32, (tb, sb), 1)

    def gathered(k, w):
        e = eg * COMBINE_EXPERTS + k
        start = st_ref[(ns * N_EXPERTS + e) * (nj + 1) + j]
        end = st_ref[(ns * N_EXPERTS + e) * (nj + 1) + j + 1]
        first = (start // sb + w) * sb

        def rows():
            pos = jnp.sum(jnp.where(lane == e, p_ref[...], 0.0), axis=1, keepdims=True)
            onehot = jnp.where(pos == (first + slot_iota).astype(F32), 1.0, 0.0).astype(BF16)
            return jnp.dot(onehot, y_refs[k * nwin + w][0, 0], preferred_element_type=F32)
        return rows, end > first

    acc = gathered(0, 0)[0]()
    for k in range(1, COMBINE_EXPERTS):
        acc = acc + gathered(k, 0)[0]()
    o_ref[...] += acc
    for k in range(COMBINE_EXPERTS):
        for w in range(1, nwin):
            rows, used = gathered(k, w)

            @pl.when(used)
            def _():
                o_ref[...] += rows()

    @pl.when(eg == N_EXPERTS // COMBINE_EXPERTS - 1)
    def _():
        o_ref[...] = x_ref[...] + m_ref[0, 5:6, :] * o_ref[...]


def _combine(starts, ys, pos_t, x1, mod, row0, ns, n, cap, tb, mod_group):
    d = x1.shape[1]
    nj = n // tb
    span = min(tb, cap)
    sb = min(LANES, cap)
    nwin = span // sb + 1
    nsb = cap // sb
    blk0 = row0 // tb

    def window(k, w):
        def index(s, j, eg, st):
            e = eg * COMBINE_EXPERTS + k
            start = st[(s * N_EXPERTS + e) * (nj + 1) + j]
            end = st[(s * N_EXPERTS + e) * (nj + 1) + j + 1]
            blk = start // sb + w
            if w == 0:
                return (s, e, jnp.minimum(blk, nsb - 1), 0)
            used = end > blk * sb
            return (jnp.where(used, s, 0), jnp.where(used, e, 0), jnp.where(used, blk, 0), 0)
        return pl.BlockSpec((1, 1, sb, d), index)

    row = lambda s, j, eg, st: (blk0 + s * nj + j, 0)
    windows = [window(k, w) for k in range(COMBINE_EXPERTS) for w in range(nwin)]
    grid_spec = pltpu.PrefetchScalarGridSpec(
        num_scalar_prefetch=1,
        grid=(ns, nj, N_EXPERTS // COMBINE_EXPERTS),
        in_specs=windows + [
            pl.BlockSpec((tb, LANES), row),
            pl.BlockSpec((tb, d), row),
            pl.BlockSpec((1, N_MOD, d), lambda s, j, eg, st: (mod_group(s), 0, 0)),
        ],
        out_specs=pl.BlockSpec((tb, d), lambda s, j, eg, st: (s * nj + j, 0)),
    )
    return pl.pallas_call(
        functools.partial(_combine_body, sb=sb, nj=nj, nwin=nwin),
        grid_spec=grid_spec,
        out_shape=jax.ShapeDtypeStruct((ns * n, d), F32),
        compiler_params=_cparams("parallel", "parallel", "arbitrary"),
        name="combine",
    )(starts, *([ys] * len(windows)), pos_t, x1, mod)


def _expert_choice(h2, aff, aff_t, x1, mod, w_gate, w_up, w_down, row0, ns, n, mod_group):
    r, d = h2.shape
    e = N_EXPERTS
    cap = CAPACITY_FACTOR * n // e
    tb = min(256, n)
    n_pad = max(n, 8 * LANES)
    a = aff_t[:, row0:row0 + ns * n].reshape(e, ns, n).transpose(1, 0, 2)
    if n_pad > n:
        a = jnp.concatenate([a, jnp.full((ns, e, n_pad - n), -1.0, F32)], axis=2)
    pos, offs = _route(a.reshape(ns, e, n_pad // LANES, LANES), cap)
    pos = pos.reshape(ns, e, n_pad)[:, :, :n]
    starts = offs.reshape(ns, e, n_pad // LANES)[:, :, :n // LANES:tb // LANES]
    starts = jnp.concatenate([starts, jnp.full((ns, e, 1), cap, jnp.int32)], axis=2).reshape(-1)
    xs, xa = _dispatch(starts, h2, aff, pos, row0, ns, n, cap, tb)
    ys = _experts(xs, xa, w_gate, w_up, w_down, cap)
    pos_t = jnp.pad(pos.transpose(0, 2, 1).reshape(ns * n, e), ((0, 0), (0, LANES - e)), constant_values=-1.0)
    pos_t = jnp.pad(pos_t, ((row0, r - row0 - ns * n), (0, 0)))
    return _combine(starts, ys, pos_t, x1, mod, row0, ns, n, cap, tb, mod_group)


def _rope_tables(b, t, tc):
    n_freq = DIFF_HEAD_DIM // 4
    inv_freq = ROPE_BASE ** (-jnp.arange(n_freq, dtype=F32) / n_freq)
    pos = jnp.arange(t)
    row = (pos // GRID_W).astype(F32)
    col = (pos % GRID_W).astype(F32)
    ang = jnp.concatenate([row[:, None] * inv_freq, col[:, None] * inv_freq], axis=-1)
    cos, sin = jnp.cos(ang), jnp.sin(ang)
    cos_seg = jnp.concatenate([cos, cos], axis=-1)
    sin_seg = jnp.concatenate([-sin, sin], axis=-1)
    cos_t = jnp.tile(cos_seg, (b, LANES // DIFF_HEAD_DIM))
    sin_t = jnp.tile(sin_seg, (b, LANES // DIFF_HEAD_DIM))
    cos_t = jnp.concatenate([cos_t, jnp.ones((b * tc, LANES), F32)])
    sin_t = jnp.concatenate([sin_t, jnp.zeros((b * tc, LANES), F32)])
    return cos_t, sin_t


def kernel(x, c, ctx, c_ctx, w_mod, b_mod, w_in, attn_q_gain, attn_k_gain, attn_lambda, attn_out_gain,
           s5_lam_re, s5_lam_im, s5_log_step, s5_b_re, s5_b_im, s5_c_re, s5_c_im, s5_d, s5_w_glu,
           mlstm_conv_w, mlstm_conv_b, mlstm_i_bias, mlstm_f_bias,
           w_branch, w_out, w_router, w_exp_gate, w_exp_up, w_exp_down):
    b, t, d = x.shape
    tc = ctx.shape[1]
    n_layers = w_mod.shape[0]
    dims = (b, t, tc, d)
    assert b + 1 <= 8 and t % 512 == 0 and tc % 256 == 0 and (b * tc) % 512 == 0

    xa = jnp.concatenate([x.reshape(b * t, d), ctx.reshape(b * tc, d)])
    cvec = jnp.zeros((8, d), F32).at[:b].set(c).at[b].set(c_ctx)
    mod_all = _mod_vectors(cvec, w_mod, b_mod).reshape(n_layers, 8, N_MOD, d)

    n_main = 8 * BRANCH_W
    wa = jnp.concatenate([w_in[:, :, :n_main], w_in[:, :, n_main + N_GATES:]], axis=2).astype(BF16)
    wg = jnp.pad(w_in[:, :, n_main:n_main + N_GATES], ((0, 0), (0, 0), (0, LANES - N_GATES))).astype(BF16)
    cos_t, sin_t = _rope_tables(b, t, tc)
    seg = jnp.arange(BRANCH_W) // DIFF_HEAD_DIM
    seg_ones = (seg[:, None] == seg[None, :]).astype(BF16)
    n_seg = BRANCH_W // DIFF_HEAD_DIM
    gq = jnp.tile(attn_q_gain, (1, n_seg))[:, None, :] * (DIFF_HEAD_DIM ** -0.5 * math.log2(math.e))
    gk = jnp.tile(attn_k_gain, (1, n_seg))[:, None, :]
    conv_w = jnp.pad(mlstm_conv_w, ((0, 0), (0, 8 - CONV_K), (0, 0)))
    gate_bias = jnp.stack([mlstm_i_bias, mlstm_f_bias], axis=2).reshape(n_layers, 1, N_GATES)
    gate_bias = jnp.pad(gate_bias, ((0, 0), (0, 0), (0, LANES - N_GATES)))
    wr_pad = jnp.pad(w_router, ((0, 0), (0, 0), (0, LANES - N_EXPERTS)))
    wr_t = jnp.swapaxes(w_router, 1, 2)
    n_chunks = (t + tc) // S5_CHUNK
    n_levels = _s5_levels(n_chunks)

    for l in range(n_layers):
        with_ctx = l != n_layers - 1
        lam_init = 0.8 - 0.6 * math.exp(-0.3 * l)
        mod = mod_all[l]
        pa, pg = _project(xa, mod, wa[l], wg[l], dims)
        qh, kh, mq, mk = _prepare(pa, cos_t, sin_t, gq[l], gk[l], seg_ones, conv_w[l],
                                  mlstm_conv_b[l][None, :], dims)
        og = attn_out_gain[l][None, :]
        bound = 1.01 * DIFF_HEAD_DIM * jnp.max(jnp.abs(gq[l])) * jnp.max(jnp.abs(gk[l]))
        ya_l = _attention(qh, kh, pa, attn_lambda[l], og, bound, lam_init, dims, ctx_queries=False)
        if with_ctx:
            ya_c = _attention(qh, kh, pa, attn_lambda[l], og, bound, lam_init, dims, ctx_queries=True)
        else:
            ya_c = jnp.zeros((b * tc, BRANCH_W), BF16)
        ya = jnp.concatenate([ya_l, ya_c])
        mats = _s5_matrices(s5_lam_re[l], s5_lam_im[l], s5_log_step[l], s5_b_re[l], s5_b_im[l],
                            s5_c_re[l], s5_c_im[l], n_levels)
        y0, y1 = _s5_mixer(pa, mats, dims)
        hm = _mlstm(mq, mk, pa, pg, gate_bias[l], dims)
        x1, h2, aff, aff_t = _merge(ya, y0, y1, pa, hm, xa, mod, s5_d[l][None, :],
                                    s5_w_glu[l].astype(BF16), w_branch[l].astype(BF16),
                                    w_out[l].astype(BF16), wr_pad[l], wr_t[l], dims)
        wge, wue, wde = (w_exp_gate[l].astype(BF16), w_exp_up[l].astype(BF16), w_exp_down[l].astype(BF16))
        x2_l = _expert_choice(h2, aff, aff_t, x1, mod, wge, wue, wde, 0, b, t, lambda s: s)
        if with_ctx:
            x2_c = _expert_choice(h2, aff, aff_t, x1, mod, wge, wue, wde, b * t, b, tc, lambda s: b)
        else:
            x2_c = x1[b * t:]
        xa = jnp.concatenate([x2_l, x2_c])
    return xa[:b * t].reshape(b, t, d)
```
